```python
import math, functools
import jax, jax.numpy as jnp
from jax import lax
import numpy as np

D_MODEL = 1024
BATCH = 8
SEQ = 2048
DEPTH = 1
DEC_BATCH = 128
DEC_SEQ = 8
PAST_LEN = 16384
PAGE_SIZE = 128

D_INNER = 2 * D_MODEL
D_POOL = D_INNER // 2
D_MLSTM = D_INNER - D_POOL
POOL_WINDOWS = (2, 4, 8, 16)
N_POOL_GROUPS = len(POOL_WINDOWS)
POOL_GROUP = D_POOL // N_POOL_GROUPS
POOL_BUF = max(POOL_WINDOWS) - 1
N_HEADS = 4
HEAD_DIM = D_MLSTM // N_HEADS
CHUNK = 64
N_META = 16
EPS = 1e-6
D_PROJ = 2 * D_POOL + 5 * D_MLSTM + 2 * N_HEADS

kernel_name = "hymba_pool_mlstm_decoder_step"


def rms_norm(x, w):
    xf = x.astype(jnp.float32)
    y = xf * lax.rsqrt(jnp.mean(xf * xf, axis=-1, keepdims=True) + EPS) * w.astype(jnp.float32)
    return y.astype(x.dtype)


def causal_multiscale_pool(u, prev, pos0):
    B, T, C = u.shape
    P = POOL_BUF
    ext = jnp.concatenate([prev.astype(jnp.float32), u.astype(jnp.float32)], axis=1)
    cs0 = jnp.concatenate([jnp.zeros((B, 1, C), jnp.float32), jnp.cumsum(ext, axis=1)], axis=1)
    end = cs0[:, P + 1:]
    pos = pos0 + jnp.arange(T)
    cur = ext[:, P:]
    outs = []
    for g, w in enumerate(POOL_WINDOWS):
        sl = slice(g * POOL_GROUP, (g + 1) * POOL_GROUP)
        start = cs0[:, P + 1 - w: P + 1 - w + T, sl]
        cnt = jnp.minimum(w, pos + 1).astype(jnp.float32)
        outs.append((end[..., sl] - start) / cnt[None, :, None] - cur[..., sl])
    return jnp.concatenate(outs, axis=-1), ext[:, -P:]


def mlstm_chunk(state, inp):
    C, n, m = state
    q, k, v, ig, lf = inp
    L = q.shape[2]
    b = jnp.cumsum(lf, axis=-1)
    causal = jnp.tril(jnp.ones((L, L), dtype=bool))
    D = jnp.where(causal, b[..., :, None] - b[..., None, :] + ig[..., None, :], -jnp.inf)
    a = b + m[..., None]
    m_t = jnp.maximum(a, jnp.max(D, axis=-1))
    W = jnp.exp(D - m_t[..., None])
    inter = jnp.exp(a - m_t)
    s = jnp.einsum('bhtd,bhsd->bhts', q, k) * W
    num = inter[..., None] * jnp.einsum('bhtd,bhde->bhte', q, C) + jnp.einsum('bhts,bhse->bhte', s, v)
    qn = inter * jnp.einsum('bhtd,bhd->bht', q, n) + jnp.sum(s, axis=-1)
    h = num / jnp.maximum(jnp.abs(qn), jnp.exp(-m_t))[..., None]
    m_new = m_t[..., -1]
    w_end = jnp.exp(b[..., -1:] - b + ig - m_new[..., None])
    decay = jnp.exp(b[..., -1] + m - m_new)
    C_new = decay[..., None, None] * C + jnp.einsum('bhs,bhsd,bhse->bhde', w_end, k, v)
    n_new = decay[..., None] * n + jnp.einsum('bhs,bhsd->bhd', w_end, k)
    return (C_new, n_new, m_new), h


def mlstm_prompt(q, k, v, ig, lf):
    B, H, T, _ = q.shape
    state = (jnp.zeros((B, H, HEAD_DIM, HEAD_DIM), jnp.float32),
             jnp.zeros((B, H, HEAD_DIM), jnp.float32),
             jnp.zeros((B, H), jnp.float32))
    state, h_meta = mlstm_chunk(state, tuple(a[:, :, :N_META] for a in (q, k, v, ig, lf)))
    nc = (T - N_META) // CHUNK

    def to_chunks(a):
        a = a[:, :, N_META:]
        return jnp.moveaxis(a.reshape(B, H, nc, CHUNK, *a.shape[3:]), 2, 0)

    state, h_c = lax.scan(mlstm_chunk, state, tuple(to_chunks(a) for a in (q, k, v, ig, lf)))
    h_c = jnp.moveaxis(h_c, 0, 2).reshape(B, H, T - N_META, HEAD_DIM)
    return state, jnp.concatenate([h_meta, h_c], axis=2)


def mlstm_sample(state, q, k, v, ig, lf):
    state = tuple(s.astype(jnp.float32) for s in state)
    return mlstm_chunk(state, (q, k, v, ig, lf))


def mixer_layer(h, pool_prev, pos0, mlstm_run, norm_w, w_in, b_if, w_pool, pool_scale, mhln_w, w_out):
    B, T, _ = h.shape
    f32 = jnp.float32
    xn = rms_norm(h, norm_w)
    proj = xn @ w_in
    cuts = np.cumsum([D_POOL, D_POOL, D_MLSTM, D_MLSTM, D_MLSTM, D_MLSTM, D_MLSTM, N_HEADS])
    u_a, z_a, q, k, v, o, z_b, i_pre, f_pre = jnp.split(proj, cuts, axis=-1)
    pooled, pool_rows = causal_multiscale_pool(u_a, pool_prev, pos0)
    mixed = jnp.einsum('btgc,gcd->btgd', pooled.reshape(B, T, N_POOL_GROUPS, POOL_GROUP),
                       w_pool.astype(f32)).reshape(B, T, D_POOL)
    y_a = mixed * pool_scale.astype(f32) * jax.nn.silu(z_a.astype(f32))
    def heads(a):
        return a.astype(f32).reshape(B, T, N_HEADS, HEAD_DIM).transpose(0, 2, 1, 3)
    qh, kh, vh = heads(q), heads(k) * (HEAD_DIM ** -0.5), heads(v)
    gb = b_if.astype(f32)
    ig = (i_pre.astype(f32) + gb[:N_HEADS]).transpose(0, 2, 1)
    lf = jax.nn.log_sigmoid(f_pre.astype(f32) + gb[N_HEADS:]).transpose(0, 2, 1)
    state, ht = mlstm_run(qh, kh, vh, ig, lf)
    mu = jnp.mean(ht, axis=-1, keepdims=True)
    var = jnp.mean(jnp.square(ht - mu), axis=-1, keepdims=True)
    hn = (ht - mu) * lax.rsqrt(var + EPS) * mhln_w.astype(f32)[None, :, None, :]
    hn = hn.transpose(0, 2, 1, 3).reshape(B, T, D_MLSTM)
    y_b = hn * jax.nn.sigmoid(o.astype(f32)) * jax.nn.silu(z_b.astype(f32))
    y = jnp.concatenate([y_a, y_b], axis=-1).astype(h.dtype) @ w_out
    return h + y, pool_rows, state


def setup_inputs(seed: int = 0) -> dict:
    key = jax.random.key(seed)
    ks = jax.random.split(key, 18)
    nrm = jax.random.normal
    G = POOL_GROUP
    b_i = 0.1 * nrm(ks[10], (DEPTH, N_HEADS))
    b_f = jnp.linspace(3.0, 6.0, N_HEADS)[None, :] + 0.1 * nrm(ks[11], (DEPTH, N_HEADS))
    return {
        "x_prompt": nrm(ks[0], (BATCH, SEQ, D_MODEL), jnp.float32),
        "x_sample": nrm(ks[1], (DEC_BATCH, DEC_SEQ, D_MODEL), jnp.float32),
        "state_pool": nrm(ks[2], (DEPTH, DEC_BATCH, POOL_BUF, D_POOL), jnp.float32),
        "state_C": 0.1 * nrm(ks[3], (DEPTH, DEC_BATCH, N_HEADS, HEAD_DIM, HEAD_DIM), jnp.float32),
        "state_n": 0.1 * nrm(ks[4], (DEPTH, DEC_BATCH, N_HEADS, HEAD_DIM), jnp.float32),
        "state_m": nrm(ks[5], (DEPTH, DEC_BATCH, N_HEADS), jnp.float32),
        "meta_tokens": nrm(ks[6], (N_META, D_MODEL), jnp.float32),
        "norm1_w": 1.0 + 0.1 * nrm(ks[7], (DEPTH, D_MODEL), jnp.float32),
        "w_in": nrm(ks[8], (DEPTH, D_MODEL, D_PROJ), jnp.float32) * D_MODEL ** -0.5,
        "b_if": jnp.concatenate([b_i, b_f], axis=-1).astype(jnp.float32),
        "w_pool": nrm(ks[12], (DEPTH, N_POOL_GROUPS, G, G), jnp.float32) * G ** -0.5,
        "pool_scale": 1.0 + 0.1 * nrm(ks[13], (DEPTH, D_POOL), jnp.float32),
        "mhln_w": 1.0 + 0.1 * nrm(ks[14], (DEPTH, N_HEADS, HEAD_DIM), jnp.float32),
        "w_out": nrm(ks[15], (DEPTH, D_INNER, D_MODEL), jnp.float32) * D_INNER ** -0.5,
        "normf_w": 1.0 + 0.1 * nrm(ks[16], (D_MODEL,), jnp.float32),
    }


def reference(x_prompt, x_sample, state_pool, state_C, state_n, state_m, meta_tokens, norm1_w, w_in,
              b_if, w_pool, pool_scale, mhln_w, w_out, normf_w):
    Bp = x_prompt.shape[0]
    meta = jnp.broadcast_to(meta_tokens.astype(x_prompt.dtype)[None], (Bp, N_META, D_MODEL))
    hp = jnp.concatenate([meta, x_prompt], axis=1)
    hs = x_sample
    pool_p, C_p, n_p, m_p = [], [], [], []
    pool_s, C_s, n_s, m_s = [], [], [], []
    for l in range(DEPTH):
        w = (norm1_w[l], w_in[l], b_if[l], w_pool[l], pool_scale[l], mhln_w[l], w_out[l])
        hp, pr, (cp, np_, mp) = mixer_layer(hp, jnp.zeros((Bp, POOL_BUF, D_POOL), hp.dtype), 0,
                                            mlstm_prompt, *w)
        run_s = functools.partial(mlstm_sample, (state_C[l], state_n[l], state_m[l]))
        hs, ps, (cs, ns, ms) = mixer_layer(hs, state_pool[l], PAST_LEN, run_s, *w)
        pool_p.append(pr); C_p.append(cp); n_p.append(np_); m_p.append(mp)
        pool_s.append(ps); C_s.append(cs); n_s.append(ns); m_s.append(ms)
    y_prompt = rms_norm(hp, normf_w)[:, N_META:]
    y_sample = rms_norm(hs, normf_w)
    return (y_prompt, y_sample,
            jnp.stack(pool_p), jnp.stack(C_p), jnp.stack(n_p), jnp.stack(m_p),
            jnp.stack(pool_s), jnp.stack(C_s), jnp.stack(n_s), jnp.stack(m_s))
```

```python
import functools

import jax
import jax.numpy as jnp
from jax import lax
from jax.experimental import pallas as pl
from jax.experimental.pallas import tpu as pltpu

D_MODEL = 1024
D_POOL = 1024
D_MLSTM = 1024
N_HEADS = 4
HEAD_DIM = 256
POOL_WINDOWS = (2, 4, 8, 16)
POOL_GROUP = 256
POOL_BUF = 15
HIST = 16
N_META = 16
EPS = 1e-6
D_MAIN = 2 * D_POOL + 5 * D_MLSTM
GATE_PAD = 128
GATE_ROWS = 16
K_SCALE = HEAD_DIM ** -0.5

OFF_U, OFF_ZA, OFF_Q, OFF_K, OFF_V, OFF_O, OFF_ZB = (i * 1024 for i in range(7))

PROMPT_TILE = 256
SAMPLE_GROUP = 8
VMEM_LIMIT = 56 * 1024 * 1024

F32 = jnp.float32
BF16 = jnp.bfloat16


def _dot(a, b):
    return jnp.dot(a, b, preferred_element_type=F32)


def _dot_nt(a, b):
    return lax.dot_general(a, b, (((1,), (1,)), ((), ())), preferred_element_type=F32)


def _dot_tn(a, b):
    return lax.dot_general(a, b, (((0,), (0,)), ((), ())), preferred_element_type=F32)


def _rms(x, w):
    return x * lax.rsqrt(jnp.mean(x * x, axis=-1, keepdims=True) + EPS) * w


def _log_sigmoid(x):
    return jnp.minimum(x, 0.0) - jnp.log1p(jnp.exp(-jnp.abs(x)))


def _split3(x):
    hi = x.astype(BF16)
    r = x - hi.astype(F32)
    mid = r.astype(BF16)
    lo = (r - mid.astype(F32)).astype(BF16)
    return hi, mid, lo


def _seq_masks(T, seq_len):
    row = lax.broadcasted_iota(jnp.int32, (T, T), 0)
    col = lax.broadcasted_iota(jnp.int32, (T, T), 1)
    causal, causal_t = col <= row, row <= col
    if seq_len < T:
        shift = seq_len.bit_length() - 1
        assert 1 << shift == seq_len
        same = (row >> shift) == (col >> shift)
        causal, causal_t = causal & same, causal_t & same
    return causal, causal_t


def _gates(xn, wg_ref, wgt_ref, brow_ref, bcol_ref, causal, causal_t):
    g_col = _dot(xn, wg_ref[...]) + brow_ref[...]
    g_row = _dot_nt(wgt_ref[...], xn) + bcol_ref[:, 0:1]
    lf_col = _log_sigmoid(g_col)
    lf_row = _log_sigmoid(g_row)
    tri = jnp.where(causal, 1.0, 0.0).astype(BF16)
    tri_t = jnp.where(causal_t, 1.0, 0.0).astype(BF16)
    b_col = sum(_dot(tri, p) for p in _split3(lf_col))
    b_row = sum(_dot(p, tri_t) for p in _split3(lf_row))
    return g_col, g_row, b_col, b_row


def _intra(qb, kb, vb, b_col, a_col, r_row, causal):
    dm = jnp.where(causal, b_col + r_row, -jnp.inf)
    m_t = jnp.maximum(a_col, jnp.max(dm, axis=-1, keepdims=True))
    w = jnp.exp(dm - m_t)
    inter = jnp.exp(a_col - m_t)
    s = _dot_nt(qb, kb) * (w * K_SCALE)
    return m_t, inter, _dot(s.astype(BF16), vb), jnp.sum(s, axis=-1, keepdims=True)


def _head_norm(ht, w_row):
    mu = jnp.mean(ht, axis=-1, keepdims=True)
    d = ht - mu
    var = jnp.mean(d * d, axis=-1, keepdims=True)
    return d * lax.rsqrt(var + EPS) * w_row


def _window_sums(ext, T):
    s2 = ext + pltpu.roll(ext, 1, axis=0)
    s4 = s2[:, 256:] + pltpu.roll(s2[:, 256:], 2, axis=0)
    s8 = s4[:, 256:] + pltpu.roll(s4[:, 256:], 4, axis=0)
    s16 = s8[:, 256:] + pltpu.roll(s8[:, 256:], 8, axis=0)
    return [s2[HIST:, 0:256], s4[HIST:, 0:256], s8[HIST:, 0:256], s16[HIST:, 0:256]]


def _pool_mix(ext, u, pos_col, wpool_ref):
    T = u.shape[0]
    sums = _window_sums(ext, T)
    outs = []
    for g, w in enumerate(POOL_WINDOWS):
        inv_cnt = 1.0 / jnp.minimum(float(w), pos_col + 1.0)
        pooled = sums[g] * inv_cnt - u[:, g * 256:(g + 1) * 256]
        outs.append(_dot(pooled.astype(BF16), wpool_ref[g]))
    return jnp.concatenate(outs, axis=-1)


def _silu(x):
    return x * jax.nn.sigmoid(x)


def _seq_tile(x, pos0, w, proj_scr, ycat_scr, hist_scr, c_scr, n_scr, m_scr):
    (wmain_ref, wg_ref, wgt_ref, brow_ref, bcol_ref, norm1_ref, wpool_ref, pscale_ref, mhln_ref,
     wout_ref, normf_ref) = w
    T = x.shape[0]
    xn = _rms(x, norm1_ref[...]).astype(BF16)
    for seg in range(7):
        cols = slice(seg * 1024, (seg + 1) * 1024)
        proj_scr[0:T, cols] = _dot(xn, wmain_ref[:, cols])
    causal, causal_t = _seq_masks(T, T)
    g_col, g_row, b_col_all, b_row_all = _gates(xn, wg_ref, wgt_ref, brow_ref, bcol_ref, causal, causal_t)

    u = proj_scr[0:T, OFF_U:OFF_U + 1024]
    ext = jnp.concatenate([hist_scr[...], u], axis=0)
    pos_col = (lax.broadcasted_iota(jnp.int32, (T, 1), 0) + pos0).astype(F32)
    mixed = _pool_mix(ext, u, pos_col, wpool_ref)
    hist_scr[...] = ext[T:T + HIST, :]
    y_a = mixed * pscale_ref[...] * _silu(proj_scr[0:T, OFF_ZA:OFF_ZA + 1024])
    ycat_scr[0:T, 0:1024] = y_a.astype(BF16)

    for h in range(N_HEADS):
        hc = slice(h * 256, (h + 1) * 256)
        q = proj_scr[0:T, OFF_Q + h * 256:OFF_Q + (h + 1) * 256]
        k = proj_scr[0:T, OFF_K + h * 256:OFF_K + (h + 1) * 256]
        v = proj_scr[0:T, OFF_V + h * 256:OFF_V + (h + 1) * 256]
        qb, kb, vb = q.astype(BF16), k.astype(BF16), v.astype(BF16)
        ig_col = g_col[:, h:h + 1]
        b_col = b_col_all[:, 4 + h:5 + h]
        r_row = g_row[h:h + 1, :] - b_row_all[4 + h:5 + h, :]
        m_prev = m_scr[h:h + 1, 0:1]
        a_col = b_col + m_prev
        m_t, inter, sv, s_sum = _intra(qb, kb, vb, b_col, a_col, r_row, causal)
        c_old = c_scr[h]
        n_old = n_scr[h:h + 1, :]
        num = inter * _dot(qb, c_old.astype(BF16)) + sv
        qn = inter * jnp.sum(q * n_old, axis=-1, keepdims=True) + s_sum
        ht = num * (1.0 / jnp.maximum(jnp.abs(qn), jnp.exp(-m_t)))
        m_new = m_t[T - 1:T, :]
        b_last = b_col[T - 1:T, :]
        w_end = jnp.exp(b_last - b_col + ig_col - m_new) * K_SCALE
        decay = jnp.exp(b_last + m_prev - m_new)
        kw = k * w_end
        c_scr[h] = decay * c_old + _dot_tn(kw.astype(BF16), vb)
        n_scr[h:h + 1, :] = decay * n_old + jnp.sum(kw, axis=0, keepdims=True)
        m_scr[h:h + 1, :] = jnp.broadcast_to(m_new, (1, 128))
        hn = _head_norm(ht, mhln_ref[:, hc])
        o = proj_scr[0:T, OFF_O + h * 256:OFF_O + (h + 1) * 256]
        zb = proj_scr[0:T, OFF_ZB + h * 256:OFF_ZB + (h + 1) * 256]
        ycat_scr[0:T, 1024 + h * 256:1024 + (h + 1) * 256] = (hn * jax.nn.sigmoid(o) * _silu(zb)).astype(BF16)

    y = _dot(ycat_scr[0:T, :], wout_ref[...])
    return _rms(x + y, normf_ref[...])


def _meta_kernel(meta_ref, *refs):
    w = refs[:11]
    c_out, n_out, m_out, hist_out = refs[11:15]
    proj_scr, ycat_scr = refs[15:17]
    c_out[...] = jnp.zeros_like(c_out)
    n_out[...] = jnp.zeros_like(n_out)
    m_out[...] = jnp.zeros_like(m_out)
    hist_out[...] = jnp.zeros_like(hist_out)
    _seq_tile(meta_ref[...], 0, w, proj_scr, ycat_scr, hist_out, c_out, n_out, m_out)


def _prompt_kernel(x_ref, c0_ref, n0_ref, m0_ref, hist0_ref, *refs):
    w = refs[:11]
    y_ref, pool_out, c_out, n_out, m_out = refs[11:16]
    proj_scr, ycat_scr, hist_scr, c_scr, n_scr, m_scr = refs[16:22]
    b = pl.program_id(0)
    t = pl.program_id(1)

    @pl.when(t == 0)
    def _():
        c_scr[...] = c0_ref[...]
        n_scr[...] = n0_ref[...]
        m_scr[...] = m0_ref[...]
        hist_scr[...] = hist0_ref[...]

    y_ref[0] = _seq_tile(x_ref[0], N_META, w, proj_scr, ycat_scr, hist_scr, c_scr, n_scr, m_scr)

    @pl.when(t == pl.num_programs(1) - 1)
    def _():
        pool_out[0, 0] = hist_scr[1:HIST, :]
        c_out[0, 0] = c_scr[...]
        n_out[0, 0] = n_scr[0:N_HEADS, :]
        m_out[pl.ds(b, 1), :] = _m_row(m_scr)


def _m_row(m_scr):
    lane = lax.broadcasted_iota(jnp.int32, (1, 128), 1)
    row = jnp.zeros((1, 128), F32)
    for h in range(N_HEADS):
        row = jnp.where(lane == h, m_scr[h:h + 1, :], row)
    return row


def _sproj_kernel(x_ref, norm1_ref, wmain_ref, proj_ref):
    xn = _rms(x_ref[...], norm1_ref[...]).astype(BF16)
    for seg in range(7):
        cols = slice(seg * 1024, (seg + 1) * 1024)
        proj_ref[:, cols] = _dot(xn, wmain_ref[:, cols])


def _expand(seq_col, vals):
    out = None
    for i, val in enumerate(vals):
        pick = jnp.where(seq_col == i, val, 0.0)
        out = pick if out is None else out + pick
    return out


def _sample_kernel(x_ref, proj_ref, pool_ref, c_ref, n_ref, m_ref, *refs, seq_len, pos0):
    (wg_ref, wgt_ref, brow_ref, bcol_ref, norm1_ref, wpool_ref, pscale_ref, mhln_ref, wout_ref,
     normf_ref) = refs[:10]
    y_ref, pool_out, c_out, n_out, m_out = refs[10:15]
    ext_scr, ycat_scr = refs[15:17]
    G = SAMPLE_GROUP
    T = G * seq_len
    x = x_ref[...]
    xn = _rms(x, norm1_ref[...]).astype(BF16)
    causal, causal_t = _seq_masks(T, seq_len)
    g_col, g_row, b_col_all, b_row_all = _gates(xn, wg_ref, wgt_ref, brow_ref, bcol_ref, causal, causal_t)
    seq_col = lax.broadcasted_iota(jnp.int32, (T, 1), 0) >> (seq_len.bit_length() - 1)
    pos_col = jnp.full((seq_len, 1), float(pos0), F32) + lax.broadcasted_iota(
        jnp.int32, (seq_len, 1), 0).astype(F32)

    mixed_rows = []
    for i in range(G):
        rows = slice(i * seq_len, (i + 1) * seq_len)
        u_i = proj_ref[rows, OFF_U:OFF_U + 1024]
        ext_scr[0:1, :] = jnp.zeros((1, 1024), F32)
        ext_scr[1:HIST, :] = pool_ref[0, i]
        ext_scr[HIST:HIST + seq_len, :] = u_i
        ext = ext_scr[...]
        mixed_rows.append(_pool_mix(ext, u_i, pos_col, wpool_ref))
        pool_out[0, i] = ext[seq_len + 1:seq_len + HIST, :]
    mixed = jnp.concatenate(mixed_rows, axis=0)
    y_a = mixed * pscale_ref[...] * _silu(proj_ref[:, OFF_ZA:OFF_ZA + 1024])
    ycat_scr[:, 0:1024] = y_a.astype(BF16)

    lane4 = lax.broadcasted_iota(jnp.int32, (1, N_HEADS), 1)
    m_rows = [jnp.zeros((1, N_HEADS), F32) for _ in range(G)]
    for h in range(N_HEADS):
        hc = slice(h * 256, (h + 1) * 256)
        q = proj_ref[:, OFF_Q + h * 256:OFF_Q + (h + 1) * 256]
        k = proj_ref[:, OFF_K + h * 256:OFF_K + (h + 1) * 256]
        v = proj_ref[:, OFF_V + h * 256:OFF_V + (h + 1) * 256]
        qb, kb, vb = q.astype(BF16), k.astype(BF16), v.astype(BF16)
        ig_col = g_col[:, h:h + 1]
        b_col = b_col_all[:, 4 + h:5 + h]
        r_row = g_row[h:h + 1, :] - b_row_all[4 + h:5 + h, :]
        m_prev = [m_ref[0, i:i + 1, h:h + 1] for i in range(G)]
        a_col = b_col + _expand(seq_col, m_prev)
        m_t, inter, sv, s_sum = _intra(qb, kb, vb, b_col, a_col, r_row, causal)
        c_old = [c_ref[0, i, h] for i in range(G)]
        n_old = [n_ref[0, i, h:h + 1, :] for i in range(G)]
        q_c = _expand(seq_col, [_dot(qb, c.astype(BF16)) for c in c_old])
        num = inter * q_c + sv
        qn = inter * jnp.sum(q * _expand(seq_col, n_old), axis=-1, keepdims=True) + s_sum
        ht = num * (1.0 / jnp.maximum(jnp.abs(qn), jnp.exp(-m_t)))
        last = [(i + 1) * seq_len - 1 for i in range(G)]
        m_new = [m_t[r:r + 1, :] for r in last]
        b_last = [b_col[r:r + 1, :] for r in last]
        w_end = jnp.exp(_expand(seq_col, b_last) - b_col + ig_col - _expand(seq_col, m_new)) * K_SCALE
        kw = k * w_end
        kwb = kw.astype(BF16)
        for i in range(G):
            decay = jnp.exp(b_last[i] + m_prev[i] - m_new[i])
            kw_i = jnp.where(seq_col == i, kwb, jnp.zeros_like(kwb))
            c_out[0, i, h] = decay * c_old[i] + _dot_tn(kw_i, vb)
            n_out[0, i, h:h + 1, :] = decay * n_old[i] + jnp.sum(
                kw[i * seq_len:(i + 1) * seq_len, :], axis=0, keepdims=True)
            m_rows[i] = jnp.where(lane4 == h, m_new[i], m_rows[i])
        hn = _head_norm(ht, mhln_ref[:, hc])
        o = proj_ref[:, OFF_O + h * 256:OFF_O + (h + 1) * 256]
        zb = proj_ref[:, OFF_ZB + h * 256:OFF_ZB + (h + 1) * 256]
        ycat_scr[:, 1024 + h * 256:1024 + (h + 1) * 256] = (hn * jax.nn.sigmoid(o) * _silu(zb)).astype(BF16)
    for i in range(G):
        m_out[0, i:i + 1, :] = m_rows[i]

    y = _dot(ycat_scr[...], wout_ref[...])
    y_ref[...] = _rms(x + y, normf_ref[...])


def _const_spec(shape):
    nd = len(shape)
    return pl.BlockSpec(shape, lambda *_: (0,) * nd, pipeline_mode=pl.Buffered(1))


def _params(sem):
    return pltpu.CompilerParams(dimension_semantics=sem, vmem_limit_bytes=VMEM_LIMIT)


def kernel(x_prompt, x_sample, state_pool, state_C, state_n, state_m, meta_tokens, norm1_w, w_in,
           b_if, w_pool, pool_scale, mhln_w, w_out, normf_w):
    B, S, _ = x_prompt.shape
    SB, SL, _ = x_sample.shape
    assert norm1_w.shape[0] == 1, "single layer"
    assert S % PROMPT_TILE == 0 and SB % SAMPLE_GROUP == 0
    past_len = 16384

    w_in0 = w_in[0]
    w_main = w_in0[:, :D_MAIN].astype(BF16)
    w_gate_cols = w_in0[:, D_MAIN:]
    w_gate = jnp.pad(w_gate_cols, ((0, 0), (0, GATE_PAD - 8))).astype(BF16)
    w_gate_t = jnp.pad(w_gate_cols.T, ((0, GATE_ROWS - 8), (0, 0))).astype(BF16)
    bias_row = jnp.pad(b_if[0], (0, GATE_PAD - 8)).reshape(1, GATE_PAD)
    bias_col = jnp.broadcast_to(jnp.pad(b_if[0], (0, GATE_ROWS - 8))[:, None], (GATE_ROWS, 128))
    norm1 = norm1_w[0].reshape(1, D_MODEL)
    wpool = w_pool[0].astype(BF16)
    pscale = pool_scale[0].reshape(1, D_POOL)
    mhln = mhln_w[0].reshape(1, D_MLSTM)
    wout = w_out[0].astype(BF16)
    normf = normf_w.reshape(1, D_MODEL)
    layer_w = (w_main, w_gate, w_gate_t, bias_row, bias_col, norm1, wpool, pscale, mhln, wout, normf)
    layer_specs = [_const_spec(a.shape) for a in layer_w]

    meta_c, meta_n, meta_m, meta_hist = pl.pallas_call(
        _meta_kernel,
        out_shape=(jax.ShapeDtypeStruct((N_HEADS, HEAD_DIM, HEAD_DIM), F32),
                   jax.ShapeDtypeStruct((8, HEAD_DIM), F32),
                   jax.ShapeDtypeStruct((8, 128), F32),
                   jax.ShapeDtypeStruct((HIST, D_POOL), F32)),
        scratch_shapes=[pltpu.VMEM((N_META, D_MAIN), F32), pltpu.VMEM((N_META, 2048), BF16)],
        compiler_params=pltpu.CompilerParams(vmem_limit_bytes=VMEM_LIMIT),
        name="meta",
    )(meta_tokens, *layer_w)

    n_tiles = S // PROMPT_TILE
    TT = PROMPT_TILE
    y_prompt, pool_p, c_p, n_p, m_p = pl.pallas_call(
        _prompt_kernel,
        grid=(B, n_tiles),
        in_specs=[pl.BlockSpec((1, TT, D_MODEL), lambda b, t: (b, t, 0)),
                  _const_spec(meta_c.shape), _const_spec(meta_n.shape), _const_spec(meta_m.shape),
                  _const_spec(meta_hist.shape)] + layer_specs,
        out_specs=(pl.BlockSpec((1, TT, D_MODEL), lambda b, t: (b, t, 0)),
                   pl.BlockSpec((1, 1, POOL_BUF, D_POOL), lambda b, t: (0, b, 0, 0)),
                   pl.BlockSpec((1, 1, N_HEADS, HEAD_DIM, HEAD_DIM), lambda b, t: (0, b, 0, 0, 0)),
                   pl.BlockSpec((1, 1, N_HEADS, HEAD_DIM), lambda b, t: (0, b, 0, 0)),
                   pl.BlockSpec((B, 128), lambda b, t: (0, 0))),
        out_shape=(jax.ShapeDtypeStruct((B, S, D_MODEL), F32),
                   jax.ShapeDtypeStruct((1, B, POOL_BUF, D_POOL), F32),
                   jax.ShapeDtypeStruct((1, B, N_HEADS, HEAD_DIM, HEAD_DIM), F32),
                   jax.ShapeDtypeStruct((1, B, N_HEADS, HEAD_DIM), F32),
                   jax.ShapeDtypeStruct((B, 128), F32)),
        scratch_shapes=[pltpu.VMEM((TT, D_MAIN), F32), pltpu.VMEM((TT, 2048), BF16),
                        pltpu.VMEM((HIST, D_POOL), F32),
                        pltpu.VMEM((N_HEADS, HEAD_DIM, HEAD_DIM), F32),
                        pltpu.VMEM((8, HEAD_DIM), F32), pltpu.VMEM((8, 128), F32)],
        compiler_params=_params(("arbitrary", "arbitrary")),
        name="prompt",
    )(x_prompt, meta_c, meta_n, meta_m, meta_hist, *layer_w)
    m_p = m_p[:, :N_HEADS].reshape(1, B, N_HEADS)

    n_tok = SB * SL
    xs = x_sample.reshape(n_tok, D_MODEL)
    PT = 256
    proj_s = pl.pallas_call(
        _sproj_kernel,
        grid=(n_tok // PT,),
        in_specs=[pl.BlockSpec((PT, D_MODEL), lambda i: (i, 0)), _const_spec(norm1.shape),
                  _const_spec(w_main.shape)],
        out_specs=pl.BlockSpec((PT, D_MAIN), lambda i: (i, 0)),
        out_shape=jax.ShapeDtypeStruct((n_tok, D_MAIN), F32),
        compiler_params=_params(("arbitrary",)),
        name="sproj",
    )(xs, norm1, w_main)

    G = SAMPLE_GROUP
    GT = G * SL
    sample_w = (w_gate, w_gate_t, bias_row, bias_col, norm1, wpool, pscale, mhln, wout, normf)
    y_s, pool_s, c_s, n_s, m_s = pl.pallas_call(
        functools.partial(_sample_kernel, seq_len=SL, pos0=past_len),
        grid=(SB // G,),
        in_specs=[pl.BlockSpec((GT, D_MODEL), lambda i: (i, 0)),
                  pl.BlockSpec((GT, D_MAIN), lambda i: (i, 0)),
                  pl.BlockSpec((1, G, POOL_BUF, D_POOL), lambda i: (0, i, 0, 0)),
                  pl.BlockSpec((1, G, N_HEADS, HEAD_DIM, HEAD_DIM), lambda i: (0, i, 0, 0, 0)),
                  pl.BlockSpec((1, G, N_HEADS, HEAD_DIM), lambda i: (0, i, 0, 0)),
                  pl.BlockSpec((1, G, N_HEADS), lambda i: (0, i, 0))]
                 + [_const_spec(a.shape) for a in sample_w],
        out_specs=(pl.BlockSpec((GT, D_MODEL), lambda i: (i, 0)),
                   pl.BlockSpec((1, G, POOL_BUF, D_POOL), lambda i: (0, i, 0, 0)),
                   pl.BlockSpec((1, G, N_HEADS, HEAD_DIM, HEAD_DIM), lambda i: (0, i, 0, 0, 0)),
                   pl.BlockSpec((1, G, N_HEADS, HEAD_DIM), lambda i: (0, i, 0, 0)),
                   pl.BlockSpec((1, G, N_HEADS), lambda i: (0, i, 0))),
        out_shape=(jax.ShapeDtypeStruct((n_tok, D_MODEL), F32),
                   jax.ShapeDtypeStruct(state_pool.shape, F32),
                   jax.ShapeDtypeStruct(state_C.shape, F32),
                   jax.ShapeDtypeStruct(state_n.shape, F32),
                   jax.ShapeDtypeStruct(state_m.shape, F32)),
        scratch_shapes=[pltpu.VMEM((HIST + SL, D_POOL), F32), pltpu.VMEM((GT, 2048), BF16)],
        compiler_params=_params(("arbitrary",)),
        name="sample",
    )(xs, proj_s, state_pool, state_C, state_n, state_m, *sample_w)
    y_sample = y_s.reshape(SB, SL, D_MODEL)

    return (y_prompt, y_sample, pool_p, c_p, n_p, m_p, pool_s, c_s, n_s, m_s)
```

```python
import functools

import jax
import jax.numpy as jnp
from jax import lax
from jax.experimental import pallas as pl
from jax.experimental.pallas import tpu as pltpu

D_MODEL = 1024
D_POOL = 1024
D_MLSTM = 1024
N_HEADS = 4
HEAD_DIM = 256
POOL_WINDOWS = (2, 4, 8, 16)
POOL_BUF = 15
HIST = 16
N_META = 16
PAST_LEN = 16384
EPS = 1e-6
D_MAIN = 2 * D_POOL + 5 * D_MLSTM
GATE_PAD = 128
GATE_ROWS = 16
K_SCALE = HEAD_DIM ** -0.5

OFF_U, OFF_ZA, OFF_Q, OFF_K, OFF_V, OFF_O, OFF_ZB = (i * 1024 for i in range(7))

PROMPT_TILE = 256
SAMPLE_GROUP = 8
VMEM_LIMIT = 60000 * 1024

F32 = jnp.float32
BF16 = jnp.bfloat16


def _dot(a, b):
    return jnp.dot(a, b, preferred_element_type=F32)


def _dot_nt(a, b):
    return lax.dot_general(a, b, (((1,), (1,)), ((), ())), preferred_element_type=F32)


def _dot_tn(a, b):
    return lax.dot_general(a, b, (((0,), (0,)), ((), ())), preferred_element_type=F32)


def _rms(x, w):
    return x * lax.rsqrt(jnp.mean(x * x, axis=-1, keepdims=True) + EPS) * w


def _log_sigmoid(x):
    return jnp.minimum(x, 0.0) - jnp.log1p(jnp.exp(-jnp.abs(x)))


def _silu(x):
    return x * jax.nn.sigmoid(x)


def _split3(x):
    hi = x.astype(BF16)
    r = x - hi.astype(F32)
    mid = r.astype(BF16)
    lo = (r - mid.astype(F32)).astype(BF16)
    return hi, mid, lo


def _seq_masks(T, seq_len):
    row = lax.broadcasted_iota(jnp.int32, (T, T), 0)
    col = lax.broadcasted_iota(jnp.int32, (T, T), 1)
    causal, causal_t = col <= row, row <= col
    if seq_len < T:
        shift = seq_len.bit_length() - 1
        assert 1 << shift == seq_len
        same = (row >> shift) == (col >> shift)
        causal, causal_t = causal & same, causal_t & same
    return causal, causal_t


def _gate_pre(xn, wg_ref, wgt_ref, brow_ref, bcol_ref):
    g_col = _dot(xn, wg_ref[...]) + brow_ref[...]
    g_row = _dot_nt(wgt_ref[...], xn) + bcol_ref[:, 0:1]
    return g_col, g_row


def _gate_cumsum(g_col, g_row, causal, causal_t):
    tri = jnp.where(causal, 1.0, 0.0).astype(BF16)
    tri_t = jnp.where(causal_t, 1.0, 0.0).astype(BF16)
    b_col = sum(_dot(tri, p) for p in _split3(_log_sigmoid(g_col)))
    b_row = sum(_dot(p, tri_t) for p in _split3(_log_sigmoid(g_row)))
    return b_col, b_row


def _intra(qb, kb, vb, b_col, a_col, r_row, causal):
    dm = jnp.where(causal, b_col + r_row, -jnp.inf)
    m_t = jnp.maximum(a_col, jnp.max(dm, axis=-1, keepdims=True))
    w = jnp.exp(dm - m_t)
    inter = jnp.exp(a_col - m_t)
    s = _dot_nt(qb, kb) * (w * K_SCALE)
    return m_t, inter, _dot(s.astype(BF16), vb), jnp.sum(s, axis=-1, keepdims=True)


def _head_norm(ht, w_row):
    mu = jnp.mean(ht, axis=-1, keepdims=True)
    d = ht - mu
    var = jnp.mean(d * d, axis=-1, keepdims=True)
    return d * lax.rsqrt(var + EPS) * w_row


def _window_sums(ext):
    s2 = ext + pltpu.roll(ext, 1, axis=0)
    s4 = s2[:, 256:] + pltpu.roll(s2[:, 256:], 2, axis=0)
    s8 = s4[:, 256:] + pltpu.roll(s4[:, 256:], 4, axis=0)
    s16 = s8[:, 256:] + pltpu.roll(s8[:, 256:], 8, axis=0)
    return [s2[HIST:, 0:256], s4[HIST:, 0:256], s8[HIST:, 0:256], s16[HIST:, 0:256]]


def _pool_mix(ext, u, pos_col, wpool_ref):
    sums = _window_sums(ext)
    outs = []
    for g, w in enumerate(POOL_WINDOWS):
        inv_cnt = 1.0 / jnp.minimum(float(w), pos_col + 1.0)
        pooled = sums[g] * inv_cnt - u[:, g * 256:(g + 1) * 256]
        outs.append(_dot(pooled.astype(BF16), wpool_ref[g]))
    return jnp.concatenate(outs, axis=-1)


def _inproj_steps(x_ref, scr, w):
    wmain_ref, wg_ref, wgt_ref, brow_ref, bcol_ref, norm1_ref = w[:6]
    proj_scr, x_scr, gcol_scr, grow_scr, _, xn_scr = scr
    half = 512

    def norm():
        x = x_ref[...]
        x_scr[...] = x
        xn_scr[...] = _rms(x, norm1_ref[...]).astype(BF16)

    def piece(c0):
        def run():
            proj_scr[:, c0:c0 + half] = _dot(xn_scr[...], wmain_ref[:, c0:c0 + half])
        return run

    def gates():
        g_col, g_row = _gate_pre(xn_scr[...], wg_ref, wgt_ref, brow_ref, bcol_ref)
        gcol_scr[...] = g_col
        grow_scr[...] = g_row

    return [norm, gates] + [piece(c0) for c0 in range(0, D_MAIN, half)]


def _post_steps(scr, pos0, w, state, y_ref):
    wpool_ref, pscale_ref, mhln_ref, wout_ref, normf_ref = w[6:11]
    proj_scr, x_scr, gcol_scr, grow_scr, ycat_scr, _ = scr
    hist_scr, c_scr, n_scr, m_scr = state
    T = x_scr.shape[0]

    def seg(off, h=None):
        if h is None:
            return proj_scr[:, off:off + 1024]
        return proj_scr[:, off + h * 256:off + (h + 1) * 256]

    causal, causal_t = _seq_masks(T, T)
    g_col, g_row = gcol_scr[...], grow_scr[...]
    b_col_all, b_row_all = _gate_cumsum(g_col, g_row, causal, causal_t)
    yield

    u = seg(OFF_U)
    ext = jnp.concatenate([hist_scr[...], u], axis=0)
    pos_col = (lax.broadcasted_iota(jnp.int32, (T, 1), 0) + pos0).astype(F32)
    mixed = _pool_mix(ext, u, pos_col, wpool_ref)
    hist_scr[...] = ext[T:T + HIST, :]
    yield
    ycat_scr[:, 0:1024] = (mixed * pscale_ref[...] * _silu(seg(OFF_ZA))).astype(BF16)
    yield

    for h in range(N_HEADS):
        q, k, v = seg(OFF_Q, h), seg(OFF_K, h), seg(OFF_V, h)
        qb, kb, vb = q.astype(BF16), k.astype(BF16), v.astype(BF16)
        ig_col = g_col[:, h:h + 1]
        b_col = b_col_all[:, 4 + h:5 + h]
        r_row = g_row[h:h + 1, :] - b_row_all[4 + h:5 + h, :]
        m_prev = m_scr[h:h + 1, 0:1]
        a_col = b_col + m_prev
        c_old = c_scr[h]
        n_old = n_scr[h:h + 1, :]
        q_c = _dot(qb, c_old.astype(BF16))
        m_t, inter, sv, s_sum = _intra(qb, kb, vb, b_col, a_col, r_row, causal)
        yield
        num = inter * q_c + sv
        qn = inter * jnp.sum(q * n_old, axis=-1, keepdims=True) + s_sum
        ht = num * (1.0 / jnp.maximum(jnp.abs(qn), jnp.exp(-m_t)))
        m_new = m_t[T - 1:T, :]
        b_last = b_col[T - 1:T, :]
        w_end = jnp.exp(b_last - b_col + ig_col - m_new) * K_SCALE
        decay = jnp.exp(b_last + m_prev - m_new)
        kw = k * w_end
        c_scr[h] = decay * c_old + _dot_tn(kw.astype(BF16), vb)
        n_scr[h:h + 1, :] = decay * n_old + jnp.sum(kw, axis=0, keepdims=True)
        m_scr[h:h + 1, :] = jnp.broadcast_to(m_new, (1, 128))
        yield
        hn = _head_norm(ht, mhln_ref[:, h * 256:(h + 1) * 256])
        ycat_scr[:, 1024 + h * 256:1024 + (h + 1) * 256] = (
            hn * jax.nn.sigmoid(seg(OFF_O, h)) * _silu(seg(OFF_ZB, h))).astype(BF16)
        yield

    y = _dot(ycat_scr[...], wout_ref[...])
    y_ref[...] = _rms(x_scr[...] + y, normf_ref[...])


POST_STAGES = 3 + 3 * N_HEADS


def _run_interleaved(post, pieces):
    done = 0
    for i in range(POST_STAGES):
        next(post)
        upto = (i + 1) * len(pieces) // POST_STAGES
        for p in pieces[done:upto]:
            p()
        done = upto
    assert done == len(pieces)
    for _ in post:
        raise AssertionError("unexpected extra stage")


def _meta_kernel(meta_ref, *refs):
    w = refs[:11]
    c_out, n_out, m_out, hist_out = refs[11:15]
    scr = refs[15:21]
    y_scr = refs[21]
    c_out[...] = jnp.zeros_like(c_out)
    n_out[...] = jnp.zeros_like(n_out)
    m_out[...] = jnp.zeros_like(m_out)
    hist_out[...] = jnp.zeros_like(hist_out)
    for p in _inproj_steps(meta_ref, scr, w):
        p()
    for _ in _post_steps(scr, 0, w, (hist_out, c_out, n_out, m_out), y_scr):
        pass


def _m_row(m_scr):
    lane = lax.broadcasted_iota(jnp.int32, (1, 128), 1)
    row = jnp.zeros((1, 128), F32)
    for h in range(N_HEADS):
        row = jnp.where(lane == h, m_scr[h:h + 1, :], row)
    return row


def _prompt_kernel(x0_ref, xb_ref, xc_ref, c0_ref, n0_ref, m0_ref, hist0_ref, *refs, steps_per_seq):
    w = refs[:11]
    y_ref, pool_out, c_out, n_out, m_out = refs[11:16]
    scr0, scr1 = refs[16:22], refs[22:28]
    state = refs[28:32]
    hist_scr, c_scr, n_scr, m_scr = state
    s = pl.program_id(0)
    TT = PROMPT_TILE

    @pl.when(s == 0)
    def _():
        for p in _inproj_steps(x0_ref, scr0, w):
            p()

    @pl.when(s % steps_per_seq == 0)
    def _():
        c_scr[...] = c0_ref[...]
        n_scr[...] = n0_ref[...]
        m_scr[...] = m0_ref[...]
        hist_scr[...] = hist0_ref[...]

    _run_interleaved(_post_steps(scr0, N_META, w, state, y_ref.at[0:TT, :]),
                     _inproj_steps(xb_ref, scr1, w))
    _run_interleaved(_post_steps(scr1, N_META, w, state, y_ref.at[TT:2 * TT, :]),
                     _inproj_steps(xc_ref, scr0, w))

    @pl.when(s % steps_per_seq == steps_per_seq - 1)
    def _():
        b = s // steps_per_seq
        pool_out[0, 0] = hist_scr[1:HIST, :]
        c_out[0, 0] = c_scr[...]
        n_out[0, 0] = n_scr[0:N_HEADS, :]
        m_out[pl.ds(b, 1), :] = _m_row(m_scr)


def _sproj_kernel(x_ref, norm1_ref, wmain_ref, proj_ref):
    xn = _rms(x_ref[...], norm1_ref[...]).astype(BF16)
    for seg in range(7):
        cols = slice(seg * 1024, (seg + 1) * 1024)
        proj_ref[:, cols] = _dot(xn, wmain_ref[:, cols])


def _expand(seq_col, vals):
    out = None
    for i, val in enumerate(vals):
        pick = jnp.where(seq_col == i, val, 0.0)
        out = pick if out is None else out + pick
    return out


def _sample_kernel(x_ref, proj_ref, pool_ref, c_ref, n_ref, m_ref, *refs, seq_len, pos0):
    (wg_ref, wgt_ref, brow_ref, bcol_ref, norm1_ref, wpool_ref, pscale_ref, mhln_ref, wout_ref,
     normf_ref) = refs[:10]
    y_ref, pool_out, c_out, n_out, m_out = refs[10:15]
    ext_scr, ycat_scr = refs[15:17]
    G = SAMPLE_GROUP
    T = G * seq_len
    x = x_ref[...]
    xn = _rms(x, norm1_ref[...]).astype(BF16)
    causal, causal_t = _seq_masks(T, seq_len)
    g_col, g_row = _gate_pre(xn, wg_ref, wgt_ref, brow_ref, bcol_ref)
    b_col_all, b_row_all = _gate_cumsum(g_col, g_row, causal, causal_t)
    seq_col = lax.broadcasted_iota(jnp.int32, (T, 1), 0) >> (seq_len.bit_length() - 1)
    pos_col = jnp.full((seq_len, 1), float(pos0), F32) + lax.broadcasted_iota(
        jnp.int32, (seq_len, 1), 0).astype(F32)

    mixed_rows = []
    for i in range(G):
        rows = slice(i * seq_len, (i + 1) * seq_len)
        u_i = proj_ref[rows, OFF_U:OFF_U + 1024]
        ext_scr[0:1, :] = jnp.zeros((1, 1024), F32)
        ext_scr[1:HIST, :] = pool_ref[0, i]
        ext_scr[HIST:HIST + seq_len, :] = u_i
        ext = ext_scr[...]
        mixed_rows.append(_pool_mix(ext, u_i, pos_col, wpool_ref))
        pool_out[0, i] = ext[seq_len + 1:seq_len + HIST, :]
    mixed = jnp.concatenate(mixed_rows, axis=0)
    y_a = mixed * pscale_ref[...] * _silu(proj_ref[:, OFF_ZA:OFF_ZA + 1024])
    ycat_scr[:, 0:1024] = y_a.astype(BF16)

    lane4 = lax.broadcasted_iota(jnp.int32, (1, N_HEADS), 1)
    m_rows = [jnp.zeros((1, N_HEADS), F32) for _ in range(G)]
    for h in range(N_HEADS):
        hc = slice(h * 256, (h + 1) * 256)
        q = proj_ref[:, OFF_Q + h * 256:OFF_Q + (h + 1) * 256]
        k = proj_ref[:, OFF_K + h * 256:OFF_K + (h + 1) * 256]
        v = proj_ref[:, OFF_V + h * 256:OFF_V + (h + 1) * 256]
        qb, kb, vb = q.astype(BF16), k.astype(BF16), v.astype(BF16)
        ig_col = g_col[:, h:h + 1]
        b_col = b_col_all[:, 4 + h:5 + h]
        r_row = g_row[h:h + 1, :] - b_row_all[4 + h:5 + h, :]
        m_prev = [m_ref[0, i:i + 1, h:h + 1] for i in range(G)]
        a_col = b_col + _expand(seq_col, m_prev)
        m_t, inter, sv, s_sum = _intra(qb, kb, vb, b_col, a_col, r_row, causal)
        c_old = [c_ref[0, i, h] for i in range(G)]
        n_old = [n_ref[0, i, h:h + 1, :] for i in range(G)]
        q_c = _expand(seq_col, [_dot(qb, c.astype(BF16)) for c in c_old])
        num = inter * q_c + sv
        qn = inter * jnp.sum(q * _expand(seq_col, n_old), axis=-1, keepdims=True) + s_sum
        ht = num * (1.0 / jnp.maximum(jnp.abs(qn), jnp.exp(-m_t)))
        last = [(i + 1) * seq_len - 1 for i in range(G)]
        m_new = [m_t[r:r + 1, :] for r in last]
        b_last = [b_col[r:r + 1, :] for r in last]
        w_end = jnp.exp(_expand(seq_col, b_last) - b_col + ig_col - _expand(seq_col, m_new)) * K_SCALE
        kw = k * w_end
        kwb = kw.astype(BF16)
        for i in range(G):
            decay = jnp.exp(b_last[i] + m_prev[i] - m_new[i])
            kw_i = jnp.where(seq_col == i, kwb, jnp.zeros_like(kwb))
            c_out[0, i, h] = decay * c_old[i] + _dot_tn(kw_i, vb)
            n_out[0, i, h:h + 1, :] = decay * n_old[i] + jnp.sum(
                kw[i * seq_len:(i + 1) * seq_len, :], axis=0, keepdims=True)
            m_rows[i] = jnp.where(lane4 == h, m_new[i], m_rows[i])
        hn = _head_norm(ht, mhln_ref[:, hc])
        o = proj_ref[:, OFF_O + h * 256:OFF_O + (h + 1) * 256]
        zb = proj_ref[:, OFF_ZB + h * 256:OFF_ZB + (h + 1) * 256]
        ycat_scr[:, 1024 + h * 256:1024 + (h + 1) * 256] = (hn * jax.nn.sigmoid(o) * _silu(zb)).astype(BF16)
    for i in range(G):
        m_out[0, i:i + 1, :] = m_rows[i]

    y = _dot(ycat_scr[...], wout_ref[...])
    y_ref[...] = _rms(x + y, normf_ref[...])


def _const_spec(shape):
    nd = len(shape)
    return pl.BlockSpec(shape, lambda *_: (0,) * nd, pipeline_mode=pl.Buffered(1))


def _params(sem):
    return pltpu.CompilerParams(dimension_semantics=sem, vmem_limit_bytes=VMEM_LIMIT)


def _tile_scratch(T):
    return [pltpu.VMEM((T, D_MAIN), F32), pltpu.VMEM((T, D_MODEL), F32),
            pltpu.VMEM((T, GATE_PAD), F32), pltpu.VMEM((GATE_ROWS, T), F32),
            pltpu.VMEM((T, 2048), BF16), pltpu.VMEM((T, D_MODEL), BF16)]


def kernel(x_prompt, x_sample, state_pool, state_C, state_n, state_m, meta_tokens, norm1_w, w_in,
           b_if, w_pool, pool_scale, mhln_w, w_out, normf_w):
    B, S, _ = x_prompt.shape
    SB, SL, _ = x_sample.shape
    TT = PROMPT_TILE
    assert norm1_w.shape[0] == 1, "single layer"
    assert S % (2 * TT) == 0 and SB % SAMPLE_GROUP == 0

    w_in0 = w_in[0]
    w_main = w_in0.astype(BF16)
    w_gate_cols = w_in0[:, D_MAIN:]
    w_gate = jnp.pad(w_gate_cols, ((0, 0), (0, GATE_PAD - 8))).astype(BF16)
    w_gate_t = jnp.pad(w_gate_cols.T, ((0, GATE_ROWS - 8), (0, 0))).astype(BF16)
    bias_row = jnp.pad(b_if[0], (0, GATE_PAD - 8)).reshape(1, GATE_PAD)
    bias_col = jnp.broadcast_to(jnp.pad(b_if[0], (0, GATE_ROWS - 8))[:, None], (GATE_ROWS, 128))
    norm1 = norm1_w[0].reshape(1, D_MODEL)
    wpool = w_pool[0].astype(BF16)
    pscale = pool_scale[0].reshape(1, D_POOL)
    mhln = mhln_w[0].reshape(1, D_MLSTM)
    wout = w_out[0].astype(BF16)
    normf = normf_w.reshape(1, D_MODEL)
    layer_w = (w_main, w_gate, w_gate_t, bias_row, bias_col, norm1, wpool, pscale, mhln, wout, normf)
    layer_specs = [_const_spec(a.shape) for a in layer_w]

    meta_c, meta_n, meta_m, meta_hist = pl.pallas_call(
        _meta_kernel,
        out_shape=(jax.ShapeDtypeStruct((N_HEADS, HEAD_DIM, HEAD_DIM), F32),
                   jax.ShapeDtypeStruct((8, HEAD_DIM), F32),
                   jax.ShapeDtypeStruct((8, 128), F32),
                   jax.ShapeDtypeStruct((HIST, D_POOL), F32)),
        scratch_shapes=_tile_scratch(N_META) + [pltpu.VMEM((N_META, D_MODEL), F32)],
        compiler_params=pltpu.CompilerParams(vmem_limit_bytes=VMEM_LIMIT),
        name="meta",
    )(meta_tokens, *layer_w)

    n_tiles = B * S // TT
    steps = n_tiles // 2
    steps_per_seq = S // (2 * TT)
    xp = x_prompt.reshape(B * S, D_MODEL)
    y_p, pool_p, c_p, n_p, m_p = pl.pallas_call(
        functools.partial(_prompt_kernel, steps_per_seq=steps_per_seq),
        grid=(steps,),
        in_specs=[pl.BlockSpec((TT, D_MODEL), lambda s: (0, 0)),
                  pl.BlockSpec((TT, D_MODEL), lambda s: (2 * s + 1, 0)),
                  pl.BlockSpec((TT, D_MODEL), lambda s: (jnp.minimum(2 * s + 2, n_tiles - 1), 0)),
                  _const_spec(meta_c.shape), _const_spec(meta_n.shape), _const_spec(meta_m.shape),
                  _const_spec(meta_hist.shape)] + layer_specs,
        out_specs=(pl.BlockSpec((2 * TT, D_MODEL), lambda s: (s, 0)),
                   pl.BlockSpec((1, 1, POOL_BUF, D_POOL), lambda s: (0, s // steps_per_seq, 0, 0)),
                   pl.BlockSpec((1, 1, N_HEADS, HEAD_DIM, HEAD_DIM),
                                lambda s: (0, s // steps_per_seq, 0, 0, 0)),
                   pl.BlockSpec((1, 1, N_HEADS, HEAD_DIM), lambda s: (0, s // steps_per_seq, 0, 0)),
                   pl.BlockSpec((B, 128), lambda s: (0, 0))),
        out_shape=(jax.ShapeDtypeStruct((B * S, D_MODEL), F32),
                   jax.ShapeDtypeStruct((1, B, POOL_BUF, D_POOL), F32),
                   jax.ShapeDtypeStruct((1, B, N_HEADS, HEAD_DIM, HEAD_DIM), F32),
                   jax.ShapeDtypeStruct((1, B, N_HEADS, HEAD_DIM), F32),
                   jax.ShapeDtypeStruct((B, 128), F32)),
        scratch_shapes=_tile_scratch(TT) + _tile_scratch(TT) + [
            pltpu.VMEM((HIST, D_POOL), F32), pltpu.VMEM((N_HEADS, HEAD_DIM, HEAD_DIM), F32),
            pltpu.VMEM((8, HEAD_DIM), F32), pltpu.VMEM((8, 128), F32)],
        compiler_params=_params(("arbitrary",)),
        name="prompt",
    )(xp, xp, xp, meta_c, meta_n, meta_m, meta_hist, *layer_w)
    y_prompt = y_p.reshape(B, S, D_MODEL)
    m_p = m_p[:, :N_HEADS].reshape(1, B, N_HEADS)

    n_tok = SB * SL
    xs = x_sample.reshape(n_tok, D_MODEL)
    PT = 256
    proj_s = pl.pallas_call(
        _sproj_kernel,
        grid=(n_tok // PT,),
        in_specs=[pl.BlockSpec((PT, D_MODEL), lambda i: (i, 0)), _const_spec(norm1.shape),
                  _const_spec(w_main.shape)],
        out_specs=pl.BlockSpec((PT, D_MAIN), lambda i: (i, 0)),
        out_shape=jax.ShapeDtypeStruct((n_tok, D_MAIN), F32),
        compiler_params=_params(("arbitrary",)),
        name="sproj",
    )(xs, norm1, w_main)

    G = SAMPLE_GROUP
    GT = G * SL
    sample_w = (w_gate, w_gate_t, bias_row, bias_col, norm1, wpool, pscale, mhln, wout, normf)
    y_s, pool_s, c_s, n_s, m_s = pl.pallas_call(
        functools.partial(_sample_kernel, seq_len=SL, pos0=PAST_LEN),
        grid=(SB // G,),
        in_specs=[pl.BlockSpec((GT, D_MODEL), lambda i: (i, 0)),
                  pl.BlockSpec((GT, D_MAIN), lambda i: (i, 0)),
                  pl.BlockSpec((1, G, POOL_BUF, D_POOL), lambda i: (0, i, 0, 0)),
                  pl.BlockSpec((1, G, N_HEADS, HEAD_DIM, HEAD_DIM), lambda i: (0, i, 0, 0, 0)),
                  pl.BlockSpec((1, G, N_HEADS, HEAD_DIM), lambda i: (0, i, 0, 0)),
                  pl.BlockSpec((1, G, N_HEADS), lambda i: (0, i, 0))]
                 + [_const_spec(a.shape) for a in sample_w],
        out_specs=(pl.BlockSpec((GT, D_MODEL), lambda i: (i, 0)),
                   pl.BlockSpec((1, G, POOL_BUF, D_POOL), lambda i: (0, i, 0, 0)),
                   pl.BlockSpec((1, G, N_HEADS, HEAD_DIM, HEAD_DIM), lambda i: (0, i, 0, 0, 0)),
                   pl.BlockSpec((1, G, N_HEADS, HEAD_DIM), lambda i: (0, i, 0, 0)),
                   pl.BlockSpec((1, G, N_HEADS), lambda i: (0, i, 0))),
        out_shape=(jax.ShapeDtypeStruct((n_tok, D_MODEL), F32),
                   jax.ShapeDtypeStruct(state_pool.shape, F32),
                   jax.ShapeDtypeStruct(state_C.shape, F32),
                   jax.ShapeDtypeStruct(state_n.shape, F32),
                   jax.ShapeDtypeStruct(state_m.shape, F32)),
        scratch_shapes=[pltpu.VMEM((HIST + SL, D_POOL), F32), pltpu.VMEM((GT, 2048), BF16)],
        compiler_params=_params(("arbitrary",)),
        name="sample",
    )(xs, proj_s, state_pool, state_C, state_n, state_m, *sample_w)
    y_sample = y_s.reshape(SB, SL, D_MODEL)

    return (y_prompt, y_sample, pool_p, c_p, n_p, m_p, pool_s, c_s, n_s, m_s)
```

```python
import collections
import functools

import jax
import jax.numpy as jnp
from jax import lax
from jax.experimental import pallas as pl
from jax.experimental.pallas import tpu as pltpu

D_MODEL = 1024
D_POOL = 1024
D_MLSTM = 1024
N_HEADS = 4
HEAD_DIM = 256
POOL_WINDOWS = (2, 4, 8, 16)
POOL_BUF = 15
HIST = 16
N_META = 16
PAST_LEN = 16384
EPS = 1e-6
D_MAIN = 2 * D_POOL + 5 * D_MLSTM
GATE_PAD = 128
K_SCALE = HEAD_DIM ** -0.5

OFF_U, OFF_ZA, OFF_Q, OFF_K, OFF_V, OFF_O, OFF_ZB = (i * 1024 for i in range(7))

PROMPT_TILE = 256
SAMPLE_GROUP = 8
VMEM_LIMIT = 60000 * 1024

F32 = jnp.float32
BF16 = jnp.bfloat16

LayerW = collections.namedtuple("LayerW", "main gate bias norm1 pool pscale mhln out normf")


def _dot(a, b):
    return jnp.dot(a, b, preferred_element_type=F32)


def _dot_nt(a, b):
    return lax.dot_general(a, b, (((1,), (1,)), ((), ())), preferred_element_type=F32)


def _dot_tn(a, b):
    return lax.dot_general(a, b, (((0,), (0,)), ((), ())), preferred_element_type=F32)


def _rms(x, w):
    return x * lax.rsqrt(jnp.mean(x * x, axis=-1, keepdims=True) + EPS) * w


def _log_sigmoid(x):
    return jnp.minimum(x, 0.0) - jnp.log1p(jnp.exp(-jnp.abs(x)))


def _silu(x):
    return x * jax.nn.sigmoid(x)


def _split3(x):
    hi = x.astype(BF16)
    r = x - hi.astype(F32)
    mid = r.astype(BF16)
    lo = (r - mid.astype(F32)).astype(BF16)
    return hi, mid, lo


def _seq_mask(T, seq_len):
    row = lax.broadcasted_iota(jnp.int32, (T, T), 0)
    col = lax.broadcasted_iota(jnp.int32, (T, T), 1)
    causal = col <= row
    if seq_len < T:
        shift = seq_len.bit_length() - 1
        assert 1 << shift == seq_len
        causal = causal & ((row >> shift) == (col >> shift))
    return causal


def _to_rows(cols):
    T = cols.shape[0]
    pad = -T % 128
    if pad:
        cols = jnp.concatenate([cols, jnp.zeros((pad, cols.shape[1]), cols.dtype)], axis=0)
    return cols.T[:, 0:T]


def _gate_pre(xn, w):
    return _dot(xn, w.gate[...]) + w.bias[...]


def _gate_cumsum(g_col, causal):
    tri = jnp.where(causal, 1.0, 0.0).astype(BF16)
    return sum(_dot(tri, p) for p in _split3(_log_sigmoid(g_col)))


def _intra(qb, kb, vb, b_col, a_col, r_row, causal):
    dm = jnp.where(causal, b_col + r_row, -jnp.inf)
    m_t = jnp.maximum(a_col, jnp.max(dm, axis=-1, keepdims=True))
    w = jnp.exp(dm - m_t)
    inter = jnp.exp(a_col - m_t)
    s = _dot_nt(qb, kb) * (w * K_SCALE)
    return m_t, inter, _dot(s.astype(BF16), vb), jnp.sum(s, axis=-1, keepdims=True)


def _head_norm(ht, w_row):
    mu = jnp.mean(ht, axis=-1, keepdims=True)
    d = ht - mu
    var = jnp.mean(d * d, axis=-1, keepdims=True)
    return d * lax.rsqrt(var + EPS) * w_row


def _window_sums(ext):
    s2 = ext + pltpu.roll(ext, 1, axis=0)
    s4 = s2[:, 256:] + pltpu.roll(s2[:, 256:], 2, axis=0)
    s8 = s4[:, 256:] + pltpu.roll(s4[:, 256:], 4, axis=0)
    s16 = s8[:, 256:] + pltpu.roll(s8[:, 256:], 8, axis=0)
    return [s2[HIST:, 0:256], s4[HIST:, 0:256], s8[HIST:, 0:256], s16[HIST:, 0:256]]


def _pooled(ext, u, pos_col):
    sums = _window_sums(ext)
    return [sums[g] * (1.0 / jnp.minimum(float(w), pos_col + 1.0)) - u[:, g * 256:(g + 1) * 256]
            for g, w in enumerate(POOL_WINDOWS)]


def _pool_mix(pooled, wpool_ref):
    return jnp.concatenate([_dot(p.astype(BF16), wpool_ref[g]) for g, p in enumerate(pooled)], axis=-1)


def _inproj_steps(x_ref, scr, w):
    proj_scr, x_scr, gcol_scr, _, xn_scr = scr
    half = 512

    def norm():
        x = x_ref[...]
        x_scr[...] = x
        xn_scr[...] = _rms(x, w.norm1[...]).astype(BF16)

    def piece(c0):
        def run():
            proj_scr[:, c0:c0 + half] = _dot(xn_scr[...], w.main[:, c0:c0 + half])
        return run

    def gates():
        gcol_scr[...] = _gate_pre(xn_scr[...], w)

    return [norm, gates] + [piece(c0) for c0 in range(0, D_MAIN, half)]


def _post_steps(scr, pos0, w, state, y_ref):
    proj_scr, x_scr, gcol_scr, ycat_scr, _ = scr
    hist_scr, c_scr, n_scr, m_scr = state
    T = x_scr.shape[0]

    def seg(off, h=None):
        if h is None:
            return proj_scr[:, off:off + 1024]
        return proj_scr[:, off + h * 256:off + (h + 1) * 256]

    causal = _seq_mask(T, T)
    g_col = gcol_scr[...]
    b_col_all = _gate_cumsum(g_col, causal)
    r_row_all = _to_rows(g_col - pltpu.roll(b_col_all, GATE_PAD - N_HEADS, axis=1))
    yield

    u = seg(OFF_U)
    ext = jnp.concatenate([hist_scr[...], u], axis=0)
    pos_col = (lax.broadcasted_iota(jnp.int32, (T, 1), 0) + pos0).astype(F32)
    mixed = _pool_mix(_pooled(ext, u, pos_col), w.pool)
    hist_scr[...] = ext[T:T + HIST, :]
    yield
    ycat_scr[:, 0:1024] = (mixed * w.pscale[...] * _silu(seg(OFF_ZA))).astype(BF16)
    yield

    for h in range(N_HEADS):
        q, k, v = seg(OFF_Q, h), seg(OFF_K, h), seg(OFF_V, h)
        qb, kb, vb = q.astype(BF16), k.astype(BF16), v.astype(BF16)
        ig_col = g_col[:, h:h + 1]
        b_col = b_col_all[:, 4 + h:5 + h]
        r_row = r_row_all[h:h + 1, :]
        m_prev = m_scr[h:h + 1, 0:1]
        a_col = b_col + m_prev
        c_old = c_scr[h]
        n_old = n_scr[h:h + 1, :]
        q_c = _dot(qb, c_old.astype(BF16))
        m_t, inter, sv, s_sum = _intra(qb, kb, vb, b_col, a_col, r_row, causal)
        yield
        num = inter * q_c + sv
        qn = inter * jnp.sum(q * n_old, axis=-1, keepdims=True) + s_sum
        ht = num * (1.0 / jnp.maximum(jnp.abs(qn), jnp.exp(-m_t)))
        m_new = m_t[T - 1:T, :]
        b_last = b_col[T - 1:T, :]
        w_end = jnp.exp(b_last - b_col + ig_col - m_new) * K_SCALE
        decay = jnp.exp(b_last + m_prev - m_new)
        kw = k * w_end
        c_scr[h] = decay * c_old + _dot_tn(kw.astype(BF16), vb)
        n_scr[h:h + 1, :] = decay * n_old + jnp.sum(kw, axis=0, keepdims=True)
        m_scr[h:h + 1, :] = jnp.broadcast_to(m_new, (1, 128))
        yield
        hn = _head_norm(ht, w.mhln[:, h * 256:(h + 1) * 256])
        ycat_scr[:, 1024 + h * 256:1024 + (h + 1) * 256] = (
            hn * jax.nn.sigmoid(seg(OFF_O, h)) * _silu(seg(OFF_ZB, h))).astype(BF16)
        yield

    y = _dot(ycat_scr[...], w.out[...])
    y_ref[...] = _rms(x_scr[...] + y, w.normf[...])


POST_STAGES = 3 + 3 * N_HEADS


def _run_interleaved(post, pieces):
    done = 0
    for i in range(POST_STAGES):
        next(post)
        upto = (i + 1) * len(pieces) // POST_STAGES
        for p in pieces[done:upto]:
            p()
        done = upto
    assert done == len(pieces)
    for _ in post:
        raise AssertionError("unexpected extra stage")


def _meta_kernel(meta_ref, *refs):
    w = LayerW(*refs[:9])
    c_out, n_out, m_out, hist_out = refs[9:13]
    scr = refs[13:18]
    y_scr = refs[18]
    c_out[...] = jnp.zeros_like(c_out)
    n_out[...] = jnp.zeros_like(n_out)
    m_out[...] = jnp.zeros_like(m_out)
    hist_out[...] = jnp.zeros_like(hist_out)
    for p in _inproj_steps(meta_ref, scr, w):
        p()
    for _ in _post_steps(scr, 0, w, (hist_out, c_out, n_out, m_out), y_scr):
        pass


def _m_row(m_scr):
    lane = lax.broadcasted_iota(jnp.int32, (1, 128), 1)
    row = jnp.zeros((1, 128), F32)
    for h in range(N_HEADS):
        row = jnp.where(lane == h, m_scr[h:h + 1, :], row)
    return row


def _prompt_kernel(x0_ref, xb_ref, xc_ref, c0_ref, n0_ref, m0_ref, hist0_ref, *refs, steps_per_seq):
    w = LayerW(*refs[:9])
    y_ref, pool_out, c_out, n_out, m_out = refs[9:14]
    scr0, scr1 = refs[14:19], refs[19:24]
    state = refs[24:28]
    hist_scr, c_scr, n_scr, m_scr = state
    s = pl.program_id(0)
    TT = PROMPT_TILE

    @pl.when(s == 0)
    def _():
        for p in _inproj_steps(x0_ref, scr0, w):
            p()

    @pl.when(s % steps_per_seq == 0)
    def _():
        c_scr[...] = c0_ref[...]
        n_scr[...] = n0_ref[...]
        m_scr[...] = m0_ref[...]
        hist_scr[...] = hist0_ref[...]

    _run_interleaved(_post_steps(scr0, N_META, w, state, y_ref.at[0:TT, :]),
                     _inproj_steps(xb_ref, scr1, w))
    _run_interleaved(_post_steps(scr1, N_META, w, state, y_ref.at[TT:2 * TT, :]),
                     _inproj_steps(xc_ref, scr0, w))

    @pl.when(s % steps_per_seq == steps_per_seq - 1)
    def _():
        b = s // steps_per_seq
        pool_out[0, 0] = hist_scr[1:HIST, :]
        c_out[0, 0] = c_scr[...]
        n_out[0, 0] = n_scr[0:N_HEADS, :]
        m_out[pl.ds(b, 1), :] = _m_row(m_scr)


def _wprep_kernel(wt_ref, o_ref):
    o_ref[...] = wt_ref[...].T.astype(BF16)


def _sproj_kernel(x_ref, norm1_ref, wmain_ref, proj_ref):
    xn = _rms(x_ref[...], norm1_ref[...]).astype(BF16)
    for seg in range(7):
        cols = slice(seg * 1024, (seg + 1) * 1024)
        proj_ref[:, cols] = _dot(xn, wmain_ref[:, cols])


def _expand(seq_col, vals):
    out = None
    for i, val in enumerate(vals):
        pick = jnp.where(seq_col == i, val, 0.0)
        out = pick if out is None else out + pick
    return out


def _sample_kernel(x_ref, proj_ref, pool_ref, c_ref, n_ref, m_ref, *refs, seq_len, pos0):
    w = LayerW(None, *refs[:8])
    y_ref, pool_out, c_out, n_out, m_out = refs[8:13]
    ext_scr, ycat_scr = refs[13:15]
    G = SAMPLE_GROUP
    T = G * seq_len
    x = x_ref[...]
    xn = _rms(x, w.norm1[...]).astype(BF16)
    causal = _seq_mask(T, seq_len)
    g_col = _gate_pre(xn, w)
    b_col_all = _gate_cumsum(g_col, causal)
    r_row_all = _to_rows(g_col - pltpu.roll(b_col_all, GATE_PAD - N_HEADS, axis=1))
    seq_col = lax.broadcasted_iota(jnp.int32, (T, 1), 0) >> (seq_len.bit_length() - 1)
    pos_col = jnp.full((seq_len, 1), float(pos0), F32) + lax.broadcasted_iota(
        jnp.int32, (seq_len, 1), 0).astype(F32)

    pooled_rows = []
    for i in range(G):
        rows = slice(i * seq_len, (i + 1) * seq_len)
        u_i = proj_ref[rows, OFF_U:OFF_U + 1024]
        ext_scr[0:1, :] = jnp.zeros((1, 1024), F32)
        ext_scr[1:HIST, :] = pool_ref[0, i]
        ext_scr[HIST:HIST + seq_len, :] = u_i
        ext = ext_scr[...]
        pooled_rows.append(_pooled(ext, u_i, pos_col))
        pool_out[0, i] = ext[seq_len + 1:seq_len + HIST, :]
    mixed = _pool_mix([jnp.concatenate(p, axis=0) for p in zip(*pooled_rows)], w.pool)
    y_a = mixed * w.pscale[...] * _silu(proj_ref[:, OFF_ZA:OFF_ZA + 1024])
    ycat_scr[:, 0:1024] = y_a.astype(BF16)

    lane4 = lax.broadcasted_iota(jnp.int32, (1, N_HEADS), 1)
    m_rows = [jnp.zeros((1, N_HEADS), F32) for _ in range(G)]
    for h in range(N_HEADS):
        hc = slice(h * 256, (h + 1) * 256)
        q = proj_ref[:, OFF_Q + h * 256:OFF_Q + (h + 1) * 256]
        k = proj_ref[:, OFF_K + h * 256:OFF_K + (h + 1) * 256]
        v = proj_ref[:, OFF_V + h * 256:OFF_V + (h + 1) * 256]
        qb, kb, vb = q.astype(BF16), k.astype(BF16), v.astype(BF16)
        ig_col = g_col[:, h:h + 1]
        b_col = b_col_all[:, 4 + h:5 + h]
        r_row = r_row_all[h:h + 1, :]
        m_prev = [m_ref[0, i:i + 1, h:h + 1] for i in range(G)]
        a_col = b_col + _expand(seq_col, m_prev)
        m_t, inter, sv, s_sum = _intra(qb, kb, vb, b_col, a_col, r_row, causal)
        c_old = [c_ref[0, i, h] for i in range(G)]
        n_old = [n_ref[0, i, h:h + 1, :] for i in range(G)]
        q_c = _expand(seq_col, [_dot(qb, c.astype(BF16)) for c in c_old])
        num = inter * q_c + sv
        qn = inter * jnp.sum(q * _expand(seq_col, n_old), axis=-1, keepdims=True) + s_sum
        ht = num * (1.0 / jnp.maximum(jnp.abs(qn), jnp.exp(-m_t)))
        last = [(i + 1) * seq_len - 1 for i in range(G)]
        m_new = [m_t[r:r + 1, :] for r in last]
        b_last = [b_col[r:r + 1, :] for r in last]
        w_end = jnp.exp(_expand(seq_col, b_last) - b_col + ig_col - _expand(seq_col, m_new)) * K_SCALE
        kw = k * w_end
        kwb = kw.astype(BF16)
        for i in range(G):
            decay = jnp.exp(b_last[i] + m_prev[i] - m_new[i])
            v_i = jnp.where(seq_col == i, vb, jnp.zeros_like(vb))
            c_out[0, i, h] = decay * c_old[i] + _dot_tn(kwb, v_i)
            n_out[0, i, h:h + 1, :] = decay * n_old[i] + jnp.sum(
                kw[i * seq_len:(i + 1) * seq_len, :], axis=0, keepdims=True)
            m_rows[i] = jnp.where(lane4 == h, m_new[i], m_rows[i])
        hn = _head_norm(ht, w.mhln[:, hc])
        o = proj_ref[:, OFF_O + h * 256:OFF_O + (h + 1) * 256]
        zb = proj_ref[:, OFF_ZB + h * 256:OFF_ZB + (h + 1) * 256]
        ycat_scr[:, 1024 + h * 256:1024 + (h + 1) * 256] = (hn * jax.nn.sigmoid(o) * _silu(zb)).astype(BF16)
    for i in range(G):
        m_out[0, i:i + 1, :] = m_rows[i]

    y = _dot(ycat_scr[...], w.out[...])
    y_ref[...] = _rms(x + y, w.normf[...])


def _const_spec(shape):
    nd = len(shape)
    return pl.BlockSpec(shape, lambda *_: (0,) * nd, pipeline_mode=pl.Buffered(1))


def _params(sem):
    return pltpu.CompilerParams(dimension_semantics=sem, vmem_limit_bytes=VMEM_LIMIT)


def _tile_scratch(T):
    return [pltpu.VMEM((T, D_MAIN), F32), pltpu.VMEM((T, D_MODEL), F32), pltpu.VMEM((T, GATE_PAD), F32),
            pltpu.VMEM((T, 2048), BF16), pltpu.VMEM((T, D_MODEL), BF16)]


def kernel(x_prompt, x_sample, state_pool, state_C, state_n, state_m, meta_tokens, norm1_w, w_in,
           b_if, w_pool, pool_scale, mhln_w, w_out, normf_w):
    B, S, _ = x_prompt.shape
    SB, SL, _ = x_sample.shape
    TT = PROMPT_TILE
    assert norm1_w.shape[0] == 1, "single layer"
    assert S % (2 * TT) == 0 and SB % SAMPLE_GROUP == 0

    w_in_t = jnp.swapaxes(w_in[0], 0, 1)
    w_main = pl.pallas_call(
        _wprep_kernel,
        grid=(D_MAIN // 1024,),
        in_specs=[pl.BlockSpec((1024, D_MODEL), lambda i: (i, 0))],
        out_specs=pl.BlockSpec((D_MODEL, 1024), lambda i: (0, i)),
        out_shape=jax.ShapeDtypeStruct((D_MODEL, D_MAIN), BF16),
        compiler_params=_params(("arbitrary",)),
        name="wprep",
    )(w_in_t)
    w_gate = jnp.pad(w_in_t[D_MAIN:, :].T, ((0, 0), (0, GATE_PAD - 8))).astype(BF16)
    bias_row = jnp.pad(b_if[0], (0, GATE_PAD - 8)).reshape(1, GATE_PAD)
    norm1 = norm1_w[0].reshape(1, D_MODEL)
    wpool = w_pool[0].astype(BF16)
    pscale = pool_scale[0].reshape(1, D_POOL)
    mhln = mhln_w[0].reshape(1, D_MLSTM)
    wout = w_out[0].astype(BF16)
    normf = normf_w.reshape(1, D_MODEL)
    layer_w = LayerW(w_main, w_gate, bias_row, norm1, wpool, pscale, mhln, wout, normf)
    layer_specs = [_const_spec(a.shape) for a in layer_w]

    meta_c, meta_n, meta_m, meta_hist = pl.pallas_call(
        _meta_kernel,
        out_shape=(jax.ShapeDtypeStruct((N_HEADS, HEAD_DIM, HEAD_DIM), F32),
                   jax.ShapeDtypeStruct((8, HEAD_DIM), F32),
                   jax.ShapeDtypeStruct((8, 128), F32),
                   jax.ShapeDtypeStruct((HIST, D_POOL), F32)),
        scratch_shapes=_tile_scratch(N_META) + [pltpu.VMEM((N_META, D_MODEL), F32)],
        compiler_params=pltpu.CompilerParams(vmem_limit_bytes=VMEM_LIMIT),
        name="meta",
    )(meta_tokens, *layer_w)

    n_tiles = B * S // TT
    steps = n_tiles // 2
    steps_per_seq = S // (2 * TT)
    xp = x_prompt.reshape(B * S, D_MODEL)
    y_p, pool_p, c_p, n_p, m_p = pl.pallas_call(
        functools.partial(_prompt_kernel, steps_per_seq=steps_per_seq),
        grid=(steps,),
        in_specs=[pl.BlockSpec((TT, D_MODEL), lambda s: (0, 0)),
                  pl.BlockSpec((TT, D_MODEL), lambda s: (2 * s + 1, 0)),
                  pl.BlockSpec((TT, D_MODEL), lambda s: (jnp.minimum(2 * s + 2, n_tiles - 1), 0)),
                  _const_spec(meta_c.shape), _const_spec(meta_n.shape), _const_spec(meta_m.shape),
                  _const_spec(meta_hist.shape)] + layer_specs,
        out_specs=(pl.BlockSpec((2 * TT, D_MODEL), lambda s: (s, 0)),
                   pl.BlockSpec((1, 1, POOL_BUF, D_POOL), lambda s: (0, s // steps_per_seq, 0, 0)),
                   pl.BlockSpec((1, 1, N_HEADS, HEAD_DIM, HEAD_DIM),
                                lambda s: (0, s // steps_per_seq, 0, 0, 0)),
                   pl.BlockSpec((1, 1, N_HEADS, HEAD_DIM), lambda s: (0, s // steps_per_seq, 0, 0)),
                   pl.BlockSpec((B, 128), lambda s: (0, 0))),
        out_shape=(jax.ShapeDtypeStruct((B * S, D_MODEL), F32),
                   jax.ShapeDtypeStruct((1, B, POOL_BUF, D_POOL), F32),
                   jax.ShapeDtypeStruct((1, B, N_HEADS, HEAD_DIM, HEAD_DIM), F32),
                   jax.ShapeDtypeStruct((1, B, N_HEADS, HEAD_DIM), F32),
                   jax.ShapeDtypeStruct((B, 128), F32)),
        scratch_shapes=_tile_scratch(TT) + _tile_scratch(TT) + [
            pltpu.VMEM((HIST, D_POOL), F32), pltpu.VMEM((N_HEADS, HEAD_DIM, HEAD_DIM), F32),
            pltpu.VMEM((8, HEAD_DIM), F32), pltpu.VMEM((8, 128), F32)],
        compiler_params=_params(("arbitrary",)),
        name="prompt",
    )(xp, xp, xp, meta_c, meta_n, meta_m, meta_hist, *layer_w)
    y_prompt = y_p.reshape(B, S, D_MODEL)
    m_p = m_p[:, :N_HEADS].reshape(1, B, N_HEADS)

    n_tok = SB * SL
    xs = x_sample.reshape(n_tok, D_MODEL)
    PT = 256
    proj_s = pl.pallas_call(
        _sproj_kernel,
        grid=(n_tok // PT,),
        in_specs=[pl.BlockSpec((PT, D_MODEL), lambda i: (i, 0)), _const_spec(norm1.shape),
                  _const_spec(w_main.shape)],
        out_specs=pl.BlockSpec((PT, D_MAIN), lambda i: (i, 0)),
        out_shape=jax.ShapeDtypeStruct((n_tok, D_MAIN), F32),
        compiler_params=_params(("arbitrary",)),
        name="sproj",
    )(xs, norm1, w_main)

    G = SAMPLE_GROUP
    GT = G * SL
    sample_w = layer_w[1:]
    y_s, pool_s, c_s, n_s, m_s = pl.pallas_call(
        functools.partial(_sample_kernel, seq_len=SL, pos0=PAST_LEN),
        grid=(SB // G,),
        in_specs=[pl.BlockSpec((GT, D_MODEL), lambda i: (i, 0)),
                  pl.BlockSpec((GT, D_MAIN), lambda i: (i, 0)),
                  pl.BlockSpec((1, G, POOL_BUF, D_POOL), lambda i: (0, i, 0, 0)),
                  pl.BlockSpec((1, G, N_HEADS, HEAD_DIM, HEAD_DIM), lambda i: (0, i, 0, 0, 0)),
                  pl.BlockSpec((1, G, N_HEADS, HEAD_DIM), lambda i: (0, i, 0, 0)),
                  pl.BlockSpec((1, G, N_HEADS), lambda i: (0, i, 0))]
                 + [_const_spec(a.shape) for a in sample_w],
        out_specs=(pl.BlockSpec((GT, D_MODEL), lambda i: (i, 0)),
                   pl.BlockSpec((1, G, POOL_BUF, D_POOL), lambda i: (0, i, 0, 0)),
                   pl.BlockSpec((1, G, N_HEADS, HEAD_DIM, HEAD_DIM), lambda i: (0, i, 0, 0, 0)),
                   pl.BlockSpec((1, G, N_HEADS, HEAD_DIM), lambda i: (0, i, 0, 0)),
                   pl.BlockSpec((1, G, N_HEADS), lambda i: (0, i, 0))),
        out_shape=(jax.ShapeDtypeStruct((n_tok, D_MODEL), F32),
                   jax.ShapeDtypeStruct(state_pool.shape, F32),
                   jax.ShapeDtypeStruct(state_C.shape, F32),
                   jax.ShapeDtypeStruct(state_n.shape, F32),
                   jax.ShapeDtypeStruct(state_m.shape, F32)),
        scratch_shapes=[pltpu.VMEM((HIST + SL, D_POOL), F32), pltpu.VMEM((GT, 2048), BF16)],
        compiler_params=_params(("arbitrary",)),
        name="sample",
    )(xs, proj_s, state_pool, state_C, state_n, state_m, *sample_w)
    y_sample = y_s.reshape(SB, SL, D_MODEL)

    return (y_prompt, y_sample, pool_p, c_p, n_p, m_p, pool_s, c_s, n_s, m_s)
```

```python
import collections
import functools

import jax
import jax.numpy as jnp
from jax import lax
from jax.experimental import pallas as pl
from jax.experimental.pallas import tpu as pltpu

D_MODEL = 1024
D_POOL = 1024
D_MLSTM = 1024
N_HEADS = 4
HEAD_DIM = 256
POOL_WINDOWS = (2, 4, 8, 16)
POOL_BUF = 15
HIST = 16
N_META = 16
PAST_LEN = 16384
EPS = 1e-6
D_MAIN = 2 * D_POOL + 5 * D_MLSTM
GATE_PAD = 128
K_SCALE = HEAD_DIM ** -0.5

OFF_U, OFF_ZA, OFF_Q, OFF_K, OFF_V, OFF_O, OFF_ZB = (i * 1024 for i in range(7))

PROMPT_TILE = 256
SAMPLE_GROUP = 8
VMEM_LIMIT = 60000 * 1024

F32 = jnp.float32
BF16 = jnp.bfloat16

LayerW = collections.namedtuple("LayerW", "main gate bias norm1 pool pscale mhln out normf")


def _dot(a, b):
    return jnp.dot(a, b, preferred_element_type=F32)


def _dot_nt(a, b):
    return lax.dot_general(a, b, (((1,), (1,)), ((), ())), preferred_element_type=F32)


def _dot_tn(a, b):
    return lax.dot_general(a, b, (((0,), (0,)), ((), ())), preferred_element_type=F32)


def _rms(x, w):
    return x * lax.rsqrt(jnp.mean(x * x, axis=-1, keepdims=True) + EPS) * w


def _log_sigmoid(x):
    return jnp.minimum(x, 0.0) - jnp.log1p(jnp.exp(-jnp.abs(x)))


def _silu(x):
    return x * jax.nn.sigmoid(x)


def _split3(x):
    hi = x.astype(BF16)
    r = x - hi.astype(F32)
    mid = r.astype(BF16)
    lo = (r - mid.astype(F32)).astype(BF16)
    return hi, mid, lo


def _seq_mask(T, seq_len):
    row = lax.broadcasted_iota(jnp.int32, (T, T), 0)
    col = lax.broadcasted_iota(jnp.int32, (T, T), 1)
    causal = col <= row
    if seq_len < T:
        shift = seq_len.bit_length() - 1
        assert 1 << shift == seq_len
        causal = causal & ((row >> shift) == (col >> shift))
    return causal


def _to_rows(cols):
    T = cols.shape[0]
    pad = -T % 128
    if pad:
        cols = jnp.concatenate([cols, jnp.zeros((pad, cols.shape[1]), cols.dtype)], axis=0)
    return cols.T[:, 0:T]


def _gate_pre(xn, w):
    return _dot(xn, w.gate[...]) + w.bias[...]


def _cumsum_operands(g_col, causal):
    tri = jnp.where(causal, 1.0, 0.0).astype(BF16)
    return tri, _split3(_log_sigmoid(g_col))


def _cumsum(tri, parts):
    return sum(_dot(tri, p) for p in parts)


def _intra_weights(qb, kb, b_col, a_col, r_row, causal):
    dm = jnp.where(causal, b_col + r_row, -jnp.inf)
    m_t = jnp.maximum(a_col, jnp.max(dm, axis=-1, keepdims=True))
    w = jnp.exp(dm - m_t)
    inter = jnp.exp(a_col - m_t)
    return m_t, inter, _dot_nt(qb, kb) * (w * K_SCALE)


def _head_norm(ht, w_row):
    mu = jnp.mean(ht, axis=-1, keepdims=True)
    d = ht - mu
    var = jnp.mean(d * d, axis=-1, keepdims=True)
    return d * lax.rsqrt(var + EPS) * w_row


def _window_sums(ext):
    s2 = ext + pltpu.roll(ext, 1, axis=0)
    s4 = s2[:, 256:] + pltpu.roll(s2[:, 256:], 2, axis=0)
    s8 = s4[:, 256:] + pltpu.roll(s4[:, 256:], 4, axis=0)
    s16 = s8[:, 256:] + pltpu.roll(s8[:, 256:], 8, axis=0)
    return [s2[HIST:, 0:256], s4[HIST:, 0:256], s8[HIST:, 0:256], s16[HIST:, 0:256]]


def _pooled(ext, u, pos_col):
    sums = _window_sums(ext)
    return [sums[g] * (1.0 / jnp.minimum(float(w), pos_col + 1.0)) - u[:, g * 256:(g + 1) * 256]
            for g, w in enumerate(POOL_WINDOWS)]


def _pool_mix(pooled, wpool_ref):
    return jnp.concatenate([_dot(p.astype(BF16), wpool_ref[g]) for g, p in enumerate(pooled)], axis=-1)


def _inproj_steps(x_ref, scr, w):
    proj_scr, x_scr, gcol_scr, _, xn_scr = scr
    half = 512

    def norm():
        x = x_ref[...]
        x_scr[...] = x
        xn_scr[...] = _rms(x, w.norm1[...]).astype(BF16)

    def piece(c0):
        def run():
            proj_scr[:, c0:c0 + half] = _dot(xn_scr[...], w.main[:, c0:c0 + half])
        return run

    def gates():
        gcol_scr[...] = _gate_pre(xn_scr[...], w)

    return [norm] + [piece(c0) for c0 in range(0, D_MAIN, half)] + [gates]


def _post_steps(scr, pos0, w, state, y_ref):
    proj_scr, x_scr, gcol_scr, ycat_scr, _ = scr
    hist_scr, c_scr, n_scr, m_scr = state
    T = x_scr.shape[0]

    def seg(off, h=None):
        if h is None:
            return proj_scr[:, off:off + 1024]
        return proj_scr[:, off + h * 256:off + (h + 1) * 256]

    causal = _seq_mask(T, T)
    g_col = gcol_scr[...]
    tri, lf_parts = _cumsum_operands(g_col, causal)
    yield
    b_col_all = _cumsum(tri, lf_parts)
    r_row_all = _to_rows(g_col - pltpu.roll(b_col_all, GATE_PAD - N_HEADS, axis=1))

    u = seg(OFF_U)
    ext = jnp.concatenate([hist_scr[...], u], axis=0)
    pos_col = (lax.broadcasted_iota(jnp.int32, (T, 1), 0) + pos0).astype(F32)
    pooled = _pooled(ext, u, pos_col)
    hist_scr[...] = ext[T:T + HIST, :]
    yield
    mixed = _pool_mix(pooled, w.pool)
    ycat_scr[:, 0:1024] = (mixed * w.pscale[...] * _silu(seg(OFF_ZA))).astype(BF16)

    for h in range(N_HEADS):
        q, k, v = seg(OFF_Q, h), seg(OFF_K, h), seg(OFF_V, h)
        qb, kb, vb = q.astype(BF16), k.astype(BF16), v.astype(BF16)
        ig_col = g_col[:, h:h + 1]
        b_col = b_col_all[:, 4 + h:5 + h]
        r_row = r_row_all[h:h + 1, :]
        m_prev = m_scr[h:h + 1, 0:1]
        a_col = b_col + m_prev
        c_old = c_scr[h]
        n_old = n_scr[h:h + 1, :]
        q_c = _dot(qb, c_old.astype(BF16))
        m_t, inter, s = _intra_weights(qb, kb, b_col, a_col, r_row, causal)
        sb = s.astype(BF16)
        yield
        num = inter * q_c + _dot(sb, vb)
        qn = inter * jnp.sum(q * n_old, axis=-1, keepdims=True) + jnp.sum(s, axis=-1, keepdims=True)
        ht = num * (1.0 / jnp.maximum(jnp.abs(qn), jnp.exp(-m_t)))
        m_new = m_t[T - 1:T, :]
        b_last = b_col[T - 1:T, :]
        w_end = jnp.exp(b_last - b_col + ig_col - m_new) * K_SCALE
        decay = jnp.exp(b_last + m_prev - m_new)
        kw = k * w_end
        kwb = kw.astype(BF16)
        yield
        c_scr[h] = decay * c_old + _dot_tn(kwb, vb)
        n_scr[h:h + 1, :] = decay * n_old + jnp.sum(kw, axis=0, keepdims=True)
        m_scr[h:h + 1, :] = jnp.broadcast_to(m_new, (1, 128))
        hn = _head_norm(ht, w.mhln[:, h * 256:(h + 1) * 256])
        ycat_scr[:, 1024 + h * 256:1024 + (h + 1) * 256] = (
            hn * jax.nn.sigmoid(seg(OFF_O, h)) * _silu(seg(OFF_ZB, h))).astype(BF16)

    yield
    y = _dot(ycat_scr[...], w.out[...])
    y_ref[...] = _rms(x_scr[...] + y, w.normf[...])


POST_YIELDS = 3 + 2 * N_HEADS


def _run_interleaved(post, pieces):
    assert len(pieces) >= POST_YIELDS + 2
    rest = iter(pieces)
    next(rest)()
    next(rest)()
    for i in range(POST_YIELDS):
        next(post)
        next(rest)()
        if i == POST_YIELDS - 1:
            for piece in rest:
                piece()
    for _ in post:
        raise AssertionError("unexpected extra yield")


def _m_row(m_scr):
    lane = lax.broadcasted_iota(jnp.int32, (1, 128), 1)
    row = jnp.zeros((1, 128), F32)
    for h in range(N_HEADS):
        row = jnp.where(lane == h, m_scr[h:h + 1, :], row)
    return row


def _prompt_kernel(meta_ref, x0_ref, xb_ref, xc_ref, *refs, steps_per_seq):
    w = LayerW(*refs[:9])
    y_ref, pool_out, c_out, n_out, m_out = refs[9:14]
    scr0, scr1, scr_meta = refs[14:19], refs[19:24], refs[24:29]
    y_meta = refs[29]
    state, state_meta = refs[30:34], refs[34:38]
    hist_scr, c_scr, n_scr, m_scr = state
    s = pl.program_id(0)
    TT = PROMPT_TILE

    @pl.when(s == 0)
    def _():
        for ref in state_meta:
            ref[...] = jnp.zeros_like(ref)
        for p in _inproj_steps(meta_ref, scr_meta, w):
            p()
        for _ in _post_steps(scr_meta, 0, w, state_meta, y_meta):
            pass
        for p in _inproj_steps(x0_ref, scr0, w):
            p()

    @pl.when(s % steps_per_seq == 0)
    def _():
        for ref, ref_meta in zip(state, state_meta):
            ref[...] = ref_meta[...]

    _run_interleaved(_post_steps(scr0, N_META, w, state, y_ref.at[0:TT, :]),
                     _inproj_steps(xb_ref, scr1, w))
    _run_interleaved(_post_steps(scr1, N_META, w, state, y_ref.at[TT:2 * TT, :]),
                     _inproj_steps(xc_ref, scr0, w))

    @pl.when(s % steps_per_seq == steps_per_seq - 1)
    def _():
        b = s // steps_per_seq
        pool_out[0, 0] = hist_scr[1:HIST, :]
        c_out[0, 0] = c_scr[...]
        n_out[0, 0] = n_scr[0:N_HEADS, :]
        m_out[pl.ds(b, 1), :] = _m_row(m_scr)


def _wprep_kernel(wt_ref, wgt_ref, o_ref, og_ref):
    o_ref[...] = wt_ref[...].T.astype(BF16)

    @pl.when(pl.program_id(0) == 0)
    def _():
        g = jnp.concatenate([wgt_ref[...], jnp.zeros((GATE_PAD - 8, D_MODEL), F32)], axis=0)
        og_ref[...] = g.T.astype(BF16)


def _sproj_kernel(x_ref, norm1_ref, wmain_ref, proj_ref):
    xn = _rms(x_ref[...], norm1_ref[...]).astype(BF16)
    for seg in range(7):
        cols = slice(seg * 1024, (seg + 1) * 1024)
        proj_ref[:, cols] = _dot(xn, wmain_ref[:, cols])


def _expand(seq_col, vals):
    out = None
    for i, val in enumerate(vals):
        pick = jnp.where(seq_col == i, val, 0.0)
        out = pick if out is None else out + pick
    return out


def _sample_kernel(x_ref, proj_ref, pool_ref, c_ref, n_ref, m_ref, *refs, seq_len, pos0):
    w = LayerW(None, *refs[:8])
    y_ref, pool_out, c_out, n_out, m_out = refs[8:13]
    ext_scr, ycat_scr = refs[13:15]
    G = SAMPLE_GROUP
    T = G * seq_len
    x = x_ref[...]
    xn = _rms(x, w.norm1[...]).astype(BF16)
    causal = _seq_mask(T, seq_len)
    g_col = _gate_pre(xn, w)
    b_col_all = _cumsum(*_cumsum_operands(g_col, causal))
    r_row_all = _to_rows(g_col - pltpu.roll(b_col_all, GATE_PAD - N_HEADS, axis=1))
    seq_col = lax.broadcasted_iota(jnp.int32, (T, 1), 0) >> (seq_len.bit_length() - 1)
    pos_col = jnp.full((seq_len, 1), float(pos0), F32) + lax.broadcasted_iota(
        jnp.int32, (seq_len, 1), 0).astype(F32)

    pooled_rows = []
    for i in range(G):
        rows = slice(i * seq_len, (i + 1) * seq_len)
        u_i = proj_ref[rows, OFF_U:OFF_U + 1024]
        ext_scr[0:1, :] = jnp.zeros((1, 1024), F32)
        ext_scr[1:HIST, :] = pool_ref[0, i]
        ext_scr[HIST:HIST + seq_len, :] = u_i
        ext = ext_scr[...]
        pooled_rows.append(_pooled(ext, u_i, pos_col))
        pool_out[0, i] = ext[seq_len + 1:seq_len + HIST, :]
    mixed = _pool_mix([jnp.concatenate(p, axis=0) for p in zip(*pooled_rows)], w.pool)
    y_a = mixed * w.pscale[...] * _silu(proj_ref[:, OFF_ZA:OFF_ZA + 1024])
    ycat_scr[:, 0:1024] = y_a.astype(BF16)

    lane4 = lax.broadcasted_iota(jnp.int32, (1, N_HEADS), 1)
    m_rows = [jnp.zeros((1, N_HEADS), F32) for _ in range(G)]
    for h in range(N_HEADS):
        hc = slice(h * 256, (h + 1) * 256)
        q = proj_ref[:, OFF_Q + h * 256:OFF_Q + (h + 1) * 256]
        k = proj_ref[:, OFF_K + h * 256:OFF_K + (h + 1) * 256]
        v = proj_ref[:, OFF_V + h * 256:OFF_V + (h + 1) * 256]
        qb, kb, vb = q.astype(BF16), k.astype(BF16), v.astype(BF16)
        ig_col = g_col[:, h:h + 1]
        b_col = b_col_all[:, 4 + h:5 + h]
        r_row = r_row_all[h:h + 1, :]
        m_prev = [m_ref[0, i:i + 1, h:h + 1] for i in range(G)]
        a_col = b_col + _expand(seq_col, m_prev)
        m_t, inter, s = _intra_weights(qb, kb, b_col, a_col, r_row, causal)
        sv, s_sum = _dot(s.astype(BF16), vb), jnp.sum(s, axis=-1, keepdims=True)
        c_old = [c_ref[0, i, h] for i in range(G)]
        n_old = [n_ref[0, i, h:h + 1, :] for i in range(G)]
        q_c = _expand(seq_col, [_dot(qb, c.astype(BF16)) for c in c_old])
        num = inter * q_c + sv
        qn = inter * jnp.sum(q * _expand(seq_col, n_old), axis=-1, keepdims=True) + s_sum
        ht = num * (1.0 / jnp.maximum(jnp.abs(qn), jnp.exp(-m_t)))
        last = [(i + 1) * seq_len - 1 for i in range(G)]
        m_new = [m_t[r:r + 1, :] for r in last]
        b_last = [b_col[r:r + 1, :] for r in last]
        w_end = jnp.exp(_expand(seq_col, b_last) - b_col + ig_col - _expand(seq_col, m_new)) * K_SCALE
        kw = k * w_end
        kwb = kw.astype(BF16)
        for i in range(G):
            decay = jnp.exp(b_last[i] + m_prev[i] - m_new[i])
            v_i = jnp.where(seq_col == i, vb, jnp.zeros_like(vb))
            c_out[0, i, h] = decay * c_old[i] + _dot_tn(kwb, v_i)
            n_out[0, i, h:h + 1, :] = decay * n_old[i] + jnp.sum(
                kw[i * seq_len:(i + 1) * seq_len, :], axis=0, keepdims=True)
            m_rows[i] = jnp.where(lane4 == h, m_new[i], m_rows[i])
        hn = _head_norm(ht, w.mhln[:, hc])
        o = proj_ref[:, OFF_O + h * 256:OFF_O + (h + 1) * 256]
        zb = proj_ref[:, OFF_ZB + h * 256:OFF_ZB + (h + 1) * 256]
        ycat_scr[:, 1024 + h * 256:1024 + (h + 1) * 256] = (hn * jax.nn.sigmoid(o) * _silu(zb)).astype(BF16)
    for i in range(G):
        m_out[0, i:i + 1, :] = m_rows[i]

    y = _dot(ycat_scr[...], w.out[...])
    y_ref[...] = _rms(x + y, w.normf[...])


def _const_spec(shape):
    nd = len(shape)
    return pl.BlockSpec(shape, lambda *_: (0,) * nd, pipeline_mode=pl.Buffered(1))


def _params(sem):
    return pltpu.CompilerParams(dimension_semantics=sem, vmem_limit_bytes=VMEM_LIMIT)


def _tile_scratch(T):
    return [pltpu.VMEM((T, D_MAIN), F32), pltpu.VMEM((T, D_MODEL), F32), pltpu.VMEM((T, GATE_PAD), F32),
            pltpu.VMEM((T, 2048), BF16), pltpu.VMEM((T, D_MODEL), BF16)]


def _state_scratch():
    return [pltpu.VMEM((HIST, D_POOL), F32), pltpu.VMEM((N_HEADS, HEAD_DIM, HEAD_DIM), F32),
            pltpu.VMEM((8, HEAD_DIM), F32), pltpu.VMEM((8, 128), F32)]


def kernel(x_prompt, x_sample, state_pool, state_C, state_n, state_m, meta_tokens, norm1_w, w_in,
           b_if, w_pool, pool_scale, mhln_w, w_out, normf_w):
    B, S, _ = x_prompt.shape
    SB, SL, _ = x_sample.shape
    TT = PROMPT_TILE
    assert norm1_w.shape[0] == 1, "single layer"
    assert S % (2 * TT) == 0 and SB % SAMPLE_GROUP == 0

    w_in_t = jnp.swapaxes(w_in[0], 0, 1)
    w_main, w_gate = pl.pallas_call(
        _wprep_kernel,
        grid=(D_MAIN // 1024,),
        in_specs=[pl.BlockSpec((1024, D_MODEL), lambda i: (i, 0)),
                  pl.BlockSpec((8, D_MODEL), lambda i: (D_MAIN // 8, 0))],
        out_specs=(pl.BlockSpec((D_MODEL, 1024), lambda i: (0, i)),
                   pl.BlockSpec((D_MODEL, GATE_PAD), lambda i: (0, 0))),
        out_shape=(jax.ShapeDtypeStruct((D_MODEL, D_MAIN), BF16),
                   jax.ShapeDtypeStruct((D_MODEL, GATE_PAD), BF16)),
        compiler_params=_params(("arbitrary",)),
        name="wprep",
    )(w_in_t, w_in_t)
    bias_row = jnp.pad(b_if[0], (0, GATE_PAD - 8)).reshape(1, GATE_PAD)
    norm1 = norm1_w[0].reshape(1, D_MODEL)
    wpool = w_pool[0].astype(BF16)
    pscale = pool_scale[0].reshape(1, D_POOL)
    mhln = mhln_w[0].reshape(1, D_MLSTM)
    wout = w_out[0].astype(BF16)
    normf = normf_w.reshape(1, D_MODEL)
    layer_w = LayerW(w_main, w_gate, bias_row, norm1, wpool, pscale, mhln, wout, normf)
    layer_specs = [_const_spec(a.shape) for a in layer_w]

    n_tiles = B * S // TT
    steps = n_tiles // 2
    steps_per_seq = S // (2 * TT)
    xp = x_prompt.reshape(B * S, D_MODEL)
    y_p, pool_p, c_p, n_p, m_p = pl.pallas_call(
        functools.partial(_prompt_kernel, steps_per_seq=steps_per_seq),
        grid=(steps,),
        in_specs=[_const_spec(meta_tokens.shape),
                  pl.BlockSpec((TT, D_MODEL), lambda s: (0, 0)),
                  pl.BlockSpec((TT, D_MODEL), lambda s: (2 * s + 1, 0)),
                  pl.BlockSpec((TT, D_MODEL), lambda s: (jnp.minimum(2 * s + 2, n_tiles - 1), 0))]
                 + layer_specs,
        out_specs=(pl.BlockSpec((2 * TT, D_MODEL), lambda s: (s, 0)),
                   pl.BlockSpec((1, 1, POOL_BUF, D_POOL), lambda s: (0, s // steps_per_seq, 0, 0)),
                   pl.BlockSpec((1, 1, N_HEADS, HEAD_DIM, HEAD_DIM),
                                lambda s: (0, s // steps_per_seq, 0, 0, 0)),
                   pl.BlockSpec((1, 1, N_HEADS, HEAD_DIM), lambda s: (0, s // steps_per_seq, 0, 0)),
                   pl.BlockSpec((B, 128), lambda s: (0, 0))),
        out_shape=(jax.ShapeDtypeStruct((B * S, D_MODEL), F32),
                   jax.ShapeDtypeStruct((1, B, POOL_BUF, D_POOL), F32),
                   jax.ShapeDtypeStruct((1, B, N_HEADS, HEAD_DIM, HEAD_DIM), F32),
                   jax.ShapeDtypeStruct((1, B, N_HEADS, HEAD_DIM), F32),
                   jax.ShapeDtypeStruct((B, 128), F32)),
        scratch_shapes=_tile_scratch(TT) + _tile_scratch(TT) + _tile_scratch(N_META)
        + [pltpu.VMEM((N_META, D_MODEL), F32)] + _state_scratch() + _state_scratch(),
        compiler_params=_params(("arbitrary",)),
        name="prompt",
    )(meta_tokens, xp, xp, xp, *layer_w)
    y_prompt = y_p.reshape(B, S, D_MODEL)
    m_p = m_p[:, :N_HEADS].reshape(1, B, N_HEADS)

    n_tok = SB * SL
    xs = x_sample.reshape(n_tok, D_MODEL)
    PT = 256
    proj_s = pl.pallas_call(
        _sproj_kernel,
        grid=(n_tok // PT,),
        in_specs=[pl.BlockSpec((PT, D_MODEL), lambda i: (i, 0)), _const_spec(norm1.shape),
                  _const_spec(w_main.shape)],
        out_specs=pl.BlockSpec((PT, D_MAIN), lambda i: (i, 0)),
        out_shape=jax.ShapeDtypeStruct((n_tok, D_MAIN), F32),
        compiler_params=_params(("arbitrary",)),
        name="sproj",
    )(xs, norm1, w_main)

    G = SAMPLE_GROUP
    GT = G * SL
    sample_w = layer_w[1:]
    y_s, pool_s, c_s, n_s, m_s = pl.pallas_call(
        functools.partial(_sample_kernel, seq_len=SL, pos0=PAST_LEN),
        grid=(SB // G,),
        in_specs=[pl.BlockSpec((GT, D_MODEL), lambda i: (i, 0)),
                  pl.BlockSpec((GT, D_MAIN), lambda i: (i, 0)),
                  pl.BlockSpec((1, G, POOL_BUF, D_POOL), lambda i: (0, i, 0, 0)),
                  pl.BlockSpec((1, G, N_HEADS, HEAD_DIM, HEAD_DIM), lambda i: (0, i, 0, 0, 0)),
                  pl.BlockSpec((1, G, N_HEADS, HEAD_DIM), lambda i: (0, i, 0, 0)),
                  pl.BlockSpec((1, G, N_HEADS), lambda i: (0, i, 0))]
                 + [_const_spec(a.shape) for a in sample_w],
        out_specs=(pl.BlockSpec((GT, D_MODEL), lambda i: (i, 0)),
                   pl.BlockSpec((1, G, POOL_BUF, D_POOL), lambda i: (0, i, 0, 0)),
                   pl.BlockSpec((1, G, N_HEADS, HEAD_DIM, HEAD_DIM), lambda i: (0, i, 0, 0, 0)),
                   pl.BlockSpec((1, G, N_HEADS, HEAD_DIM), lambda i: (0, i, 0, 0)),
                   pl.BlockSpec((1, G, N_HEADS), lambda i: (0, i, 0))),
        out_shape=(jax.ShapeDtypeStruct((n_tok, D_MODEL), F32),
                   jax.ShapeDtypeStruct(state_pool.shape, F32),
                   jax.ShapeDtypeStruct(state_C.shape, F32),
                   jax.ShapeDtypeStruct(state_n.shape, F32),
                   jax.ShapeDtypeStruct(state_m.shape, F32)),
        scratch_shapes=[pltpu.VMEM((HIST + SL, D_POOL), F32), pltpu.VMEM((GT, 2048), BF16)],
        compiler_params=_params(("arbitrary",)),
        name="sample",
    )(xs, proj_s, state_pool, state_C, state_n, state_m, *sample_w)
    y_sample = y_s.reshape(SB, SL, D_MODEL)

    return (y_prompt, y_sample, pool_p, c_p, n_p, m_p, pool_s, c_s, n_s, m_s)
```

```python
import collections
import functools

import jax
import jax.numpy as jnp
from jax import lax
from jax.experimental import pallas as pl
from jax.experimental.pallas import tpu as pltpu

D_MODEL = 1024
D_POOL = 1024
D_MLSTM = 1024
N_HEADS = 4
HEAD_DIM = 256
POOL_WINDOWS = (2, 4, 8, 16)
POOL_BUF = 15
HIST = 16
N_META = 16
PAST_LEN = 16384
EPS = 1e-6
D_MAIN = 2 * D_POOL + 5 * D_MLSTM
GATE_PAD = 128
K_SCALE = HEAD_DIM ** -0.5

OFF_U, OFF_ZA, OFF_Q, OFF_K, OFF_V, OFF_O, OFF_ZB = (i * 1024 for i in range(7))

PROMPT_TILE = 256
SAMPLE_GROUP = 8
VMEM_LIMIT = 60000 * 1024

F32 = jnp.float32
BF16 = jnp.bfloat16

LayerW = collections.namedtuple("LayerW", "main gate bias norm1 pool pscale mhln out normf")


def _dot(a, b):
    return jnp.dot(a, b, preferred_element_type=F32)


def _dot_nt(a, b):
    return lax.dot_general(a, b, (((1,), (1,)), ((), ())), preferred_element_type=F32)


def _dot_tn(a, b):
    return lax.dot_general(a, b, (((0,), (0,)), ((), ())), preferred_element_type=F32)


def _rms(x, w):
    return x * lax.rsqrt(jnp.mean(x * x, axis=-1, keepdims=True) + EPS) * w


def _log_sigmoid(x):
    return jnp.minimum(x, 0.0) - jnp.log1p(jnp.exp(-jnp.abs(x)))


def _silu(x):
    return x * jax.nn.sigmoid(x)


def _split3(x):
    hi = x.astype(BF16)
    r = x - hi.astype(F32)
    mid = r.astype(BF16)
    lo = (r - mid.astype(F32)).astype(BF16)
    return hi, mid, lo


def _seq_mask(T, seq_len):
    row = lax.broadcasted_iota(jnp.int32, (T, T), 0)
    col = lax.broadcasted_iota(jnp.int32, (T, T), 1)
    causal = col <= row
    if seq_len < T:
        shift = seq_len.bit_length() - 1
        assert 1 << shift == seq_len
        causal = causal & ((row >> shift) == (col >> shift))
    return causal


def _to_rows(cols):
    T = cols.shape[0]
    pad = -T % 128
    if pad:
        cols = jnp.concatenate([cols, jnp.zeros((pad, cols.shape[1]), cols.dtype)], axis=0)
    return cols.T[:, 0:T]


def _gate_pre(xn, w):
    return _dot(xn, w.gate[...]) + w.bias[...]


def _cumsum_operands(g_col, causal):
    tri = jnp.where(causal, 1.0, 0.0).astype(BF16)
    return tri, _split3(_log_sigmoid(g_col))


def _cumsum(tri, parts):
    return sum(_dot(tri, p) for p in parts)


def _intra_weights(qb, kb, b_col, a_col, r_row, causal):
    dm = jnp.where(causal, b_col + r_row, -jnp.inf)
    m_t = jnp.maximum(a_col, jnp.max(dm, axis=-1, keepdims=True))
    w = jnp.exp(dm - m_t)
    inter = jnp.exp(a_col - m_t)
    return m_t, inter, _dot_nt(qb, kb) * (w * K_SCALE)


def _head_norm(ht, w_row):
    mu = jnp.mean(ht, axis=-1, keepdims=True)
    d = ht - mu
    var = jnp.mean(d * d, axis=-1, keepdims=True)
    return d * lax.rsqrt(var + EPS) * w_row


def _window_sums(ext):
    s2 = ext + pltpu.roll(ext, 1, axis=0)
    s4 = s2[:, 256:] + pltpu.roll(s2[:, 256:], 2, axis=0)
    s8 = s4[:, 256:] + pltpu.roll(s4[:, 256:], 4, axis=0)
    s16 = s8[:, 256:] + pltpu.roll(s8[:, 256:], 8, axis=0)
    return [s2[HIST:, 0:256], s4[HIST:, 0:256], s8[HIST:, 0:256], s16[HIST:, 0:256]]


def _pooled(ext, u, pos_col):
    sums = _window_sums(ext)
    return [sums[g] * (1.0 / jnp.minimum(float(w), pos_col + 1.0)) - u[:, g * 256:(g + 1) * 256]
            for g, w in enumerate(POOL_WINDOWS)]


def _pool_mix(pooled, wpool_ref):
    return jnp.concatenate([_dot(p.astype(BF16), wpool_ref[g]) for g, p in enumerate(pooled)], axis=-1)


def _inproj_steps(x_ref, scr, w):
    proj_scr, x_scr, gcol_scr, _, xn_scr = scr
    half = PIECE_COLS

    def norm():
        x = x_ref[...]
        x_scr[...] = x
        xn_scr[...] = _rms(x, w.norm1[...]).astype(BF16)

    def piece(c0):
        def run():
            proj_scr[:, c0:c0 + half] = _dot(xn_scr[...], w.main[:, c0:c0 + half])
        return run

    def gates():
        gcol_scr[...] = _gate_pre(xn_scr[...], w)

    pieces = [piece(c0) for c0 in range(0, D_MAIN, half)]
    keep = len(pieces) - CARRY
    return [norm] + pieces[:keep] + [gates], pieces[keep:]


def _post_steps(scr, pos0, w, state, y_ref):
    proj_scr, x_scr, gcol_scr, ycat_scr, _ = scr
    hist_scr, c_scr, n_scr, m_scr = state
    T = x_scr.shape[0]

    def seg(off, h=None):
        if h is None:
            return proj_scr[:, off:off + 1024]
        return proj_scr[:, off + h * 256:off + (h + 1) * 256]

    causal = _seq_mask(T, T)
    g_col = gcol_scr[...]
    tri, lf_parts = _cumsum_operands(g_col, causal)
    yield
    b_col_all = _cumsum(tri, lf_parts)
    r_row_all = _to_rows(g_col - pltpu.roll(b_col_all, GATE_PAD - N_HEADS, axis=1))

    u = seg(OFF_U)
    ext = jnp.concatenate([hist_scr[...], u], axis=0)
    pos_col = (lax.broadcasted_iota(jnp.int32, (T, 1), 0) + pos0).astype(F32)
    pooled = _pooled(ext, u, pos_col)
    hist_scr[...] = ext[T:T + HIST, :]
    yield
    mixed = _pool_mix(pooled, w.pool)
    ycat_scr[:, 0:1024] = (mixed * w.pscale[...] * _silu(seg(OFF_ZA))).astype(BF16)

    for h in range(N_HEADS):
        q, k, v = seg(OFF_Q, h), seg(OFF_K, h), seg(OFF_V, h)
        qb, kb, vb = q.astype(BF16), k.astype(BF16), v.astype(BF16)
        ig_col = g_col[:, h:h + 1]
        b_col = b_col_all[:, 4 + h:5 + h]
        r_row = r_row_all[h:h + 1, :]
        m_prev = m_scr[h:h + 1, 0:1]
        a_col = b_col + m_prev
        c_old = c_scr[h]
        n_old = n_scr[h:h + 1, :]
        q_c = _dot(qb, c_old.astype(BF16))
        m_t, inter, s = _intra_weights(qb, kb, b_col, a_col, r_row, causal)
        sb = s.astype(BF16)
        yield
        num = inter * q_c + _dot(sb, vb)
        qn = inter * jnp.sum(q * n_old, axis=-1, keepdims=True) + jnp.sum(s, axis=-1, keepdims=True)
        ht = num * (1.0 / jnp.maximum(jnp.abs(qn), jnp.exp(-m_t)))
        m_new = m_t[T - 1:T, :]
        b_last = b_col[T - 1:T, :]
        w_end = jnp.exp(b_last - b_col + ig_col - m_new) * K_SCALE
        decay = jnp.exp(b_last + m_prev - m_new)
        kw = k * w_end
        kwb = kw.astype(BF16)
        yield
        c_scr[h] = decay * c_old + _dot_tn(kwb, vb)
        n_scr[h:h + 1, :] = decay * n_old + jnp.sum(kw, axis=0, keepdims=True)
        m_scr[h:h + 1, :] = jnp.broadcast_to(m_new, (1, 128))
        hn = _head_norm(ht, w.mhln[:, h * 256:(h + 1) * 256])
        ycat_scr[:, 1024 + h * 256:1024 + (h + 1) * 256] = (
            hn * jax.nn.sigmoid(seg(OFF_O, h)) * _silu(seg(OFF_ZB, h))).astype(BF16)

    yield
    y = _dot(ycat_scr[...], w.out[...])
    y_ref[...] = _rms(x_scr[...] + y, w.normf[...])


PIECE_COLS = 512
CARRY = 0
PIECE_PLAN = (CARRY + 2, (1,) * (2 + 2 * N_HEADS) + (4,), 0)


def _run_interleaved(post, pieces):
    lead, at_yield, tail = PIECE_PLAN
    assert lead + sum(at_yield) + tail == len(pieces)
    rest = iter(pieces)
    for _ in range(lead):
        next(rest)()
    for count in at_yield:
        next(post)
        for _ in range(count):
            next(rest)()
    for _ in post:
        raise AssertionError("unexpected extra yield")
    for piece in rest:
        piece()


def _m_row(m_scr):
    lane = lax.broadcasted_iota(jnp.int32, (1, 128), 1)
    row = jnp.zeros((1, 128), F32)
    for h in range(N_HEADS):
        row = jnp.where(lane == h, m_scr[h:h + 1, :], row)
    return row


def _prompt_kernel(meta_ref, x0_ref, xb_ref, xc_ref, *refs, steps_per_seq):
    w = LayerW(*refs[:9])
    y_ref, pool_out, c_out, n_out, m_out = refs[9:14]
    scr0, scr1, scr_meta = refs[14:19], refs[19:24], refs[24:29]
    y_meta = refs[29]
    state, state_meta = refs[30:34], refs[34:38]
    hist_scr, c_scr, n_scr, m_scr = state
    s = pl.program_id(0)
    TT = PROMPT_TILE

    @pl.when(s == 0)
    def _():
        for ref in state_meta:
            ref[...] = jnp.zeros_like(ref)
        for p in sum(_inproj_steps(meta_ref, scr_meta, w), []):
            p()
        for _ in _post_steps(scr_meta, 0, w, state_meta, y_meta):
            pass
        for p in _inproj_steps(x0_ref, scr0, w)[0]:
            p()

    @pl.when(s % steps_per_seq == 0)
    def _():
        for ref, ref_meta in zip(state, state_meta):
            ref[...] = ref_meta[...]

    first_b, carried_1 = _inproj_steps(xb_ref, scr1, w)
    first_c, carried_0 = _inproj_steps(xc_ref, scr0, w)
    _run_interleaved(_post_steps(scr0, N_META, w, state, y_ref.at[0:TT, :]), carried_0 + first_b)
    _run_interleaved(_post_steps(scr1, N_META, w, state, y_ref.at[TT:2 * TT, :]), carried_1 + first_c)

    @pl.when(s % steps_per_seq == steps_per_seq - 1)
    def _():
        b = s // steps_per_seq
        for j in range(POOL_BUF):
            pool_out[0, j, pl.ds(b, 1), :] = hist_scr[1 + j:2 + j, :]
        c_out[0, 0] = c_scr[...]
        n_out[0, 0] = n_scr[0:N_HEADS, :]
        m_out[pl.ds(b, 1), :] = _m_row(m_scr)


def _prep_kernel(wt_ref, wgt_ref, xs_ref, norm1_ref, wout_ref, wmain_ref, wgate_ref, proj_ref, woutb_ref,
                 xn_scr, *, n_seg):
    i = pl.program_id(0)

    @pl.when(i == 0)
    def _():
        g = jnp.concatenate([wgt_ref[...], jnp.zeros((GATE_PAD - 8, D_MODEL), F32)], axis=0)
        wgate_ref[...] = g.T.astype(BF16)
        xn_scr[...] = _rms(xs_ref[...], norm1_ref[...]).astype(BF16)

    woutb_ref[...] = wout_ref[...].astype(BF16)

    @pl.when(i < n_seg)
    def _():
        wb = wt_ref[...].T.astype(BF16)
        wmain_ref[...] = wb
        proj_ref[...] = _dot(xn_scr[...], wb)


def _expand(seq_col, vals):
    out = None
    for i, val in enumerate(vals):
        pick = jnp.where(seq_col == i, val, 0.0)
        out = pick if out is None else out + pick
    return out


def _sample_kernel(x_ref, proj_ref, pool_ref, c_ref, n_ref, m_ref, *refs, seq_len, pos0):
    w = LayerW(None, *refs[:8])
    y_ref, pool_out, c_out, n_out, m_out = refs[8:13]
    ext_scr, ycat_scr = refs[13:15]
    G = SAMPLE_GROUP
    T = G * seq_len
    x = x_ref[...]
    xn = _rms(x, w.norm1[...]).astype(BF16)
    causal = _seq_mask(T, seq_len)
    g_col = _gate_pre(xn, w)
    b_col_all = _cumsum(*_cumsum_operands(g_col, causal))
    r_row_all = _to_rows(g_col - pltpu.roll(b_col_all, GATE_PAD - N_HEADS, axis=1))
    seq_col = lax.broadcasted_iota(jnp.int32, (T, 1), 0) >> (seq_len.bit_length() - 1)
    pos_col = jnp.full((seq_len, 1), float(pos0), F32) + lax.broadcasted_iota(
        jnp.int32, (seq_len, 1), 0).astype(F32)

    pooled_rows = []
    for i in range(G):
        rows = slice(i * seq_len, (i + 1) * seq_len)
        u_i = proj_ref[rows, OFF_U:OFF_U + 1024]
        ext_scr[0:1, :] = jnp.zeros((1, 1024), F32)
        for j in range(POOL_BUF):
            ext_scr[1 + j:2 + j, :] = pool_ref[0, j, i:i + 1, :]
        ext_scr[HIST:HIST + seq_len, :] = u_i
        ext = ext_scr[...]
        pooled_rows.append(_pooled(ext, u_i, pos_col))
        for j in range(POOL_BUF):
            pool_out[0, j, i:i + 1, :] = ext[seq_len + 1 + j:seq_len + 2 + j, :]
    mixed = _pool_mix([jnp.concatenate(p, axis=0) for p in zip(*pooled_rows)], w.pool)
    y_a = mixed * w.pscale[...] * _silu(proj_ref[:, OFF_ZA:OFF_ZA + 1024])
    ycat_scr[:, 0:1024] = y_a.astype(BF16)

    lane4 = lax.broadcasted_iota(jnp.int32, (1, N_HEADS), 1)
    m_rows = [jnp.zeros((1, N_HEADS), F32) for _ in range(G)]
    for h in range(N_HEADS):
        hc = slice(h * 256, (h + 1) * 256)
        q = proj_ref[:, OFF_Q + h * 256:OFF_Q + (h + 1) * 256]
        k = proj_ref[:, OFF_K + h * 256:OFF_K + (h + 1) * 256]
        v = proj_ref[:, OFF_V + h * 256:OFF_V + (h + 1) * 256]
        qb, kb, vb = q.astype(BF16), k.astype(BF16), v.astype(BF16)
        ig_col = g_col[:, h:h + 1]
        b_col = b_col_all[:, 4 + h:5 + h]
        r_row = r_row_all[h:h + 1, :]
        m_prev = [m_ref[0, i:i + 1, h:h + 1] for i in range(G)]
        a_col = b_col + _expand(seq_col, m_prev)
        m_t, inter, s = _intra_weights(qb, kb, b_col, a_col, r_row, causal)
        sv, s_sum = _dot(s.astype(BF16), vb), jnp.sum(s, axis=-1, keepdims=True)
        c_old = [c_ref[0, i, h] for i in range(G)]
        n_old = [n_ref[0, i, h:h + 1, :] for i in range(G)]
        q_c = _expand(seq_col, [_dot(qb, c.astype(BF16)) for c in c_old])
        num = inter * q_c + sv
        qn = inter * jnp.sum(q * _expand(seq_col, n_old), axis=-1, keepdims=True) + s_sum
        ht = num * (1.0 / jnp.maximum(jnp.abs(qn), jnp.exp(-m_t)))
        last = [(i + 1) * seq_len - 1 for i in range(G)]
        m_new = [m_t[r:r + 1, :] for r in last]
        b_last = [b_col[r:r + 1, :] for r in last]
        w_end = jnp.exp(_expand(seq_col, b_last) - b_col + ig_col - _expand(seq_col, m_new)) * K_SCALE
        kw = k * w_end
        kwb = kw.astype(BF16)
        for i in range(G):
            decay = jnp.exp(b_last[i] + m_prev[i] - m_new[i])
            v_i = jnp.where(seq_col == i, vb, jnp.zeros_like(vb))
            c_out[0, i, h] = decay * c_old[i] + _dot_tn(kwb, v_i)
            n_out[0, i, h:h + 1, :] = decay * n_old[i] + jnp.sum(
                kw[i * seq_len:(i + 1) * seq_len, :], axis=0, keepdims=True)
            m_rows[i] = jnp.where(lane4 == h, m_new[i], m_rows[i])
        hn = _head_norm(ht, w.mhln[:, hc])
        o = proj_ref[:, OFF_O + h * 256:OFF_O + (h + 1) * 256]
        zb = proj_ref[:, OFF_ZB + h * 256:OFF_ZB + (h + 1) * 256]
        ycat_scr[:, 1024 + h * 256:1024 + (h + 1) * 256] = (hn * jax.nn.sigmoid(o) * _silu(zb)).astype(BF16)
    for i in range(G):
        m_out[0, i:i + 1, :] = m_rows[i]

    y = _dot(ycat_scr[...], w.out[...])
    y_ref[...] = _rms(x + y, w.normf[...])


def _const_spec(shape):
    nd = len(shape)
    return pl.BlockSpec(shape, lambda *_: (0,) * nd, pipeline_mode=pl.Buffered(1))


def _params(sem):
    return pltpu.CompilerParams(dimension_semantics=sem, vmem_limit_bytes=VMEM_LIMIT)


def _tile_scratch(T):
    return [pltpu.VMEM((T, D_MAIN), F32), pltpu.VMEM((T, D_MODEL), F32), pltpu.VMEM((T, GATE_PAD), F32),
            pltpu.VMEM((T, 2048), BF16), pltpu.VMEM((T, D_MODEL), BF16)]


def _state_scratch():
    return [pltpu.VMEM((HIST, D_POOL), F32), pltpu.VMEM((N_HEADS, HEAD_DIM, HEAD_DIM), F32),
            pltpu.VMEM((8, HEAD_DIM), F32), pltpu.VMEM((8, 128), F32)]


def kernel(x_prompt, x_sample, state_pool, state_C, state_n, state_m, meta_tokens, norm1_w, w_in,
           b_if, w_pool, pool_scale, mhln_w, w_out, normf_w):
    B, S, _ = x_prompt.shape
    SB, SL, _ = x_sample.shape
    TT = PROMPT_TILE
    assert norm1_w.shape[0] == 1, "single layer"
    assert S % (2 * TT) == 0 and SB % SAMPLE_GROUP == 0

    w_in_t = jnp.swapaxes(w_in[0], 0, 1)
    norm1 = norm1_w[0].reshape(1, D_MODEL)
    n_tok = SB * SL
    xs = x_sample.reshape(n_tok, D_MODEL)
    n_seg = D_MAIN // 1024
    wout_rows = w_out.shape[1] // (n_seg + 1)
    seg = lambda i: jnp.minimum(i, n_seg - 1)
    w_main, w_gate, proj_s, wout = pl.pallas_call(
        functools.partial(_prep_kernel, n_seg=n_seg),
        grid=(n_seg + 1,),
        in_specs=[pl.BlockSpec((1024, D_MODEL), lambda i: (seg(i), 0)),
                  pl.BlockSpec((8, D_MODEL), lambda i: (D_MAIN // 8, 0)),
                  _const_spec(xs.shape), _const_spec(norm1.shape),
                  pl.BlockSpec((wout_rows, D_MODEL), lambda i: (i, 0))],
        out_specs=(pl.BlockSpec((D_MODEL, 1024), lambda i: (0, seg(i))),
                   pl.BlockSpec((D_MODEL, GATE_PAD), lambda i: (0, 0)),
                   pl.BlockSpec((n_tok, 1024), lambda i: (0, seg(i))),
                   pl.BlockSpec((wout_rows, D_MODEL), lambda i: (i, 0))),
        out_shape=(jax.ShapeDtypeStruct((D_MODEL, D_MAIN), BF16),
                   jax.ShapeDtypeStruct((D_MODEL, GATE_PAD), BF16),
                   jax.ShapeDtypeStruct((n_tok, D_MAIN), F32),
                   jax.ShapeDtypeStruct(w_out.shape[1:], BF16)),
        scratch_shapes=[pltpu.VMEM((n_tok, D_MODEL), BF16)],
        compiler_params=_params(("arbitrary",)),
        name="prep",
    )(w_in_t, w_in_t, xs, norm1, w_out[0])
    bias_row = jnp.pad(b_if[0], (0, GATE_PAD - 8)).reshape(1, GATE_PAD)
    wpool = w_pool[0].astype(BF16)
    pscale = pool_scale[0].reshape(1, D_POOL)
    mhln = mhln_w[0].reshape(1, D_MLSTM)
    normf = normf_w.reshape(1, D_MODEL)
    layer_w = LayerW(w_main, w_gate, bias_row, norm1, wpool, pscale, mhln, wout, normf)
    layer_specs = [_const_spec(a.shape) for a in layer_w]

    n_tiles = B * S // TT
    steps = n_tiles // 2
    steps_per_seq = S // (2 * TT)
    xp = x_prompt.reshape(B * S, D_MODEL)
    y_p, pool_p, c_p, n_p, m_p = pl.pallas_call(
        functools.partial(_prompt_kernel, steps_per_seq=steps_per_seq),
        grid=(steps,),
        in_specs=[_const_spec(meta_tokens.shape),
                  pl.BlockSpec((TT, D_MODEL), lambda s: (0, 0)),
                  pl.BlockSpec((TT, D_MODEL), lambda s: (2 * s + 1, 0)),
                  pl.BlockSpec((TT, D_MODEL), lambda s: (jnp.minimum(2 * s + 2, n_tiles - 1), 0))]
                 + layer_specs,
        out_specs=(pl.BlockSpec((2 * TT, D_MODEL), lambda s: (s, 0)),
                   pl.BlockSpec((1, POOL_BUF, B, D_POOL), lambda s: (0, 0, 0, 0)),
                   pl.BlockSpec((1, 1, N_HEADS, HEAD_DIM, HEAD_DIM),
                                lambda s: (0, s // steps_per_seq, 0, 0, 0)),
                   pl.BlockSpec((1, 1, N_HEADS, HEAD_DIM), lambda s: (0, s // steps_per_seq, 0, 0)),
                   pl.BlockSpec((B, 128), lambda s: (0, 0))),
        out_shape=(jax.ShapeDtypeStruct((B * S, D_MODEL), F32),
                   jax.ShapeDtypeStruct((1, POOL_BUF, B, D_POOL), F32),
                   jax.ShapeDtypeStruct((1, B, N_HEADS, HEAD_DIM, HEAD_DIM), F32),
                   jax.ShapeDtypeStruct((1, B, N_HEADS, HEAD_DIM), F32),
                   jax.ShapeDtypeStruct((B, 128), F32)),
        scratch_shapes=_tile_scratch(TT) + _tile_scratch(TT) + _tile_scratch(N_META)
        + [pltpu.VMEM((N_META, D_MODEL), F32)] + _state_scratch() + _state_scratch(),
        compiler_params=_params(("arbitrary",)),
        name="prompt",
    )(meta_tokens, xp, xp, xp, *layer_w)
    y_prompt = y_p.reshape(B, S, D_MODEL)
    pool_p = jnp.swapaxes(pool_p, 1, 2)
    m_p = m_p[:, :N_HEADS].reshape(1, B, N_HEADS)

    G = SAMPLE_GROUP
    GT = G * SL
    pool_in = jnp.swapaxes(state_pool, 1, 2)
    sample_w = layer_w[1:]
    y_s, pool_s, c_s, n_s, m_s = pl.pallas_call(
        functools.partial(_sample_kernel, seq_len=SL, pos0=PAST_LEN),
        grid=(SB // G,),
        in_specs=[pl.BlockSpec((GT, D_MODEL), lambda i: (i, 0)),
                  pl.BlockSpec((GT, D_MAIN), lambda i: (i, 0)),
                  pl.BlockSpec((1, POOL_BUF, G, D_POOL), lambda i: (0, 0, i, 0)),
                  pl.BlockSpec((1, G, N_HEADS, HEAD_DIM, HEAD_DIM), lambda i: (0, i, 0, 0, 0)),
                  pl.BlockSpec((1, G, N_HEADS, HEAD_DIM), lambda i: (0, i, 0, 0)),
                  pl.BlockSpec((1, G, N_HEADS), lambda i: (0, i, 0))]
                 + [_const_spec(a.shape) for a in sample_w],
        out_specs=(pl.BlockSpec((GT, D_MODEL), lambda i: (i, 0)),
                   pl.BlockSpec((1, POOL_BUF, G, D_POOL), lambda i: (0, 0, i, 0)),
                   pl.BlockSpec((1, G, N_HEADS, HEAD_DIM, HEAD_DIM), lambda i: (0, i, 0, 0, 0)),
                   pl.BlockSpec((1, G, N_HEADS, HEAD_DIM), lambda i: (0, i, 0, 0)),
                   pl.BlockSpec((1, G, N_HEADS), lambda i: (0, i, 0))),
        out_shape=(jax.ShapeDtypeStruct((n_tok, D_MODEL), F32),
                   jax.ShapeDtypeStruct(pool_in.shape, F32),
                   jax.ShapeDtypeStruct(state_C.shape, F32),
                   jax.ShapeDtypeStruct(state_n.shape, F32),
                   jax.ShapeDtypeStruct(state_m.shape, F32)),
        scratch_shapes=[pltpu.VMEM((HIST + SL, D_POOL), F32), pltpu.VMEM((GT, 2048), BF16)],
        compiler_params=_params(("arbitrary",)),
        name="sample",
    )(xs, proj_s, pool_in, state_C, state_n, state_m, *sample_w)
    y_sample = y_s.reshape(SB, SL, D_MODEL)
    pool_s = jnp.swapaxes(pool_s, 1, 2)

    return (y_prompt, y_sample, pool_p, c_p, n_p, m_p, pool_s, c_s, n_s, m_s)
```

```python
import collections
import functools

import jax
import jax.numpy as jnp
from jax import lax
from jax.experimental import pallas as pl
from jax.experimental.pallas import tpu as pltpu

D_MODEL = 1024
D_POOL = 1024
D_MLSTM = 1024
N_HEADS = 4
HEAD_DIM = 256
POOL_WINDOWS = (2, 4, 8, 16)
POOL_BUF = 15
HIST = 16
N_META = 16
PAST_LEN = 16384
EPS = 1e-6
D_MAIN = 2 * D_POOL + 5 * D_MLSTM
GATE_PAD = 128
K_SCALE = HEAD_DIM ** -0.5

OFF_U, OFF_ZA, OFF_Q, OFF_K, OFF_V, OFF_O, OFF_ZB = (i * 1024 for i in range(7))

PROMPT_TILE = 256
HALF_SEQS = 2
POOL_SEQS = 8
VMEM_LIMIT = (255 * 1024 * 1024) // 4

F32 = jnp.float32
BF16 = jnp.bfloat16

LayerW = collections.namedtuple("LayerW", "main gate bias norm1 pool pscale mhln out normf")


def _dot(a, b):
    return jnp.dot(a, b, preferred_element_type=F32)


def _dot_nt(a, b):
    return lax.dot_general(a, b, (((1,), (1,)), ((), ())), preferred_element_type=F32)


def _dot_tn(a, b):
    return lax.dot_general(a, b, (((0,), (0,)), ((), ())), preferred_element_type=F32)


def _rms(x, w):
    return x * lax.rsqrt(jnp.mean(x * x, axis=-1, keepdims=True) + EPS) * w


def _log_sigmoid(x):
    return jnp.minimum(x, 0.0) - jnp.log1p(jnp.exp(-jnp.abs(x)))


def _silu(x):
    return x * jax.nn.sigmoid(x)


def _split3(x):
    hi = x.astype(BF16)
    r = x - hi.astype(F32)
    mid = r.astype(BF16)
    lo = (r - mid.astype(F32)).astype(BF16)
    return hi, mid, lo


def _seq_mask(T, seq_len):
    row = lax.broadcasted_iota(jnp.int32, (T, T), 0)
    col = lax.broadcasted_iota(jnp.int32, (T, T), 1)
    causal = col <= row
    if seq_len < T:
        shift = seq_len.bit_length() - 1
        assert 1 << shift == seq_len
        causal = causal & ((row >> shift) == (col >> shift))
    return causal


def _to_rows(cols):
    T = cols.shape[0]
    pad = -T % 128
    if pad:
        cols = jnp.concatenate([cols, jnp.zeros((pad, cols.shape[1]), cols.dtype)], axis=0)
    return cols.T[:, 0:T]


def _gate_pre(xn, wgate, bias):
    return _dot(xn, wgate) + bias


def _cumsum_operands(g_col, causal):
    tri = jnp.where(causal, 1.0, 0.0).astype(BF16)
    return tri, _split3(_log_sigmoid(g_col))


def _cumsum(tri, parts):
    return sum(_dot(tri, p) for p in parts)


def _gate_rows(g_col, b_col_all):
    return _to_rows(g_col - pltpu.roll(b_col_all, GATE_PAD - N_HEADS, axis=1))


def _intra_weights(qb, kb, b_col, a_col, r_row, causal):
    dm = jnp.where(causal, b_col + r_row, -jnp.inf)
    m_t = jnp.maximum(a_col, jnp.max(dm, axis=-1, keepdims=True))
    w = jnp.exp(dm - m_t)
    inter = jnp.exp(a_col - m_t)
    return m_t, inter, _dot_nt(qb, kb) * (w * K_SCALE)


def _head_norm(ht, w_row):
    mu = jnp.mean(ht, axis=-1, keepdims=True)
    d = ht - mu
    var = jnp.mean(d * d, axis=-1, keepdims=True)
    return d * lax.rsqrt(var + EPS) * w_row


def _window_sums(ext):
    s2 = ext + pltpu.roll(ext, 1, axis=0)
    s4 = s2[:, 256:] + pltpu.roll(s2[:, 256:], 2, axis=0)
    s8 = s4[:, 256:] + pltpu.roll(s4[:, 256:], 4, axis=0)
    s16 = s8[:, 256:] + pltpu.roll(s8[:, 256:], 8, axis=0)
    return [s2[HIST:, 0:256], s4[HIST:, 0:256], s8[HIST:, 0:256], s16[HIST:, 0:256]]


def _pooled(ext, u, pos_col):
    sums = _window_sums(ext)
    return [sums[g] * (1.0 / jnp.minimum(float(w), pos_col + 1.0)) - u[:, g * 256:(g + 1) * 256]
            for g, w in enumerate(POOL_WINDOWS)]


def _pool_mix(pooled, wpool_ref):
    return jnp.concatenate([_dot(p.astype(BF16), wpool_ref[g]) for g, p in enumerate(pooled)], axis=-1)


def _inproj_steps(x_ref, scr, w):
    proj_scr, x_scr, gcol_scr, _, xn_scr = scr

    def norm():
        x = x_ref[...]
        x_scr[...] = x
        xn_scr[...] = _rms(x, w.norm1[...]).astype(BF16)

    def piece(c0):
        def run():
            proj_scr[:, c0:c0 + PIECE_COLS] = _dot(xn_scr[...], w.main[:, c0:c0 + PIECE_COLS])
        return run

    def gates():
        gcol_scr[...] = _gate_pre(xn_scr[...], w.gate[...], w.bias[...])

    return [norm] + [piece(c0) for c0 in range(0, D_MAIN, PIECE_COLS)] + [gates]


def _post_steps(scr, pos0, w, state, y_ref):
    proj_scr, x_scr, gcol_scr, ycat_scr, _ = scr
    hist_scr, c_scr, n_scr, m_scr = state
    T = x_scr.shape[0]

    def seg(off, h=None):
        if h is None:
            return proj_scr[:, off:off + 1024]
        return proj_scr[:, off + h * 256:off + (h + 1) * 256]

    causal = _seq_mask(T, T)
    g_col = gcol_scr[...]
    tri, lf_parts = _cumsum_operands(g_col, causal)
    yield
    b_col_all = _cumsum(tri, lf_parts)
    r_row_all = _gate_rows(g_col, b_col_all)

    u = seg(OFF_U)
    ext = jnp.concatenate([hist_scr[...], u], axis=0)
    pos_col = (lax.broadcasted_iota(jnp.int32, (T, 1), 0) + pos0).astype(F32)
    pooled = _pooled(ext, u, pos_col)
    hist_scr[...] = ext[T:T + HIST, :]
    yield
    mixed = _pool_mix(pooled, w.pool)
    ycat_scr[:, 0:1024] = (mixed * w.pscale[...] * _silu(seg(OFF_ZA))).astype(BF16)

    for h in range(N_HEADS):
        q, k, v = seg(OFF_Q, h), seg(OFF_K, h), seg(OFF_V, h)
        qb, kb, vb = q.astype(BF16), k.astype(BF16), v.astype(BF16)
        ig_col = g_col[:, h:h + 1]
        b_col = b_col_all[:, 4 + h:5 + h]
        r_row = r_row_all[h:h + 1, :]
        m_prev = m_scr[h:h + 1, 0:1]
        a_col = b_col + m_prev
        c_old = c_scr[h]
        n_old = n_scr[h:h + 1, :]
        q_c = _dot(qb, c_old.astype(BF16))
        m_t, inter, s = _intra_weights(qb, kb, b_col, a_col, r_row, causal)
        sb = s.astype(BF16)
        yield
        num = inter * q_c + _dot(sb, vb)
        qn = inter * jnp.sum(q * n_old, axis=-1, keepdims=True) + jnp.sum(s, axis=-1, keepdims=True)
        ht = num * (1.0 / jnp.maximum(jnp.abs(qn), jnp.exp(-m_t)))
        m_new = m_t[T - 1:T, :]
        b_last = b_col[T - 1:T, :]
        w_end = jnp.exp(b_last - b_col + ig_col - m_new) * K_SCALE
        decay = jnp.exp(b_last + m_prev - m_new)
        kw = k * w_end
        kwb = kw.astype(BF16)
        yield
        c_scr[h] = decay * c_old + _dot_tn(kwb, vb)
        n_scr[h:h + 1, :] = decay * n_old + jnp.sum(kw, axis=0, keepdims=True)
        m_scr[h:h + 1, :] = jnp.broadcast_to(m_new, (1, 128))
        hn = _head_norm(ht, w.mhln[:, h * 256:(h + 1) * 256])
        ycat_scr[:, 1024 + h * 256:1024 + (h + 1) * 256] = (
            hn * jax.nn.sigmoid(seg(OFF_O, h)) * _silu(seg(OFF_ZB, h))).astype(BF16)

    yield
    y = _dot(ycat_scr[...], w.out[...])
    y_ref[...] = _rms(x_scr[...] + y, w.normf[...])


PIECE_COLS = 512
POST_YIELDS = 3 + 2 * N_HEADS
SAMPLE_STAGES = 2 + 3 * N_HEADS
SAMPLE_PLAN = (1, 1, 1, 1, 2, 2, 2, 2, 2, 0, 0)
WRITEBACK_AT = 8


def _run_interleaved(post, pieces, extras=None):
    assert len(pieces) >= POST_YIELDS + 2
    extras = extras or {}
    rest = iter(pieces)
    next(rest)()
    next(rest)()
    for i in range(POST_YIELDS):
        next(post)
        next(rest)()
        if i == POST_YIELDS - 1:
            for piece in rest:
                piece()
        for extra in extras.get(i, ()):
            extra()
    for _ in post:
        raise AssertionError("unexpected extra yield")


def _m_row(m_scr):
    lane = lax.broadcasted_iota(jnp.int32, (1, 128), 1)
    row = jnp.zeros((1, 128), F32)
    for h in range(N_HEADS):
        row = jnp.where(lane == h, m_scr[h:h + 1, :], row)
    return row


def _expand(seq_col, vals):
    out = None
    for i, val in enumerate(vals):
        pick = jnp.where(seq_col == i, val, 0.0)
        out = pick if out is None else out + pick
    return out


def _sample_group(w, seq_len, pos0, proj_ref, gcol_ref, pool_in, pool_out, pool_base, c_buf, n_in, n_out,
                  m_in, m_out, ycat_ref, ext_scr):
    G = c_buf.shape[0]
    T = G * seq_len
    causal = _seq_mask(T, seq_len)
    g_col = gcol_ref[...]
    seq_col = lax.broadcasted_iota(jnp.int32, (T, 1), 0) >> (seq_len.bit_length() - 1)
    pos_col = jnp.full((seq_len, 1), float(pos0), F32) + lax.broadcasted_iota(
        jnp.int32, (seq_len, 1), 0).astype(F32)

    tri, lf_parts = _cumsum_operands(g_col, causal)

    pooled_rows = []
    for i in range(G):
        u_i = proj_ref[i * seq_len:(i + 1) * seq_len, OFF_U:OFF_U + 1024]
        ext_scr[0:1, :] = jnp.zeros((1, 1024), F32)
        for j in range(POOL_BUF):
            ext_scr[1 + j:2 + j, :] = pool_in[0, j, pl.ds(pool_base + i, 1), :]
        ext_scr[HIST:HIST + seq_len, :] = u_i
        ext = ext_scr[...]
        pooled_rows.append(_pooled(ext, u_i, pos_col))
        for j in range(POOL_BUF):
            pool_out[0, j, pl.ds(pool_base + i, 1), :] = ext[seq_len + 1 + j:seq_len + 2 + j, :]
    pooled = [jnp.concatenate(p, axis=0) for p in zip(*pooled_rows)]

    def operands(h):
        q = proj_ref[:, OFF_Q + h * 256:OFF_Q + (h + 1) * 256]
        k = proj_ref[:, OFF_K + h * 256:OFF_K + (h + 1) * 256]
        v = proj_ref[:, OFF_V + h * 256:OFF_V + (h + 1) * 256]
        c_old = [c_buf[i, h] for i in range(G)]
        return q, k, q.astype(BF16), k.astype(BF16), v.astype(BF16), c_old, [c.astype(BF16) for c in c_old]

    yield
    b_col_all = _cumsum(tri, lf_parts)
    mixed = _pool_mix(pooled, w.pool)
    r_row_all = _gate_rows(g_col, b_col_all)
    y_a = mixed * w.pscale[...] * _silu(proj_ref[:, OFF_ZA:OFF_ZA + 1024])
    ycat_ref[:, 0:1024] = y_a.astype(BF16)
    ops = operands(0)
    yield

    lane4 = lax.broadcasted_iota(jnp.int32, (1, N_HEADS), 1)
    m_rows = [jnp.zeros((1, N_HEADS), F32) for _ in range(G)]
    last = [(i + 1) * seq_len - 1 for i in range(G)]
    for h in range(N_HEADS):
        q, k, qb, kb, vb, c_old, c_old_b = ops
        qk = _dot_nt(qb, kb)
        q_cs = [_dot(qb, cb) for cb in c_old_b]
        ig_col = g_col[:, h:h + 1]
        b_col = b_col_all[:, 4 + h:5 + h]
        r_row = r_row_all[h:h + 1, :]
        m_prev = [m_in[i:i + 1, h:h + 1] for i in range(G)]
        a_col = b_col + _expand(seq_col, m_prev)
        dm = jnp.where(causal, b_col + r_row, -jnp.inf)
        m_t = jnp.maximum(a_col, jnp.max(dm, axis=-1, keepdims=True))
        inter = jnp.exp(a_col - m_t)
        s = qk * (jnp.exp(dm - m_t) * K_SCALE)
        sb = s.astype(BF16)
        yield
        sv = _dot(sb, vb)
        n_old = [n_in[i, h:h + 1, :] for i in range(G)]
        num = inter * _expand(seq_col, q_cs) + sv
        qn = inter * jnp.sum(q * _expand(seq_col, n_old), axis=-1, keepdims=True) + jnp.sum(
            s, axis=-1, keepdims=True)
        ht = num * (1.0 / jnp.maximum(jnp.abs(qn), jnp.exp(-m_t)))
        m_new = [m_t[r:r + 1, :] for r in last]
        b_last = [b_col[r:r + 1, :] for r in last]
        w_end = jnp.exp(_expand(seq_col, b_last) - b_col + ig_col - _expand(seq_col, m_new)) * K_SCALE
        kw = k * w_end
        kwb = kw.astype(BF16)
        v_seq = [jnp.where(seq_col == i, vb, jnp.zeros_like(vb)) for i in range(G)]
        yield
        updates = [_dot_tn(kwb, v_i) for v_i in v_seq]
        for i in range(G):
            decay = jnp.exp(b_last[i] + m_prev[i] - m_new[i])
            c_buf[i, h] = decay * c_old[i] + updates[i]
            n_out[i, h:h + 1, :] = decay * n_old[i] + jnp.sum(
                kw[i * seq_len:(i + 1) * seq_len, :], axis=0, keepdims=True)
            m_rows[i] = jnp.where(lane4 == h, m_new[i], m_rows[i])
        hn = _head_norm(ht, w.mhln[:, h * 256:(h + 1) * 256])
        o = proj_ref[:, OFF_O + h * 256:OFF_O + (h + 1) * 256]
        zb = proj_ref[:, OFF_ZB + h * 256:OFF_ZB + (h + 1) * 256]
        ycat_ref[:, 1024 + h * 256:1024 + (h + 1) * 256] = (hn * jax.nn.sigmoid(o) * _silu(zb)).astype(BF16)
        if h == N_HEADS - 1:
            for i in range(G):
                m_out[i:i + 1, :] = m_rows[i]
        else:
            ops = operands(h + 1)
        yield


def _prep_kernel(wt_ref, wgt_ref, xs_ref, norm1_ref, bias_ref, wout_ref,
                 wmain_ref, wgate_ref, proj_ref, gcol_ref, woutb_ref, xn_scr, *, n_seg):
    i = pl.program_id(0)

    @pl.when(i == 0)
    def _():
        g = jnp.concatenate([wgt_ref[...], jnp.zeros((GATE_PAD - 8, D_MODEL), F32)], axis=0)
        wgate = g.T.astype(BF16)
        wgate_ref[...] = wgate
        xn = _rms(xs_ref[...], norm1_ref[...]).astype(BF16)
        xn_scr[...] = xn
        gcol_ref[...] = _gate_pre(xn, wgate, bias_ref[...])

    woutb_ref[...] = wout_ref[...].astype(BF16)

    @pl.when(i < n_seg)
    def _():
        wb = wt_ref[...].T.astype(BF16)
        wmain_ref[...] = wb
        proj_ref[...] = _dot(xn_scr[...], wb)


def _main_kernel(meta_ref, x_hbm, xb_ref, xc_ref, sproj_ref, sgcol_ref, spool_ref, sn_ref, sm_ref, sc_hbm,
                 *refs, steps_per_seq, seq_len):
    w = LayerW(*refs[:9])
    y_ref, pool_out, c_out, n_out, m_out, sycat_ref, spool_out, sn_out, sm_out, sc_out_hbm = refs[9:19]
    scr0, scr1, scr_meta = refs[19:24], refs[24:29], refs[29:34]
    y_meta = refs[34]
    state, state_meta = refs[35:39], refs[39:43]
    c_buf, ext_scr, sem_in, sem_out, sem_x = refs[43:48]
    hist_scr, c_scr, n_scr, m_scr = state
    s = pl.program_id(0)
    last = pl.num_programs(0) - 1
    TT = PROMPT_TILE
    G = HALF_SEQS
    GT = G * seq_len
    n_seqs = sc_hbm.shape[1]

    def c_in_copy(step, half):
        first = jnp.minimum(2 * G * step + G * half, n_seqs - G)
        return pltpu.make_async_copy(sc_hbm.at[0, pl.ds(first, G)], c_buf.at[half], sem_in.at[half])

    def c_out_copy(step, half):
        first = 2 * G * step + G * half
        return pltpu.make_async_copy(c_buf.at[half], sc_out_hbm.at[0, pl.ds(first, G)], sem_out.at[half])

    @pl.when(s == 0)
    def _():
        c_in_copy(0, 0).start()
        c_buf[1] = jnp.zeros(c_buf.shape[1:], F32)
        c_out_copy(0, 1).start()
        for ref in state_meta:
            ref[...] = jnp.zeros_like(ref)
        for p in _inproj_steps(meta_ref, scr_meta, w):
            p()
        for _ in _post_steps(scr_meta, 0, w, state_meta, y_meta):
            pass
        x_scr0 = scr0[1]
        first_tile = pltpu.make_async_copy(x_hbm.at[pl.ds(0, TT)], x_scr0, sem_x.at[0])
        first_tile.start()
        first_tile.wait()
        for p in _inproj_steps(x_scr0, scr0, w):
            p()

    @pl.when(s % steps_per_seq == 0)
    def _():
        for ref, ref_meta in zip(state, state_meta):
            ref[...] = ref_meta[...]

    pool_base = (s % (POOL_SEQS // (2 * G))) * (2 * G)

    def half_step(half, post, pieces):
        rows = slice(half * GT, (half + 1) * GT)
        seqs = slice(half * G, (half + 1) * G)
        group = _sample_group(
            w, seq_len, PAST_LEN, sproj_ref.at[rows, :], sgcol_ref.at[rows, :], spool_ref, spool_out,
            pool_base + half * G, c_buf.at[half], sn_ref.at[0, seqs], sn_out.at[0, seqs],
            sm_ref.at[0, seqs], sm_out.at[0, seqs], sycat_ref.at[rows, :], ext_scr)

        assert len(SAMPLE_PLAN) == POST_YIELDS and sum(SAMPLE_PLAN) == SAMPLE_STAGES
        extras = {at: [lambda: next(group)] * count for at, count in enumerate(SAMPLE_PLAN)}
        extras[WRITEBACK_AT] = extras[WRITEBACK_AT] + [lambda: c_out_copy(s, half).start()]
        c_in_copy(s, half).wait()
        c_out_copy(s, 1 - half).wait()
        c_in_copy(s + half, 1 - half).start()
        _run_interleaved(post, pieces, extras)

    half_step(0, _post_steps(scr0, N_META, w, state, y_ref.at[0:TT, :]), _inproj_steps(xb_ref, scr1, w))
    half_step(1, _post_steps(scr1, N_META, w, state, y_ref.at[TT:2 * TT, :]), _inproj_steps(xc_ref, scr0, w))

    @pl.when(s % steps_per_seq == steps_per_seq - 1)
    def _():
        b = s // steps_per_seq
        for j in range(POOL_BUF):
            pool_out[0, j, pl.ds(b, 1), :] = hist_scr[1 + j:2 + j, :]
        c_out[0, 0] = c_scr[...]
        n_out[0, 0] = n_scr[0:N_HEADS, :]
        m_out[pl.ds(b, 1), :] = _m_row(m_scr)

    @pl.when(s == last)
    def _():
        c_in_copy(s + 1, 0).wait()
        c_out_copy(s, 1).wait()


def _sout_kernel(x_ref, ycat_ref, wout_ref, normf_ref, y_ref):
    y_ref[...] = _rms(x_ref[...] + _dot(ycat_ref[...], wout_ref[...]), normf_ref[...])


def _const_spec(shape):
    nd = len(shape)
    return pl.BlockSpec(shape, lambda *_: (0,) * nd, pipeline_mode=pl.Buffered(1))


def _params(sem):
    return pltpu.CompilerParams(dimension_semantics=sem, vmem_limit_bytes=VMEM_LIMIT)


def _tile_scratch(T):
    return [pltpu.VMEM((T, D_MAIN), F32), pltpu.VMEM((T, D_MODEL), F32), pltpu.VMEM((T, GATE_PAD), F32),
            pltpu.VMEM((T, 2048), BF16), pltpu.VMEM((T, D_MODEL), BF16)]


def _state_scratch():
    return [pltpu.VMEM((HIST, D_POOL), F32), pltpu.VMEM((N_HEADS, HEAD_DIM, HEAD_DIM), F32),
            pltpu.VMEM((8, HEAD_DIM), F32), pltpu.VMEM((8, 128), F32)]


def kernel(x_prompt, x_sample, state_pool, state_C, state_n, state_m, meta_tokens, norm1_w, w_in,
           b_if, w_pool, pool_scale, mhln_w, w_out, normf_w):
    B, S, _ = x_prompt.shape
    SB, SL, _ = x_sample.shape
    TT = PROMPT_TILE
    n_tiles = B * S // TT
    steps = n_tiles // 2
    steps_per_seq = S // (2 * TT)
    step_seqs = 2 * HALF_SEQS
    assert norm1_w.shape[0] == 1, "single layer"
    assert S % (2 * TT) == 0 and SB == steps * step_seqs and POOL_SEQS % step_seqs == 0

    w_in_t = jnp.swapaxes(w_in[0], 0, 1)
    norm1 = norm1_w[0].reshape(1, D_MODEL)
    bias_row = jnp.pad(b_if[0], (0, GATE_PAD - 8)).reshape(1, GATE_PAD)
    n_tok = SB * SL
    xs = x_sample.reshape(n_tok, D_MODEL)
    n_seg = D_MAIN // 1024
    wout_rows = w_out.shape[1] // (n_seg + 1)
    seg = lambda i: jnp.minimum(i, n_seg - 1)
    w_main, w_gate, proj_s, gcol_s, wout = pl.pallas_call(
        functools.partial(_prep_kernel, n_seg=n_seg),
        grid=(n_seg + 1,),
        in_specs=[pl.BlockSpec((1024, D_MODEL), lambda i: (seg(i), 0)),
                  pl.BlockSpec((8, D_MODEL), lambda i: (D_MAIN // 8, 0)),
                  _const_spec(xs.shape), _const_spec(norm1.shape), _const_spec(bias_row.shape),
                  pl.BlockSpec((wout_rows, D_MODEL), lambda i: (i, 0))],
        out_specs=(pl.BlockSpec((D_MODEL, 1024), lambda i: (0, seg(i))),
                   pl.BlockSpec((D_MODEL, GATE_PAD), lambda i: (0, 0)),
                   pl.BlockSpec((n_tok, 1024), lambda i: (0, seg(i))),
                   pl.BlockSpec((n_tok, GATE_PAD), lambda i: (0, 0)),
                   pl.BlockSpec((wout_rows, D_MODEL), lambda i: (i, 0))),
        out_shape=(jax.ShapeDtypeStruct((D_MODEL, D_MAIN), BF16),
                   jax.ShapeDtypeStruct((D_MODEL, GATE_PAD), BF16),
                   jax.ShapeDtypeStruct((n_tok, D_MAIN), F32),
                   jax.ShapeDtypeStruct((n_tok, GATE_PAD), F32),
                   jax.ShapeDtypeStruct(w_out.shape[1:], BF16)),
        scratch_shapes=[pltpu.VMEM((n_tok, D_MODEL), BF16)],
        compiler_params=_params(("arbitrary",)),
        name="prep",
    )(w_in_t, w_in_t, xs, norm1, bias_row, w_out[0])
    wpool = w_pool[0].astype(BF16)
    pscale = pool_scale[0].reshape(1, D_POOL)
    mhln = mhln_w[0].reshape(1, D_MLSTM)
    normf = normf_w.reshape(1, D_MODEL)
    layer_w = LayerW(w_main, w_gate, bias_row, norm1, wpool, pscale, mhln, wout, normf)
    layer_specs = [_const_spec(a.shape) for a in layer_w]

    xp = x_prompt.reshape(B * S, D_MODEL)
    pool_in = jnp.swapaxes(state_pool, 1, 2)
    sm_in = state_m.reshape(steps, step_seqs, N_HEADS)
    step_rows = step_seqs * SL
    pool_block = lambda s: (0, 0, s // (POOL_SEQS // step_seqs), 0)
    seq_block = lambda s: (0, s, 0, 0)
    any_spec = pl.BlockSpec(memory_space=pl.ANY)
    (y_p, pool_p, c_p, n_p, m_p, ycat_s, pool_s, n_s, m_s, c_s) = pl.pallas_call(
        functools.partial(_main_kernel, steps_per_seq=steps_per_seq, seq_len=SL),
        grid=(steps,),
        in_specs=[_const_spec(meta_tokens.shape),
                  any_spec,
                  pl.BlockSpec((TT, D_MODEL), lambda s: (2 * s + 1, 0)),
                  pl.BlockSpec((TT, D_MODEL), lambda s: (jnp.minimum(2 * s + 2, n_tiles - 1), 0)),
                  pl.BlockSpec((step_rows, D_MAIN), lambda s: (s, 0)),
                  pl.BlockSpec((step_rows, GATE_PAD), lambda s: (s, 0)),
                  pl.BlockSpec((1, POOL_BUF, POOL_SEQS, D_POOL), pool_block, pipeline_mode=pl.Buffered(1)),
                  pl.BlockSpec((1, step_seqs, N_HEADS, HEAD_DIM), seq_block),
                  pl.BlockSpec((1, step_seqs, N_HEADS), lambda s: (s, 0, 0)),
                  any_spec]
                 + layer_specs,
        out_specs=(pl.BlockSpec((2 * TT, D_MODEL), lambda s: (s, 0)),
                   pl.BlockSpec((1, POOL_BUF, B, D_POOL), lambda s: (0, 0, 0, 0)),
                   pl.BlockSpec((1, 1, N_HEADS, HEAD_DIM, HEAD_DIM),
                                lambda s: (0, s // steps_per_seq, 0, 0, 0), pipeline_mode=pl.Buffered(1)),
                   pl.BlockSpec((1, 1, N_HEADS, HEAD_DIM), lambda s: (0, s // steps_per_seq, 0, 0)),
                   pl.BlockSpec((B, 128), lambda s: (0, 0)),
                   pl.BlockSpec((step_rows, 2048), lambda s: (s, 0)),
                   pl.BlockSpec((1, POOL_BUF, POOL_SEQS, D_POOL), pool_block, pipeline_mode=pl.Buffered(1)),
                   pl.BlockSpec((1, step_seqs, N_HEADS, HEAD_DIM), seq_block),
                   pl.BlockSpec((1, step_seqs, N_HEADS), lambda s: (s, 0, 0)),
                   any_spec),
        out_shape=(jax.ShapeDtypeStruct((B * S, D_MODEL), F32),
                   jax.ShapeDtypeStruct((1, POOL_BUF, B, D_POOL), F32),
                   jax.ShapeDtypeStruct((1, B, N_HEADS, HEAD_DIM, HEAD_DIM), F32),
                   jax.ShapeDtypeStruct((1, B, N_HEADS, HEAD_DIM), F32),
                   jax.ShapeDtypeStruct((B, 128), F32),
                   jax.ShapeDtypeStruct((n_tok, 2048), BF16),
                   jax.ShapeDtypeStruct(pool_in.shape, F32),
                   jax.ShapeDtypeStruct(state_n.shape, F32),
                   jax.ShapeDtypeStruct(sm_in.shape, F32),
                   jax.ShapeDtypeStruct(state_C.shape, F32)),
        scratch_shapes=_tile_scratch(TT) + _tile_scratch(TT) + _tile_scratch(N_META)
        + [pltpu.VMEM((N_META, D_MODEL), F32)] + _state_scratch() + _state_scratch()
        + [pltpu.VMEM((2, HALF_SEQS, N_HEADS, HEAD_DIM, HEAD_DIM), F32), pltpu.VMEM((HIST + SL, D_POOL), F32),
           pltpu.SemaphoreType.DMA((2,)), pltpu.SemaphoreType.DMA((2,)), pltpu.SemaphoreType.DMA((1,))],
        compiler_params=_params(("arbitrary",)),
        name="main",
    )(meta_tokens, xp, xp, xp, proj_s, gcol_s, pool_in, state_n, sm_in, state_C, *layer_w)
    y_prompt = y_p.reshape(B, S, D_MODEL)
    pool_p = jnp.swapaxes(pool_p, 1, 2)
    m_p = m_p[:, :N_HEADS].reshape(1, B, N_HEADS)
    pool_s = jnp.swapaxes(pool_s, 1, 2)
    m_s = m_s.reshape(state_m.shape)

    PT = 256
    y_s = pl.pallas_call(
        _sout_kernel,
        grid=(n_tok // PT,),
        in_specs=[pl.BlockSpec((PT, D_MODEL), lambda i: (i, 0)), pl.BlockSpec((PT, 2048), lambda i: (i, 0)),
                  _const_spec(wout.shape), _const_spec(normf.shape)],
        out_specs=pl.BlockSpec((PT, D_MODEL), lambda i: (i, 0)),
        out_shape=jax.ShapeDtypeStruct((n_tok, D_MODEL), F32),
        compiler_params=_params(("arbitrary",)),
        name="sout",
    )(xs, ycat_s, wout, normf)
    y_sample = y_s.reshape(SB, SL, D_MODEL)

    return (y_prompt, y_sample, pool_p, c_p, n_p, m_p, pool_s, c_s, n_s, m_s)
```

```python
import collections
import functools

import jax
import jax.numpy as jnp
from jax import lax
from jax.experimental import pallas as pl
from jax.experimental.pallas import tpu as pltpu

D_MODEL = 1024
D_POOL = 1024
D_MLSTM = 1024
N_HEADS = 4
HEAD_DIM = 256
POOL_WINDOWS = (2, 4, 8, 16)
POOL_BUF = 15
HIST = 16
N_META = 16
PAST_LEN = 16384
EPS = 1e-6
D_MAIN = 2 * D_POOL + 5 * D_MLSTM
GATE_PAD = 128
K_SCALE = HEAD_DIM ** -0.5

OFF_U, OFF_ZA, OFF_Q, OFF_K, OFF_V, OFF_O, OFF_ZB = (i * 1024 for i in range(7))

PROMPT_TILE = 256
STEP_TILES = 2
SAMPLE_GROUP = 8
VMEM_LIMIT = 60000 * 1024

F32 = jnp.float32
BF16 = jnp.bfloat16

LayerW = collections.namedtuple("LayerW", "main gate bias norm1 pool pscale mhln out normf")


def _dot(a, b):
    return jnp.dot(a, b, preferred_element_type=F32)


def _dot_nt(a, b):
    return lax.dot_general(a, b, (((1,), (1,)), ((), ())), preferred_element_type=F32)


def _dot_tn(a, b):
    return lax.dot_general(a, b, (((0,), (0,)), ((), ())), preferred_element_type=F32)


def _rms(x, w):
    return x * lax.rsqrt(jnp.mean(x * x, axis=-1, keepdims=True) + EPS) * w


def _log_sigmoid(x):
    return jnp.minimum(x, 0.0) - jnp.log1p(jnp.exp(-jnp.abs(x)))


def _silu(x):
    return x * jax.nn.sigmoid(x)


def _split3(x):
    hi = x.astype(BF16)
    r = x - hi.astype(F32)
    mid = r.astype(BF16)
    lo = (r - mid.astype(F32)).astype(BF16)
    return hi, mid, lo


def _seq_mask(T, seq_len):
    row = lax.broadcasted_iota(jnp.int32, (T, T), 0)
    col = lax.broadcasted_iota(jnp.int32, (T, T), 1)
    causal = col <= row
    if seq_len < T:
        shift = seq_len.bit_length() - 1
        assert 1 << shift == seq_len
        causal = causal & ((row >> shift) == (col >> shift))
    return causal


def _to_rows(cols):
    T = cols.shape[0]
    pad = -T % 128
    if pad:
        cols = jnp.concatenate([cols, jnp.zeros((pad, cols.shape[1]), cols.dtype)], axis=0)
    return cols.T[:, 0:T]


def _gate_pre(xn, w):
    return _dot(xn, w.gate[...]) + w.bias[...]


def _cumsum_operands(g_col, causal):
    tri = jnp.where(causal, 1.0, 0.0).astype(BF16)
    return tri, _split3(_log_sigmoid(g_col))


def _cumsum(tri, parts):
    return sum(_dot(tri, p) for p in parts)


def _intra_weights(qb, kb, b_col, a_col, r_row, causal):
    dm = jnp.where(causal, b_col + r_row, -jnp.inf)
    m_t = jnp.maximum(a_col, jnp.max(dm, axis=-1, keepdims=True))
    w = jnp.exp(dm - m_t)
    inter = jnp.exp(a_col - m_t)
    return m_t, inter, _dot_nt(qb, kb) * (w * K_SCALE)


def _head_norm(ht, w_row):
    mu = jnp.mean(ht, axis=-1, keepdims=True)
    d = ht - mu
    var = jnp.mean(d * d, axis=-1, keepdims=True)
    return d * lax.rsqrt(var + EPS) * w_row


def _window_sums(ext):
    s2 = ext + pltpu.roll(ext, 1, axis=0)
    s4 = s2[:, 256:] + pltpu.roll(s2[:, 256:], 2, axis=0)
    s8 = s4[:, 256:] + pltpu.roll(s4[:, 256:], 4, axis=0)
    s16 = s8[:, 256:] + pltpu.roll(s8[:, 256:], 8, axis=0)
    return [s2[HIST:, 0:256], s4[HIST:, 0:256], s8[HIST:, 0:256], s16[HIST:, 0:256]]


def _pooled(ext, u, pos_col):
    sums = _window_sums(ext)
    return [sums[g] * (1.0 / jnp.minimum(float(w), pos_col + 1.0)) - u[:, g * 256:(g + 1) * 256]
            for g, w in enumerate(POOL_WINDOWS)]


def _pool_mix(pooled, wpool_ref):
    return jnp.concatenate([_dot(p.astype(BF16), wpool_ref[g]) for g, p in enumerate(pooled)], axis=-1)


def _inproj_steps(x_ref, scr, w):
    proj_scr, x_scr, gcol_scr, _, xn_scr = scr

    def norm():
        x = x_ref[...]
        x_scr[...] = x
        xn_scr[...] = _rms(x, w.norm1[...]).astype(BF16)

    def piece(c0):
        def run():
            proj_scr[:, c0:c0 + PIECE_COLS] = _dot(xn_scr[...], w.main[:, c0:c0 + PIECE_COLS])
        return run

    def gates():
        gcol_scr[...] = _gate_pre(xn_scr[...], w)

    return [norm] + [piece(c0) for c0 in range(0, D_MAIN, PIECE_COLS)] + [gates]


def _mid_steps(scr, pos0, w, state):
    proj_scr, x_scr, gcol_scr, ycat_scr, _ = scr
    hist_scr, c_scr, n_scr, m_scr = state
    T = x_scr.shape[0]

    def seg(off, h=None):
        if h is None:
            return proj_scr[:, off:off + 1024]
        return proj_scr[:, off + h * 256:off + (h + 1) * 256]

    causal = _seq_mask(T, T)
    g_col = gcol_scr[...]
    tri, lf_parts = _cumsum_operands(g_col, causal)
    yield
    b_col_all = _cumsum(tri, lf_parts)
    r_row_all = _to_rows(g_col - pltpu.roll(b_col_all, GATE_PAD - N_HEADS, axis=1))

    u = seg(OFF_U)
    ext = jnp.concatenate([hist_scr[...], u], axis=0)
    pos_col = (lax.broadcasted_iota(jnp.int32, (T, 1), 0) + pos0).astype(F32)
    pooled = _pooled(ext, u, pos_col)
    hist_scr[...] = ext[T:T + HIST, :]
    yield
    mixed = _pool_mix(pooled, w.pool)
    ycat_scr[:, 0:1024] = (mixed * w.pscale[...] * _silu(seg(OFF_ZA))).astype(BF16)

    def head(h):
        q, k, v = seg(OFF_Q, h), seg(OFF_K, h), seg(OFF_V, h)
        qb, kb, vb = q.astype(BF16), k.astype(BF16), v.astype(BF16)
        ig_col = g_col[:, h:h + 1]
        b_col = b_col_all[:, 4 + h:5 + h]
        r_row = r_row_all[h:h + 1, :]
        m_prev = m_scr[h:h + 1, 0:1]
        a_col = b_col + m_prev
        c_old = c_scr[h]
        n_old = n_scr[h:h + 1, :]
        q_c = _dot(qb, c_old.astype(BF16))
        m_t, inter, s = _intra_weights(qb, kb, b_col, a_col, r_row, causal)
        sb = s.astype(BF16)
        yield
        num = inter * q_c + _dot(sb, vb)
        qn = inter * jnp.sum(q * n_old, axis=-1, keepdims=True) + jnp.sum(s, axis=-1, keepdims=True)
        ht = num * (1.0 / jnp.maximum(jnp.abs(qn), jnp.exp(-m_t)))
        m_new = m_t[T - 1:T, :]
        b_last = b_col[T - 1:T, :]
        w_end = jnp.exp(b_last - b_col + ig_col - m_new) * K_SCALE
        decay = jnp.exp(b_last + m_prev - m_new)
        kw = k * w_end
        kwb = kw.astype(BF16)
        yield
        c_scr[h] = decay * c_old + _dot_tn(kwb, vb)
        n_scr[h:h + 1, :] = decay * n_old + jnp.sum(kw, axis=0, keepdims=True)
        m_scr[h:h + 1, :] = jnp.broadcast_to(m_new, (1, 128))
        hn = _head_norm(ht, w.mhln[:, h * 256:(h + 1) * 256])
        ycat_scr[:, 1024 + h * 256:1024 + (h + 1) * 256] = (
            hn * jax.nn.sigmoid(seg(OFF_O, h)) * _silu(seg(OFF_ZB, h))).astype(BF16)

    for pair in range(0, N_HEADS, HEADS_IN_FLIGHT):
        running = [head(h) for h in range(pair, pair + HEADS_IN_FLIGHT)]
        while running:
            alive = []
            for g in running:
                if next(g, g) is not g:
                    alive.append(g)
                yield
            running = alive


def _out_steps(scr, w, y_ref):
    _, x_scr, _, ycat_scr, _ = scr

    def residual():
        y_ref[...] = x_scr[...]

    def piece(c0):
        def run():
            cols = slice(c0, c0 + OUT_COLS)
            y_ref[:, cols] = y_ref[:, cols] + _dot(ycat_scr[...], w.out[:, cols])
        return run

    def norm():
        y_ref[...] = _rms(y_ref[...], w.normf[...])

    return [residual] + [piece(c0) for c0 in range(0, D_MODEL, OUT_COLS)] + [norm]


PIECE_COLS = 512
OUT_COLS = 256
HEADS_IN_FLIGHT = 2
MID_YIELDS = 2 + 3 * N_HEADS


def _run_interleaved(mid, lead, at_yield):
    assert len(at_yield) == MID_YIELDS
    for piece in lead:
        piece()
    for pieces in at_yield:
        next(mid)
        for piece in pieces:
            piece()
    for _ in mid:
        raise AssertionError("unexpected extra yield")


def _spread(pieces, yields):
    base, extra = divmod(len(pieces), yields)
    groups, start = [], 0
    for i in range(yields):
        stop = start + base + (i < extra)
        groups.append(pieces[start:stop])
        start = stop
    return groups


def _m_row(m_scr):
    lane = lax.broadcasted_iota(jnp.int32, (1, 128), 1)
    row = jnp.zeros((1, 128), F32)
    for h in range(N_HEADS):
        row = jnp.where(lane == h, m_scr[h:h + 1, :], row)
    return row


def _prompt_kernel(meta_ref, x0_ref, *refs, steps_per_seq):
    x_next = refs[:STEP_TILES]
    refs = refs[STEP_TILES:]
    w = LayerW(*refs[:9])
    y_ref, pool_out, c_out, n_out, m_out = refs[9:14]
    scr = (refs[14:19], refs[19:24])
    scr0, scr_meta = scr[0], refs[24:29]
    state, state_meta = refs[29:33], refs[33:37]
    hist_scr, c_scr, n_scr, m_scr = state
    s = pl.program_id(0)
    TT = PROMPT_TILE

    @pl.when(s == 0)
    def _():
        for ref in state_meta:
            ref[...] = jnp.zeros_like(ref)
        for p in _inproj_steps(meta_ref, scr_meta, w):
            p()
        for _ in _mid_steps(scr_meta, 0, w, state_meta):
            pass
        for p in _inproj_steps(x0_ref, scr0, w):
            p()

    @pl.when(s % steps_per_seq == 0)
    def _():
        for ref, ref_meta in zip(state, state_meta):
            ref[...] = ref_meta[...]

    out_pieces = []
    for j in range(STEP_TILES):
        norm, *proj = _inproj_steps(x_next[j], scr[1 - j % 2], w)
        if out_pieces:
            residual, o0, o1, o2, o3, out_norm = out_pieces
            lead = [residual, o0, norm, o1]
            at_yield = [[o2], [o3]] + _spread(proj, MID_YIELDS - 2)
            at_yield[3] = at_yield[3] + [out_norm]
        else:
            lead = [norm, proj[0]]
            at_yield = _spread(proj[1:], MID_YIELDS)
        _run_interleaved(_mid_steps(scr[j % 2], N_META, w, state), lead, at_yield)
        out_pieces = _out_steps(scr[j % 2], w, y_ref.at[j * TT:(j + 1) * TT, :])
    for piece in out_pieces:
        piece()

    @pl.when(s % steps_per_seq == steps_per_seq - 1)
    def _():
        b = s // steps_per_seq
        for j in range(POOL_BUF):
            pool_out[0, j, pl.ds(b, 1), :] = hist_scr[1 + j:2 + j, :]
        c_out[0, 0] = c_scr[...]
        n_out[0, 0] = n_scr[0:N_HEADS, :]
        m_out[pl.ds(b, 1), :] = _m_row(m_scr)


def _prep_kernel(wt_ref, wgt_ref, xs_ref, norm1_ref, wout_ref, wmain_ref, wgate_ref, proj_ref, woutb_ref,
                 xn_scr, *, n_seg):
    i = pl.program_id(0)

    @pl.when(i == 0)
    def _():
        g = jnp.concatenate([wgt_ref[...], jnp.zeros((GATE_PAD - 8, D_MODEL), F32)], axis=0)
        wgate_ref[...] = g.T.astype(BF16)
        xn_scr[...] = _rms(xs_ref[...], norm1_ref[...]).astype(BF16)

    woutb_ref[...] = wout_ref[...].astype(BF16)

    @pl.when(i < n_seg)
    def _():
        wb = wt_ref[...].T.astype(BF16)
        wmain_ref[...] = wb
        proj_ref[...] = _dot(xn_scr[...], wb)


def _expand(seq_col, vals):
    out = None
    for i, val in enumerate(vals):
        pick = jnp.where(seq_col == i, val, 0.0)
        out = pick if out is None else out + pick
    return out


def _sample_kernel(x_ref, proj_ref, pool_ref, c_ref, n_ref, m_ref, *refs, seq_len, pos0):
    w = LayerW(None, *refs[:8])
    y_ref, pool_out, c_out, n_out, m_out = refs[8:13]
    ext_scr, ycat_scr = refs[13:15]
    G = SAMPLE_GROUP
    T = G * seq_len
    x = x_ref[...]
    xn = _rms(x, w.norm1[...]).astype(BF16)
    causal = _seq_mask(T, seq_len)
    g_col = _gate_pre(xn, w)
    b_col_all = _cumsum(*_cumsum_operands(g_col, causal))
    r_row_all = _to_rows(g_col - pltpu.roll(b_col_all, GATE_PAD - N_HEADS, axis=1))
    seq_col = lax.broadcasted_iota(jnp.int32, (T, 1), 0) >> (seq_len.bit_length() - 1)
    pos_col = jnp.full((seq_len, 1), float(pos0), F32) + lax.broadcasted_iota(
        jnp.int32, (seq_len, 1), 0).astype(F32)

    pooled_rows = []
    for i in range(G):
        rows = slice(i * seq_len, (i + 1) * seq_len)
        u_i = proj_ref[rows, OFF_U:OFF_U + 1024]
        ext_scr[0:1, :] = jnp.zeros((1, 1024), F32)
        for j in range(POOL_BUF):
            ext_scr[1 + j:2 + j, :] = pool_ref[0, j, i:i + 1, :]
        ext_scr[HIST:HIST + seq_len, :] = u_i
        ext = ext_scr[...]
        pooled_rows.append(_pooled(ext, u_i, pos_col))
        for j in range(POOL_BUF):
            pool_out[0, j, i:i + 1, :] = ext[seq_len + 1 + j:seq_len + 2 + j, :]
    mixed = _pool_mix([jnp.concatenate(p, axis=0) for p in zip(*pooled_rows)], w.pool)
    y_a = mixed * w.pscale[...] * _silu(proj_ref[:, OFF_ZA:OFF_ZA + 1024])
    ycat_scr[:, 0:1024] = y_a.astype(BF16)

    lane4 = lax.broadcasted_iota(jnp.int32, (1, N_HEADS), 1)
    last = [(i + 1) * seq_len - 1 for i in range(G)]
    m_new_heads = [None] * N_HEADS

    def head(h):
        hc = slice(h * 256, (h + 1) * 256)
        q = proj_ref[:, OFF_Q + h * 256:OFF_Q + (h + 1) * 256]
        k = proj_ref[:, OFF_K + h * 256:OFF_K + (h + 1) * 256]
        v = proj_ref[:, OFF_V + h * 256:OFF_V + (h + 1) * 256]
        qb, kb, vb = q.astype(BF16), k.astype(BF16), v.astype(BF16)
        c_old = [c_ref[0, i, h] for i in range(G)]
        c_old_b = [c.astype(BF16) for c in c_old]
        yield
        q_cs = [_dot(qb, cb) for cb in c_old_b]
        ig_col = g_col[:, h:h + 1]
        b_col = b_col_all[:, 4 + h:5 + h]
        r_row = r_row_all[h:h + 1, :]
        m_prev = [m_ref[0, i:i + 1, h:h + 1] for i in range(G)]
        a_col = b_col + _expand(seq_col, m_prev)
        m_t, inter, s = _intra_weights(qb, kb, b_col, a_col, r_row, causal)
        sb = s.astype(BF16)
        yield
        sv = _dot(sb, vb)
        n_old = [n_ref[0, i, h:h + 1, :] for i in range(G)]
        num = inter * _expand(seq_col, q_cs) + sv
        qn = inter * jnp.sum(q * _expand(seq_col, n_old), axis=-1, keepdims=True) + jnp.sum(
            s, axis=-1, keepdims=True)
        ht = num * (1.0 / jnp.maximum(jnp.abs(qn), jnp.exp(-m_t)))
        m_new = [m_t[r:r + 1, :] for r in last]
        b_last = [b_col[r:r + 1, :] for r in last]
        w_end = jnp.exp(_expand(seq_col, b_last) - b_col + ig_col - _expand(seq_col, m_new)) * K_SCALE
        kw = k * w_end
        kwb = kw.astype(BF16)
        v_seq = [jnp.where(seq_col == i, vb, jnp.zeros_like(vb)) for i in range(G)]
        yield
        updates = [_dot_tn(kwb, v_i) for v_i in v_seq]
        for i in range(G):
            decay = jnp.exp(b_last[i] + m_prev[i] - m_new[i])
            c_out[0, i, h] = decay * c_old[i] + updates[i]
            n_out[0, i, h:h + 1, :] = decay * n_old[i] + jnp.sum(
                kw[i * seq_len:(i + 1) * seq_len, :], axis=0, keepdims=True)
        m_new_heads[h] = m_new
        hn = _head_norm(ht, w.mhln[:, hc])
        o = proj_ref[:, OFF_O + h * 256:OFF_O + (h + 1) * 256]
        zb = proj_ref[:, OFF_ZB + h * 256:OFF_ZB + (h + 1) * 256]
        ycat_scr[:, 1024 + h * 256:1024 + (h + 1) * 256] = (hn * jax.nn.sigmoid(o) * _silu(zb)).astype(BF16)

    running = [head(h) for h in range(N_HEADS)]
    while running:
        running = [g for g in running if next(g, g) is not g]
    for i in range(G):
        row = jnp.zeros((1, N_HEADS), F32)
        for h in range(N_HEADS):
            row = jnp.where(lane4 == h, m_new_heads[h][i], row)
        m_out[0, i:i + 1, :] = row

    y = _dot(ycat_scr[...], w.out[...])
    y_ref[...] = _rms(x + y, w.normf[...])


def _const_spec(shape):
    nd = len(shape)
    return pl.BlockSpec(shape, lambda *_: (0,) * nd, pipeline_mode=pl.Buffered(1))


def _params(sem):
    return pltpu.CompilerParams(dimension_semantics=sem, vmem_limit_bytes=VMEM_LIMIT)


def _tile_scratch(T):
    return [pltpu.VMEM((T, D_MAIN), F32), pltpu.VMEM((T, D_MODEL), F32), pltpu.VMEM((T, GATE_PAD), F32),
            pltpu.VMEM((T, 2048), BF16), pltpu.VMEM((T, D_MODEL), BF16)]


def _state_scratch():
    return [pltpu.VMEM((HIST, D_POOL), F32), pltpu.VMEM((N_HEADS, HEAD_DIM, HEAD_DIM), F32),
            pltpu.VMEM((8, HEAD_DIM), F32), pltpu.VMEM((8, 128), F32)]


def kernel(x_prompt, x_sample, state_pool, state_C, state_n, state_m, meta_tokens, norm1_w, w_in,
           b_if, w_pool, pool_scale, mhln_w, w_out, normf_w):
    B, S, _ = x_prompt.shape
    SB, SL, _ = x_sample.shape
    TT = PROMPT_TILE
    assert norm1_w.shape[0] == 1, "single layer"
    assert STEP_TILES % 2 == 0 and S % (STEP_TILES * TT) == 0 and SB % SAMPLE_GROUP == 0

    w_in_t = jnp.swapaxes(w_in[0], 0, 1)
    norm1 = norm1_w[0].reshape(1, D_MODEL)
    n_tok = SB * SL
    xs = x_sample.reshape(n_tok, D_MODEL)
    n_seg = D_MAIN // 1024
    wout_rows = w_out.shape[1] // (n_seg + 1)
    seg = lambda i: jnp.minimum(i, n_seg - 1)
    w_main, w_gate, proj_s, wout = pl.pallas_call(
        functools.partial(_prep_kernel, n_seg=n_seg),
        grid=(n_seg + 1,),
        in_specs=[pl.BlockSpec((1024, D_MODEL), lambda i: (seg(i), 0)),
                  pl.BlockSpec((8, D_MODEL), lambda i: (D_MAIN // 8, 0)),
                  _const_spec(xs.shape), _const_spec(norm1.shape),
                  pl.BlockSpec((wout_rows, D_MODEL), lambda i: (i, 0))],
        out_specs=(pl.BlockSpec((D_MODEL, 1024), lambda i: (0, seg(i))),
                   pl.BlockSpec((D_MODEL, GATE_PAD), lambda i: (0, 0)),
                   pl.BlockSpec((n_tok, 1024), lambda i: (0, seg(i))),
                   pl.BlockSpec((wout_rows, D_MODEL), lambda i: (i, 0))),
        out_shape=(jax.ShapeDtypeStruct((D_MODEL, D_MAIN), BF16),
                   jax.ShapeDtypeStruct((D_MODEL, GATE_PAD), BF16),
                   jax.ShapeDtypeStruct((n_tok, D_MAIN), F32),
                   jax.ShapeDtypeStruct(w_out.shape[1:], BF16)),
        scratch_shapes=[pltpu.VMEM((n_tok, D_MODEL), BF16)],
        compiler_params=_params(("arbitrary",)),
        name="prep",
    )(w_in_t, w_in_t, xs, norm1, w_out[0])
    bias_row = jnp.pad(b_if[0], (0, GATE_PAD - 8)).reshape(1, GATE_PAD)
    wpool = w_pool[0].astype(BF16)
    pscale = pool_scale[0].reshape(1, D_POOL)
    mhln = mhln_w[0].reshape(1, D_MLSTM)
    normf = normf_w.reshape(1, D_MODEL)
    layer_w = LayerW(w_main, w_gate, bias_row, norm1, wpool, pscale, mhln, wout, normf)
    layer_specs = [_const_spec(a.shape) for a in layer_w]

    n_tiles = B * S // TT
    steps = n_tiles // STEP_TILES
    steps_per_seq = S // (STEP_TILES * TT)
    xp = x_prompt.reshape(B * S, D_MODEL)
    y_p, pool_p, c_p, n_p, m_p = pl.pallas_call(
        functools.partial(_prompt_kernel, steps_per_seq=steps_per_seq),
        grid=(steps,),
        in_specs=[_const_spec(meta_tokens.shape),
                  pl.BlockSpec((TT, D_MODEL), lambda s: (0, 0))]
                 + [pl.BlockSpec((TT, D_MODEL),
                                 lambda s, j=j: (jnp.minimum(STEP_TILES * s + 1 + j, n_tiles - 1), 0))
                    for j in range(STEP_TILES)]
                 + layer_specs,
        out_specs=(pl.BlockSpec((STEP_TILES * TT, D_MODEL), lambda s: (s, 0)),
                   pl.BlockSpec((1, POOL_BUF, B, D_POOL), lambda s: (0, 0, 0, 0)),
                   pl.BlockSpec((1, 1, N_HEADS, HEAD_DIM, HEAD_DIM),
                                lambda s: (0, s // steps_per_seq, 0, 0, 0)),
                   pl.BlockSpec((1, 1, N_HEADS, HEAD_DIM), lambda s: (0, s // steps_per_seq, 0, 0)),
                   pl.BlockSpec((B, 128), lambda s: (0, 0))),
        out_shape=(jax.ShapeDtypeStruct((B * S, D_MODEL), F32),
                   jax.ShapeDtypeStruct((1, POOL_BUF, B, D_POOL), F32),
                   jax.ShapeDtypeStruct((1, B, N_HEADS, HEAD_DIM, HEAD_DIM), F32),
                   jax.ShapeDtypeStruct((1, B, N_HEADS, HEAD_DIM), F32),
                   jax.ShapeDtypeStruct((B, 128), F32)),
        scratch_shapes=_tile_scratch(TT) + _tile_scratch(TT) + _tile_scratch(N_META)
        + _state_scratch() + _state_scratch(),
        compiler_params=_params(("arbitrary",)),
        name="prompt",
    )(meta_tokens, xp, *([xp] * STEP_TILES), *layer_w)
    y_prompt = y_p.reshape(B, S, D_MODEL)
    pool_p = jnp.swapaxes(pool_p, 1, 2)
    m_p = m_p[:, :N_HEADS].reshape(1, B, N_HEADS)

    G = SAMPLE_GROUP
    GT = G * SL
    pool_in = jnp.swapaxes(state_pool, 1, 2)
    sample_w = layer_w[1:]
    y_s, pool_s, c_s, n_s, m_s = pl.pallas_call(
        functools.partial(_sample_kernel, seq_len=SL, pos0=PAST_LEN),
        grid=(SB // G,),
        in_specs=[pl.BlockSpec((GT, D_MODEL), lambda i: (i, 0)),
                  pl.BlockSpec((GT, D_MAIN), lambda i: (i, 0)),
                  pl.BlockSpec((1, POOL_BUF, G, D_POOL), lambda i: (0, 0, i, 0)),
                  pl.BlockSpec((1, G, N_HEADS, HEAD_DIM, HEAD_DIM), lambda i: (0, i, 0, 0, 0)),
                  pl.BlockSpec((1, G, N_HEADS, HEAD_DIM), lambda i: (0, i, 0, 0)),
                  pl.BlockSpec((1, G, N_HEADS), lambda i: (0, i, 0))]
                 + [_const_spec(a.shape) for a in sample_w],
        out_specs=(pl.BlockSpec((GT, D_MODEL), lambda i: (i, 0)),
                   pl.BlockSpec((1, POOL_BUF, G, D_POOL), lambda i: (0, 0, i, 0)),
                   pl.BlockSpec((1, G, N_HEADS, HEAD_DIM, HEAD_DIM), lambda i: (0, i, 0, 0, 0)),
                   pl.BlockSpec((1, G, N_HEADS, HEAD_DIM), lambda i: (0, i, 0, 0)),
                   pl.BlockSpec((1, G, N_HEADS), lambda i: (0, i, 0))),
        out_shape=(jax.ShapeDtypeStruct((n_tok, D_MODEL), F32),
                   jax.ShapeDtypeStruct(pool_in.shape, F32),
                   jax.ShapeDtypeStruct(state_C.shape, F32),
                   jax.ShapeDtypeStruct(state_n.shape, F32),
                   jax.ShapeDtypeStruct(state_m.shape, F32)),
        scratch_shapes=[pltpu.VMEM((HIST + SL, D_POOL), F32), pltpu.VMEM((GT, 2048), BF16)],
        compiler_params=_params(("arbitrary",)),
        name="sample",
    )(xs, proj_s, pool_in, state_C, state_n, state_m, *sample_w)
    y_sample = y_s.reshape(SB, SL, D_MODEL)
    pool_s = jnp.swapaxes(pool_s, 1, 2)

    return (y_prompt, y_sample, pool_p, c_p, n_p, m_p, pool_s, c_s, n_s, m_s)
```

```python
import collections
import functools

import jax
import jax.numpy as jnp
from jax import lax
from jax.experimental import pallas as pl
from jax.experimental.pallas import tpu as pltpu

D_MODEL = 1024
D_POOL = 1024
D_MLSTM = 1024
N_HEADS = 4
HEAD_DIM = 256
POOL_WINDOWS = (2, 4, 8, 16)
POOL_BUF = 15
HIST = 16
N_META = 16
PAST_LEN = 16384
EPS = 1e-6
D_MAIN = 2 * D_POOL + 5 * D_MLSTM
GATE_PAD = 128
K_SCALE = HEAD_DIM ** -0.5

OFF_U, OFF_ZA, OFF_Q, OFF_K, OFF_V, OFF_O, OFF_ZB = (i * 1024 for i in range(7))

PROMPT_TILE = 256
STEP_TILES = 2
SAMPLE_GROUP = 8
VMEM_LIMIT = 60000 * 1024

F32 = jnp.float32
BF16 = jnp.bfloat16

LayerW = collections.namedtuple("LayerW", "main gate bias norm1 pool pscale mhln out normf")


def _dot(a, b):
    return jnp.dot(a, b, preferred_element_type=F32)


def _dot_nt(a, b):
    return lax.dot_general(a, b, (((1,), (1,)), ((), ())), preferred_element_type=F32)


def _dot_tn(a, b):
    return lax.dot_general(a, b, (((0,), (0,)), ((), ())), preferred_element_type=F32)


def _rms(x, w):
    return x * lax.rsqrt(jnp.mean(x * x, axis=-1, keepdims=True) + EPS) * w


def _log_sigmoid(x):
    return jnp.minimum(x, 0.0) - jnp.log1p(jnp.exp(-jnp.abs(x)))


def _silu(x):
    return x * jax.nn.sigmoid(x)


def _split3(x):
    hi = x.astype(BF16)
    r = x - hi.astype(F32)
    mid = r.astype(BF16)
    lo = (r - mid.astype(F32)).astype(BF16)
    return hi, mid, lo


def _seq_mask(T, seq_len):
    row = lax.broadcasted_iota(jnp.int32, (T, T), 0)
    col = lax.broadcasted_iota(jnp.int32, (T, T), 1)
    causal = col <= row
    if seq_len < T:
        shift = seq_len.bit_length() - 1
        assert 1 << shift == seq_len
        causal = causal & ((row >> shift) == (col >> shift))
    return causal


def _to_rows(cols):
    T = cols.shape[0]
    pad = -T % 128
    if pad:
        cols = jnp.concatenate([cols, jnp.zeros((pad, cols.shape[1]), cols.dtype)], axis=0)
    return cols.T[:, 0:T]


def _gate_pre(xn, w):
    return _dot(xn, w.gate[...]) + w.bias[...]


def _cumsum_operands(g_col, causal):
    tri = jnp.where(causal, 1.0, 0.0).astype(BF16)
    return tri, _split3(_log_sigmoid(g_col))


def _cumsum(tri, parts):
    return sum(_dot(tri, p) for p in parts)


def _intra_weights(qb, kb, b_col, a_col, r_row, causal):
    dm = jnp.where(causal, b_col + r_row, -jnp.inf)
    m_t = jnp.maximum(a_col, jnp.max(dm, axis=-1, keepdims=True))
    w = jnp.exp(dm - m_t)
    inter = jnp.exp(a_col - m_t)
    return m_t, inter, _dot_nt(qb, kb) * (w * K_SCALE)


def _head_norm(ht, w_row):
    mu = jnp.mean(ht, axis=-1, keepdims=True)
    d = ht - mu
    var = jnp.mean(d * d, axis=-1, keepdims=True)
    return d * lax.rsqrt(var + EPS) * w_row


def _window_sums(ext):
    s2 = ext + pltpu.roll(ext, 1, axis=0)
    s4 = s2[:, 256:] + pltpu.roll(s2[:, 256:], 2, axis=0)
    s8 = s4[:, 256:] + pltpu.roll(s4[:, 256:], 4, axis=0)
    s16 = s8[:, 256:] + pltpu.roll(s8[:, 256:], 8, axis=0)
    return [s2[HIST:, 0:256], s4[HIST:, 0:256], s8[HIST:, 0:256], s16[HIST:, 0:256]]


def _pooled(ext, u, pos_col):
    sums = _window_sums(ext)
    return [sums[g] * (1.0 / jnp.minimum(float(w), pos_col + 1.0)) - u[:, g * 256:(g + 1) * 256]
            for g, w in enumerate(POOL_WINDOWS)]


def _pool_mix(pooled, wpool_ref):
    return jnp.concatenate([_dot(p.astype(BF16), wpool_ref[g]) for g, p in enumerate(pooled)], axis=-1)


def _inproj_steps(x_ref, scr, w):
    proj_scr, x_scr, gcol_scr, _, xn_scr = scr[:5]

    def norm():
        x = x_ref[...]
        x_scr[...] = x
        xn_scr[...] = _rms(x, w.norm1[...]).astype(BF16)

    def piece(c0):
        def run():
            proj_scr[:, c0:c0 + PIECE_COLS] = _dot(xn_scr[...], w.main[:, c0:c0 + PIECE_COLS])
        return run

    def gates():
        gcol_scr[...] = _gate_pre(xn_scr[...], w)

    return [norm, gates] + [piece(c0) for c0 in range(0, D_MAIN, PIECE_COLS)]


def _pre_stage(scr, pos0, hist):
    proj_scr, _, gcol_scr, _, _, lf_scr, pooled_scr, hist_scr = scr
    T = proj_scr.shape[0]
    for i, part in enumerate(_split3(_log_sigmoid(gcol_scr[...]))):
        lf_scr[i] = part
    u = proj_scr[:, OFF_U:OFF_U + 1024]
    ext = jnp.concatenate([hist, u], axis=0)
    pos_col = (lax.broadcasted_iota(jnp.int32, (T, 1), 0) + pos0).astype(F32)
    for g, p in enumerate(_pooled(ext, u, pos_col)):
        pooled_scr[:, g * 256:(g + 1) * 256] = p
    hist_scr[...] = ext[T:T + HIST, :]


def _mid_steps(scr, w, state):
    proj_scr, x_scr, gcol_scr, ycat_scr, _, lf_scr, pooled_scr, _ = scr
    c_scr, n_scr, m_scr = state
    T = x_scr.shape[0]

    def seg(off, h=None):
        if h is None:
            return proj_scr[:, off:off + 1024]
        return proj_scr[:, off + h * 256:off + (h + 1) * 256]

    causal = _seq_mask(T, T)
    g_col = gcol_scr[...]
    tri = jnp.where(causal, 1.0, 0.0).astype(BF16)
    b_col_all = _cumsum(tri, [lf_scr[i] for i in range(3)])
    r_row_all = _to_rows(g_col - pltpu.roll(b_col_all, GATE_PAD - N_HEADS, axis=1))

    mixed = _pool_mix([pooled_scr[:, g * 256:(g + 1) * 256] for g in range(len(POOL_WINDOWS))], w.pool)
    ycat_scr[:, 0:1024] = (mixed * w.pscale[...] * _silu(seg(OFF_ZA))).astype(BF16)
    yield

    def head(h):
        q, k, v = seg(OFF_Q, h), seg(OFF_K, h), seg(OFF_V, h)
        qb, kb, vb = q.astype(BF16), k.astype(BF16), v.astype(BF16)
        ig_col = g_col[:, h:h + 1]
        b_col = b_col_all[:, 4 + h:5 + h]
        r_row = r_row_all[h:h + 1, :]
        m_prev = m_scr[h:h + 1, 0:1]
        a_col = b_col + m_prev
        c_old = c_scr[h]
        n_old = n_scr[h:h + 1, :]
        q_c = _dot(qb, c_old.astype(BF16))
        m_t, inter, s = _intra_weights(qb, kb, b_col, a_col, r_row, causal)
        sb = s.astype(BF16)
        yield
        num = inter * q_c + _dot(sb, vb)
        qn = inter * jnp.sum(q * n_old, axis=-1, keepdims=True) + jnp.sum(s, axis=-1, keepdims=True)
        ht = num * (1.0 / jnp.maximum(jnp.abs(qn), jnp.exp(-m_t)))
        m_new = m_t[T - 1:T, :]
        b_last = b_col[T - 1:T, :]
        w_end = jnp.exp(b_last - b_col + ig_col - m_new) * K_SCALE
        decay = jnp.exp(b_last + m_prev - m_new)
        kw = k * w_end
        kwb = kw.astype(BF16)
        yield
        c_scr[h] = decay * c_old + _dot_tn(kwb, vb)
        n_scr[h:h + 1, :] = decay * n_old + jnp.sum(kw, axis=0, keepdims=True)
        m_scr[h:h + 1, :] = jnp.broadcast_to(m_new, (1, 128))
        hn = _head_norm(ht, w.mhln[:, h * 256:(h + 1) * 256])
        ycat_scr[:, 1024 + h * 256:1024 + (h + 1) * 256] = (
            hn * jax.nn.sigmoid(seg(OFF_O, h)) * _silu(seg(OFF_ZB, h))).astype(BF16)

    for pair in range(0, N_HEADS, HEADS_IN_FLIGHT):
        running = [head(h) for h in range(pair, pair + HEADS_IN_FLIGHT)]
        while running:
            alive = []
            for g in running:
                if next(g, g) is not g:
                    alive.append(g)
                yield
            running = alive


def _out_steps(scr, w, y_ref):
    _, x_scr, _, ycat_scr = scr[:4]

    def residual():
        y_ref[...] = x_scr[...]

    def piece(c0):
        def run():
            cols = slice(c0, c0 + OUT_COLS)
            y_ref[:, cols] = y_ref[:, cols] + _dot(ycat_scr[...], w.out[:, cols])
        return run

    def norm():
        y_ref[...] = _rms(y_ref[...], w.normf[...])

    return [residual] + [piece(c0) for c0 in range(0, D_MODEL, OUT_COLS)] + [norm]


PIECE_COLS = 512
OUT_COLS = 256
HEADS_IN_FLIGHT = 4
MID_YIELDS = 1 + 3 * N_HEADS
PRE_AT = 8


def _run_interleaved(mid, lead, at_yield):
    assert len(at_yield) == MID_YIELDS
    for piece in lead:
        piece()
    for pieces in at_yield:
        next(mid)
        for piece in pieces:
            piece()
    for _ in mid:
        raise AssertionError("unexpected extra yield")


def _spread(pieces, yields):
    base, extra = divmod(len(pieces), yields)
    groups, start = [], 0
    for i in range(yields):
        stop = start + base + (i < extra)
        groups.append(pieces[start:stop])
        start = stop
    return groups


def _m_row(m_scr):
    lane = lax.broadcasted_iota(jnp.int32, (1, 128), 1)
    row = jnp.zeros((1, 128), F32)
    for h in range(N_HEADS):
        row = jnp.where(lane == h, m_scr[h:h + 1, :], row)
    return row


def _prompt_kernel(meta_ref, x0_ref, *refs, steps_per_seq):
    x_next = refs[:STEP_TILES]
    refs = refs[STEP_TILES:]
    w = LayerW(*refs[:9])
    y_ref, pool_out, c_out, n_out, m_out = refs[9:14]
    scr = (refs[14:22], refs[22:30])
    scr0, scr_meta = scr[0], refs[30:38]
    state, state_meta = refs[38:41], refs[41:44]
    c_scr, n_scr, m_scr = state
    hist_of = lambda slot_scr: slot_scr[7]
    s = pl.program_id(0)
    TT = PROMPT_TILE

    @pl.when(s == 0)
    def _():
        for ref in state_meta:
            ref[...] = jnp.zeros_like(ref)
        for p in _inproj_steps(meta_ref, scr_meta, w):
            p()
        _pre_stage(scr_meta, 0, jnp.zeros((HIST, D_POOL), F32))
        for _ in _mid_steps(scr_meta, w, state_meta):
            pass
        for p in _inproj_steps(x0_ref, scr0, w):
            p()
        _pre_stage(scr0, N_META, hist_of(scr_meta)[...])

    @pl.when(s % steps_per_seq == 0)
    def _():
        for ref, ref_meta in zip(state, state_meta):
            ref[...] = ref_meta[...]

    next_opens_seq = (s + 1) % steps_per_seq == 0
    out_pieces = []
    for j in range(STEP_TILES):
        this, nxt = scr[j % 2], scr[1 - j % 2]
        norm, gates, *proj = _inproj_steps(x_next[j], nxt, w)
        if out_pieces:
            residual, o0, o1, o2, o3, out_norm = out_pieces
            lead = [residual, o0, norm, o1, gates]
            at_yield = [[o2, o3]] + _spread(proj, MID_YIELDS - 1)
            at_yield[2] = at_yield[2] + [out_norm]
        else:
            lead = [norm, proj[0], gates]
            at_yield = _spread(proj[1:], MID_YIELDS)
        opens_seq = next_opens_seq if j == STEP_TILES - 1 else False

        def pre(this=this, nxt=nxt, opens_seq=opens_seq):
            hist = hist_of(this)[...]
            if opens_seq is not False:
                hist = jnp.where(opens_seq, hist_of(scr_meta)[...], hist)
            _pre_stage(nxt, N_META, hist)

        at_yield[PRE_AT] = at_yield[PRE_AT] + [pre]
        _run_interleaved(_mid_steps(this, w, state), lead, at_yield)
        out_pieces = _out_steps(this, w, y_ref.at[j * TT:(j + 1) * TT, :])
    for piece in out_pieces:
        piece()

    @pl.when(s % steps_per_seq == steps_per_seq - 1)
    def _():
        b = s // steps_per_seq
        last_hist = hist_of(scr[(STEP_TILES - 1) % 2])
        for j in range(POOL_BUF):
            pool_out[0, j, pl.ds(b, 1), :] = last_hist[1 + j:2 + j, :]
        c_out[0, 0] = c_scr[...]
        n_out[0, 0] = n_scr[0:N_HEADS, :]
        m_out[pl.ds(b, 1), :] = _m_row(m_scr)


def _prep_kernel(wt_ref, wgt_ref, xs_ref, norm1_ref, wout_ref, wmain_ref, wgate_ref, proj_ref, woutb_ref,
                 xn_scr, *, n_seg):
    i = pl.program_id(0)

    @pl.when(i == 0)
    def _():
        g = jnp.concatenate([wgt_ref[...], jnp.zeros((GATE_PAD - 8, D_MODEL), F32)], axis=0)
        wgate_ref[...] = g.T.astype(BF16)
        xn_scr[...] = _rms(xs_ref[...], norm1_ref[...]).astype(BF16)

    woutb_ref[...] = wout_ref[...].astype(BF16)

    @pl.when(i < n_seg)
    def _():
        wb = wt_ref[...].T.astype(BF16)
        wmain_ref[...] = wb
        proj_ref[...] = _dot(xn_scr[...], wb)


def _expand(seq_col, vals):
    out = None
    for i, val in enumerate(vals):
        pick = jnp.where(seq_col == i, val, 0.0)
        out = pick if out is None else out + pick
    return out


def _sample_kernel(x_ref, proj_ref, pool_ref, c_ref, n_ref, m_ref, *refs, seq_len, pos0):
    w = LayerW(None, *refs[:8])
    y_ref, pool_out, c_out, n_out, m_out = refs[8:13]
    ext_scr, ycat_scr = refs[13:15]
    G = SAMPLE_GROUP
    T = G * seq_len
    x = x_ref[...]
    xn = _rms(x, w.norm1[...]).astype(BF16)
    causal = _seq_mask(T, seq_len)
    g_col = _gate_pre(xn, w)
    b_col_all = _cumsum(*_cumsum_operands(g_col, causal))
    r_row_all = _to_rows(g_col - pltpu.roll(b_col_all, GATE_PAD - N_HEADS, axis=1))
    seq_col = lax.broadcasted_iota(jnp.int32, (T, 1), 0) >> (seq_len.bit_length() - 1)
    pos_col = jnp.full((seq_len, 1), float(pos0), F32) + lax.broadcasted_iota(
        jnp.int32, (seq_len, 1), 0).astype(F32)

    pooled_rows = []
    for i in range(G):
        rows = slice(i * seq_len, (i + 1) * seq_len)
        u_i = proj_ref[rows, OFF_U:OFF_U + 1024]
        ext_scr[0:1, :] = jnp.zeros((1, 1024), F32)
        for j in range(POOL_BUF):
            ext_scr[1 + j:2 + j, :] = pool_ref[0, j, i:i + 1, :]
        ext_scr[HIST:HIST + seq_len, :] = u_i
        ext = ext_scr[...]
        pooled_rows.append(_pooled(ext, u_i, pos_col))
        for j in range(POOL_BUF):
            pool_out[0, j, i:i + 1, :] = ext[seq_len + 1 + j:seq_len + 2 + j, :]
    mixed = _pool_mix([jnp.concatenate(p, axis=0) for p in zip(*pooled_rows)], w.pool)
    y_a = mixed * w.pscale[...] * _silu(proj_ref[:, OFF_ZA:OFF_ZA + 1024])
    ycat_scr[:, 0:1024] = y_a.astype(BF16)

    lane4 = lax.broadcasted_iota(jnp.int32, (1, N_HEADS), 1)
    last = [(i + 1) * seq_len - 1 for i in range(G)]
    m_new_heads = [None] * N_HEADS

    def head(h):
        hc = slice(h * 256, (h + 1) * 256)
        q = proj_ref[:, OFF_Q + h * 256:OFF_Q + (h + 1) * 256]
        k = proj_ref[:, OFF_K + h * 256:OFF_K + (h + 1) * 256]
        v = proj_ref[:, OFF_V + h * 256:OFF_V + (h + 1) * 256]
        qb, kb, vb = q.astype(BF16), k.astype(BF16), v.astype(BF16)
        c_old = [c_ref[0, i, h] for i in range(G)]
        c_old_b = [c.astype(BF16) for c in c_old]
        yield
        q_cs = [_dot(qb, cb) for cb in c_old_b]
        ig_col = g_col[:, h:h + 1]
        b_col = b_col_all[:, 4 + h:5 + h]
        r_row = r_row_all[h:h + 1, :]
        m_prev = [m_ref[0, i:i + 1, h:h + 1] for i in range(G)]
        a_col = b_col + _expand(seq_col, m_prev)
        m_t, inter, s = _intra_weights(qb, kb, b_col, a_col, r_row, causal)
        sb = s.astype(BF16)
        yield
        sv = _dot(sb, vb)
        n_old = [n_ref[0, i, h:h + 1, :] for i in range(G)]
        num = inter * _expand(seq_col, q_cs) + sv
        qn = inter * jnp.sum(q * _expand(seq_col, n_old), axis=-1, keepdims=True) + jnp.sum(
            s, axis=-1, keepdims=True)
        ht = num * (1.0 / jnp.maximum(jnp.abs(qn), jnp.exp(-m_t)))
        m_new = [m_t[r:r + 1, :] for r in last]
        b_last = [b_col[r:r + 1, :] for r in last]
        w_end = jnp.exp(_expand(seq_col, b_last) - b_col + ig_col - _expand(seq_col, m_new)) * K_SCALE
        kw = k * w_end
        kwb = kw.astype(BF16)
        v_seq = [jnp.where(seq_col == i, vb, jnp.zeros_like(vb)) for i in range(G)]
        yield
        updates = [_dot_tn(kwb, v_i) for v_i in v_seq]
        for i in range(G):
            decay = jnp.exp(b_last[i] + m_prev[i] - m_new[i])
            c_out[0, i, h] = decay * c_old[i] + updates[i]
            n_out[0, i, h:h + 1, :] = decay * n_old[i] + jnp.sum(
                kw[i * seq_len:(i + 1) * seq_len, :], axis=0, keepdims=True)
        m_new_heads[h] = m_new
        hn = _head_norm(ht, w.mhln[:, hc])
        o = proj_ref[:, OFF_O + h * 256:OFF_O + (h + 1) * 256]
        zb = proj_ref[:, OFF_ZB + h * 256:OFF_ZB + (h + 1) * 256]
        ycat_scr[:, 1024 + h * 256:1024 + (h + 1) * 256] = (hn * jax.nn.sigmoid(o) * _silu(zb)).astype(BF16)

    running = [head(h) for h in range(N_HEADS)]
    while running:
        running = [g for g in running if next(g, g) is not g]
    for i in range(G):
        row = jnp.zeros((1, N_HEADS), F32)
        for h in range(N_HEADS):
            row = jnp.where(lane4 == h, m_new_heads[h][i], row)
        m_out[0, i:i + 1, :] = row

    y = _dot(ycat_scr[...], w.out[...])
    y_ref[...] = _rms(x + y, w.normf[...])


def _const_spec(shape):
    nd = len(shape)
    return pl.BlockSpec(shape, lambda *_: (0,) * nd, pipeline_mode=pl.Buffered(1))


def _params(sem):
    return pltpu.CompilerParams(dimension_semantics=sem, vmem_limit_bytes=VMEM_LIMIT)


def _tile_scratch(T):
    return [pltpu.VMEM((T, D_MAIN), F32), pltpu.VMEM((T, D_MODEL), F32), pltpu.VMEM((T, GATE_PAD), F32),
            pltpu.VMEM((T, 2048), BF16), pltpu.VMEM((T, D_MODEL), BF16),
            pltpu.VMEM((3, T, GATE_PAD), BF16), pltpu.VMEM((T, D_POOL), F32), pltpu.VMEM((HIST, D_POOL), F32)]


def _state_scratch():
    return [pltpu.VMEM((N_HEADS, HEAD_DIM, HEAD_DIM), F32), pltpu.VMEM((8, HEAD_DIM), F32),
            pltpu.VMEM((8, 128), F32)]


def kernel(x_prompt, x_sample, state_pool, state_C, state_n, state_m, meta_tokens, norm1_w, w_in,
           b_if, w_pool, pool_scale, mhln_w, w_out, normf_w):
    B, S, _ = x_prompt.shape
    SB, SL, _ = x_sample.shape
    TT = PROMPT_TILE
    assert norm1_w.shape[0] == 1, "single layer"
    assert STEP_TILES % 2 == 0 and S % (STEP_TILES * TT) == 0 and SB % SAMPLE_GROUP == 0

    w_in_t = jnp.swapaxes(w_in[0], 0, 1)
    norm1 = norm1_w[0].reshape(1, D_MODEL)
    n_tok = SB * SL
    xs = x_sample.reshape(n_tok, D_MODEL)
    n_seg = D_MAIN // 1024
    wout_rows = w_out.shape[1] // (n_seg + 1)
    seg = lambda i: jnp.minimum(i, n_seg - 1)
    w_main, w_gate, proj_s, wout = pl.pallas_call(
        functools.partial(_prep_kernel, n_seg=n_seg),
        grid=(n_seg + 1,),
        in_specs=[pl.BlockSpec((1024, D_MODEL), lambda i: (seg(i), 0)),
                  pl.BlockSpec((8, D_MODEL), lambda i: (D_MAIN // 8, 0)),
                  _const_spec(xs.shape), _const_spec(norm1.shape),
                  pl.BlockSpec((wout_rows, D_MODEL), lambda i: (i, 0))],
        out_specs=(pl.BlockSpec((D_MODEL, 1024), lambda i: (0, seg(i))),
                   pl.BlockSpec((D_MODEL, GATE_PAD), lambda i: (0, 0)),
                   pl.BlockSpec((n_tok, 1024), lambda i: (0, seg(i))),
                   pl.BlockSpec((wout_rows, D_MODEL), lambda i: (i, 0))),
        out_shape=(jax.ShapeDtypeStruct((D_MODEL, D_MAIN), BF16),
                   jax.ShapeDtypeStruct((D_MODEL, GATE_PAD), BF16),
                   jax.ShapeDtypeStruct((n_tok, D_MAIN), F32),
                   jax.ShapeDtypeStruct(w_out.shape[1:], BF16)),
        scratch_shapes=[pltpu.VMEM((n_tok, D_MODEL), BF16)],
        compiler_params=_params(("arbitrary",)),
        name="prep",
    )(w_in_t, w_in_t, xs, norm1, w_out[0])
    bias_row = jnp.pad(b_if[0], (0, GATE_PAD - 8)).reshape(1, GATE_PAD)
    wpool = w_pool[0].astype(BF16)
    pscale = pool_scale[0].reshape(1, D_POOL)
    mhln = mhln_w[0].reshape(1, D_MLSTM)
    normf = normf_w.reshape(1, D_MODEL)
    layer_w = LayerW(w_main, w_gate, bias_row, norm1, wpool, pscale, mhln, wout, normf)
    layer_specs = [_const_spec(a.shape) for a in layer_w]

    n_tiles = B * S // TT
    steps = n_tiles // STEP_TILES
    steps_per_seq = S // (STEP_TILES * TT)
    xp = x_prompt.reshape(B * S, D_MODEL)
    y_p, pool_p, c_p, n_p, m_p = pl.pallas_call(
        functools.partial(_prompt_kernel, steps_per_seq=steps_per_seq),
        grid=(steps,),
        in_specs=[_const_spec(meta_tokens.shape),
                  pl.BlockSpec((TT, D_MODEL), lambda s: (0, 0))]
                 + [pl.BlockSpec((TT, D_MODEL),
                                 lambda s, j=j: (jnp.minimum(STEP_TILES * s + 1 + j, n_tiles - 1), 0))
                    for j in range(STEP_TILES)]
                 + layer_specs,
        out_specs=(pl.BlockSpec((STEP_TILES * TT, D_MODEL), lambda s: (s, 0)),
                   pl.BlockSpec((1, POOL_BUF, B, D_POOL), lambda s: (0, 0, 0, 0)),
                   pl.BlockSpec((1, 1, N_HEADS, HEAD_DIM, HEAD_DIM),
                                lambda s: (0, s // steps_per_seq, 0, 0, 0)),
                   pl.BlockSpec((1, 1, N_HEADS, HEAD_DIM), lambda s: (0, s // steps_per_seq, 0, 0)),
                   pl.BlockSpec((B, 128), lambda s: (0, 0))),
        out_shape=(jax.ShapeDtypeStruct((B * S, D_MODEL), F32),
                   jax.ShapeDtypeStruct((1, POOL_BUF, B, D_POOL), F32),
                   jax.ShapeDtypeStruct((1, B, N_HEADS, HEAD_DIM, HEAD_DIM), F32),
                   jax.ShapeDtypeStruct((1, B, N_HEADS, HEAD_DIM), F32),
                   jax.ShapeDtypeStruct((B, 128), F32)),
        scratch_shapes=_tile_scratch(TT) + _tile_scratch(TT) + _tile_scratch(N_META)
        + _state_scratch() + _state_scratch(),
        compiler_params=_params(("arbitrary",)),
        name="prompt",
    )(meta_tokens, xp, *([xp] * STEP_TILES), *layer_w)
    y_prompt = y_p.reshape(B, S, D_MODEL)
    pool_p = jnp.swapaxes(pool_p, 1, 2)
    m_p = m_p[:, :N_HEADS].reshape(1, B, N_HEADS)

    G = SAMPLE_GROUP
    GT = G * SL
    pool_in = jnp.swapaxes(state_pool, 1, 2)
    sample_w = layer_w[1:]
    y_s, pool_s, c_s, n_s, m_s = pl.pallas_call(
        functools.partial(_sample_kernel, seq_len=SL, pos0=PAST_LEN),
        grid=(SB // G,),
        in_specs=[pl.BlockSpec((GT, D_MODEL), lambda i: (i, 0)),
                  pl.BlockSpec((GT, D_MAIN), lambda i: (i, 0)),
                  pl.BlockSpec((1, POOL_BUF, G, D_POOL), lambda i: (0, 0, i, 0)),
                  pl.BlockSpec((1, G, N_HEADS, HEAD_DIM, HEAD_DIM), lambda i: (0, i, 0, 0, 0)),
                  pl.BlockSpec((1, G, N_HEADS, HEAD_DIM), lambda i: (0, i, 0, 0)),
                  pl.BlockSpec((1, G, N_HEADS), lambda i: (0, i, 0))]
                 + [_const_spec(a.shape) for a in sample_w],
        out_specs=(pl.BlockSpec((GT, D_MODEL), lambda i: (i, 0)),
                   pl.BlockSpec((1, POOL_BUF, G, D_POOL), lambda i: (0, 0, i, 0)),
                   pl.BlockSpec((1, G, N_HEADS, HEAD_DIM, HEAD_DIM), lambda i: (0, i, 0, 0, 0)),
                   pl.BlockSpec((1, G, N_HEADS, HEAD_DIM), lambda i: (0, i, 0, 0)),
                   pl.BlockSpec((1, G, N_HEADS), lambda i: (0, i, 0))),
        out_shape=(jax.ShapeDtypeStruct((n_tok, D_MODEL), F32),
                   jax.ShapeDtypeStruct(pool_in.shape, F32),
                   jax.ShapeDtypeStruct(state_C.shape, F32),
                   jax.ShapeDtypeStruct(state_n.shape, F32),
                   jax.ShapeDtypeStruct(state_m.shape, F32)),
        scratch_shapes=[pltpu.VMEM((HIST + SL, D_POOL), F32), pltpu.VMEM((GT, 2048), BF16)],
        compiler_params=_params(("arbitrary",)),
        name="sample",
    )(xs, proj_s, pool_in, state_C, state_n, state_m, *sample_w)
    y_sample = y_s.reshape(SB, SL, D_MODEL)
    pool_s = jnp.swapaxes(pool_s, 1, 2)

    return (y_prompt, y_sample, pool_p, c_p, n_p, m_p, pool_s, c_s, n_s, m_s)
```

```python
import collections
import functools

import jax
import jax.numpy as jnp
from jax import lax
from jax.experimental import pallas as pl
from jax.experimental.pallas import tpu as pltpu

D_MODEL = 1024
D_POOL = 1024
D_MLSTM = 1024
N_HEADS = 4
HEAD_DIM = 256
POOL_WINDOWS = (2, 4, 8, 16)
POOL_BUF = 15
HIST = 16
N_META = 16
PAST_LEN = 16384
EPS = 1e-6
D_MAIN = 2 * D_POOL + 5 * D_MLSTM
GATE_PAD = 128
K_SCALE = HEAD_DIM ** -0.5

OFF_U, OFF_ZA, OFF_Q, OFF_K, OFF_V, OFF_O, OFF_ZB = (i * 1024 for i in range(7))

PROMPT_TILE = 256
STEP_TILES = 2
SAMPLE_GROUP = 8
VMEM_LIMIT = 60000 * 1024

F32 = jnp.float32
BF16 = jnp.bfloat16

LayerW = collections.namedtuple("LayerW", "main gate bias norm1 pool pscale mhln out normf")


def _dot(a, b):
    return jnp.dot(a, b, preferred_element_type=F32)


def _dot_nt(a, b):
    return lax.dot_general(a, b, (((1,), (1,)), ((), ())), preferred_element_type=F32)


def _dot_tn(a, b):
    return lax.dot_general(a, b, (((0,), (0,)), ((), ())), preferred_element_type=F32)


def _rms(x, w):
    return x * lax.rsqrt(jnp.mean(x * x, axis=-1, keepdims=True) + EPS) * w


def _log_sigmoid(x):
    return jnp.minimum(x, 0.0) - jnp.log1p(jnp.exp(-jnp.abs(x)))


def _silu(x):
    return x * jax.nn.sigmoid(x)


def _split3(x):
    hi = x.astype(BF16)
    r = x - hi.astype(F32)
    mid = r.astype(BF16)
    lo = (r - mid.astype(F32)).astype(BF16)
    return hi, mid, lo


def _seq_mask(T, seq_len):
    row = lax.broadcasted_iota(jnp.int32, (T, T), 0)
    col = lax.broadcasted_iota(jnp.int32, (T, T), 1)
    causal = col <= row
    if seq_len < T:
        shift = seq_len.bit_length() - 1
        assert 1 << shift == seq_len
        causal = causal & ((row >> shift) == (col >> shift))
    return causal


def _to_rows(cols):
    T = cols.shape[0]
    pad = -T % 128
    if pad:
        cols = jnp.concatenate([cols, jnp.zeros((pad, cols.shape[1]), cols.dtype)], axis=0)
    return cols.T[:, 0:T]


def _gate_pre(xn, w):
    return _dot(xn, w.gate[...]) + w.bias[...]


def _cumsum_operands(g_col, causal):
    tri = jnp.where(causal, 1.0, 0.0).astype(BF16)
    return tri, _split3(_log_sigmoid(g_col))


def _cumsum(tri, parts):
    return sum(_dot(tri, p) for p in parts)


def _intra_weights(qb, kb, b_col, a_col, r_row, causal):
    dm = jnp.where(causal, b_col + r_row, -jnp.inf)
    m_t = jnp.maximum(a_col, jnp.max(dm, axis=-1, keepdims=True))
    w = jnp.exp(dm - m_t)
    inter = jnp.exp(a_col - m_t)
    return m_t, inter, _dot_nt(qb, kb) * (w * K_SCALE)


def _head_norm(ht, w_row):
    mu = jnp.mean(ht, axis=-1, keepdims=True)
    d = ht - mu
    var = jnp.mean(d * d, axis=-1, keepdims=True)
    return d * lax.rsqrt(var + EPS) * w_row


def _window_sums(ext):
    s2 = ext + pltpu.roll(ext, 1, axis=0)
    s4 = s2[:, 256:] + pltpu.roll(s2[:, 256:], 2, axis=0)
    s8 = s4[:, 256:] + pltpu.roll(s4[:, 256:], 4, axis=0)
    s16 = s8[:, 256:] + pltpu.roll(s8[:, 256:], 8, axis=0)
    return [s2[HIST:, 0:256], s4[HIST:, 0:256], s8[HIST:, 0:256], s16[HIST:, 0:256]]


def _pooled(ext, u, pos_col):
    sums = _window_sums(ext)
    return [sums[g] * (1.0 / jnp.minimum(float(w), pos_col + 1.0)) - u[:, g * 256:(g + 1) * 256]
            for g, w in enumerate(POOL_WINDOWS)]


def _pool_mix(pooled, wpool_ref):
    return jnp.concatenate([_dot(p.astype(BF16), wpool_ref[g]) for g, p in enumerate(pooled)], axis=-1)


def _inproj_steps(x_ref, scr, w):
    proj_scr, x_scr, gcol_scr, _, xn_scr = scr

    def norm():
        x = x_ref[...]
        x_scr[...] = x
        xn_scr[...] = _rms(x, w.norm1[...]).astype(BF16)

    def piece(c0):
        act = {OFF_ZA: _silu, OFF_O: jax.nn.sigmoid, OFF_ZB: _silu}.get(c0 // 1024 * 1024, lambda p: p)

        def run():
            proj_scr[:, c0:c0 + PIECE_COLS] = act(_dot(xn_scr[...], w.main[:, c0:c0 + PIECE_COLS]))
        return run

    def gates():
        gcol_scr[...] = _gate_pre(xn_scr[...], w)

    return [norm] + [piece(c0) for c0 in range(0, D_MAIN, PIECE_COLS)] + [gates]


def _mid_steps(scr, pos0, w, state):
    proj_scr, x_scr, gcol_scr, ycat_scr, _ = scr
    hist_scr, c_scr, n_scr, m_scr = state
    T = x_scr.shape[0]

    def seg(off, h=None):
        if h is None:
            return proj_scr[:, off:off + 1024]
        return proj_scr[:, off + h * 256:off + (h + 1) * 256]

    causal = _seq_mask(T, T)
    g_col = gcol_scr[...]
    tri, lf_parts = _cumsum_operands(g_col, causal)
    yield
    b_col_all = _cumsum(tri, lf_parts)
    r_row_all = _to_rows(g_col - pltpu.roll(b_col_all, GATE_PAD - N_HEADS, axis=1))

    u = seg(OFF_U)
    ext = jnp.concatenate([hist_scr[...], u], axis=0)
    pos_col = (lax.broadcasted_iota(jnp.int32, (T, 1), 0) + pos0).astype(F32)
    pooled = _pooled(ext, u, pos_col)
    hist_scr[...] = ext[T:T + HIST, :]
    yield
    mixed = _pool_mix(pooled, w.pool)
    ycat_scr[:, 0:1024] = (mixed * w.pscale[...] * seg(OFF_ZA)).astype(BF16)

    def head(h):
        q, k, v = seg(OFF_Q, h), seg(OFF_K, h), seg(OFF_V, h)
        qb, kb, vb = q.astype(BF16), k.astype(BF16), v.astype(BF16)
        ig_col = g_col[:, h:h + 1]
        b_col = b_col_all[:, 4 + h:5 + h]
        r_row = r_row_all[h:h + 1, :]
        m_prev = m_scr[h:h + 1, 0:1]
        a_col = b_col + m_prev
        c_old = c_scr[h]
        n_old = n_scr[h:h + 1, :]
        q_c = _dot(qb, c_old.astype(BF16))
        m_t, inter, s = _intra_weights(qb, kb, b_col, a_col, r_row, causal)
        sb = s.astype(BF16)
        yield
        num = inter * q_c + _dot(sb, vb)
        qn = inter * jnp.sum(q * n_old, axis=-1, keepdims=True) + jnp.sum(s, axis=-1, keepdims=True)
        ht = num * (1.0 / jnp.maximum(jnp.abs(qn), jnp.exp(-m_t)))
        m_new = m_t[T - 1:T, :]
        b_last = b_col[T - 1:T, :]
        w_end = jnp.exp(b_last - b_col + ig_col - m_new) * K_SCALE
        decay = jnp.exp(b_last + m_prev - m_new)
        kw = k * w_end
        kwb = kw.astype(BF16)
        yield
        c_scr[h] = decay * c_old + _dot_tn(kwb, vb)
        n_scr[h:h + 1, :] = decay * n_old + jnp.sum(kw, axis=0, keepdims=True)
        m_scr[h:h + 1, :] = jnp.broadcast_to(m_new, (1, 128))
        hn = _head_norm(ht, w.mhln[:, h * 256:(h + 1) * 256])
        ycat_scr[:, 1024 + h * 256:1024 + (h + 1) * 256] = (hn * seg(OFF_O, h) * seg(OFF_ZB, h)).astype(BF16)

    for pair in range(0, N_HEADS, HEADS_IN_FLIGHT):
        running = [head(h) for h in range(pair, pair + HEADS_IN_FLIGHT)]
        while running:
            alive = []
            for g in running:
                if next(g, g) is not g:
                    alive.append(g)
                yield
            running = alive


def _out_steps(scr, w, y_ref):
    _, x_scr, _, ycat_scr, _ = scr

    def residual():
        y_ref[...] = x_scr[...]

    def piece(c0):
        def run():
            cols = slice(c0, c0 + OUT_COLS)
            y_ref[:, cols] = y_ref[:, cols] + _dot(ycat_scr[...], w.out[:, cols])
        return run

    def norm():
        y_ref[...] = _rms(y_ref[...], w.normf[...])

    return [residual] + [piece(c0) for c0 in range(0, D_MODEL, OUT_COLS)] + [norm]


PIECE_COLS = 512
OUT_COLS = 256
HEADS_IN_FLIGHT = 2
MID_YIELDS = 2 + 3 * N_HEADS


def _run_interleaved(mid, lead, at_yield):
    assert len(at_yield) == MID_YIELDS
    for piece in lead:
        piece()
    for pieces in at_yield:
        next(mid)
        for piece in pieces:
            piece()
    for _ in mid:
        raise AssertionError("unexpected extra yield")


def _spread(pieces, yields):
    base, extra = divmod(len(pieces), yields)
    groups, start = [], 0
    for i in range(yields):
        stop = start + base + (i < extra)
        groups.append(pieces[start:stop])
        start = stop
    return groups


def _m_row(m_scr):
    lane = lax.broadcasted_iota(jnp.int32, (1, 128), 1)
    row = jnp.zeros((1, 128), F32)
    for h in range(N_HEADS):
        row = jnp.where(lane == h, m_scr[h:h + 1, :], row)
    return row


def _prompt_kernel(meta_ref, x0_ref, *refs, steps_per_seq):
    x_next = refs[:STEP_TILES]
    refs = refs[STEP_TILES:]
    w = LayerW(*refs[:9])
    y_ref, pool_out, c_out, n_out, m_out = refs[9:14]
    scr = (refs[14:19], refs[19:24])
    scr0, scr_meta = scr[0], refs[24:29]
    state, state_meta = refs[29:33], refs[33:37]
    hist_scr, c_scr, n_scr, m_scr = state
    s = pl.program_id(0)
    TT = PROMPT_TILE

    @pl.when(s == 0)
    def _():
        for ref in state_meta:
            ref[...] = jnp.zeros_like(ref)
        for p in _inproj_steps(meta_ref, scr_meta, w):
            p()
        for _ in _mid_steps(scr_meta, 0, w, state_meta):
            pass
        for p in _inproj_steps(x0_ref, scr0, w):
            p()

    @pl.when(s % steps_per_seq == 0)
    def _():
        for ref, ref_meta in zip(state, state_meta):
            ref[...] = ref_meta[...]

    out_pieces = []
    for j in range(STEP_TILES):
        norm, *proj = _inproj_steps(x_next[j], scr[1 - j % 2], w)
        if out_pieces:
            residual, o0, o1, o2, o3, out_norm = out_pieces
            lead = [residual, o0, norm, o1]
            at_yield = [[o2], [o3]] + _spread(proj, MID_YIELDS - 2)
            at_yield[3] = at_yield[3] + [out_norm]
        else:
            lead = [norm, proj[0]]
            at_yield = _spread(proj[1:], MID_YIELDS)
        _run_interleaved(_mid_steps(scr[j % 2], N_META, w, state), lead, at_yield)
        out_pieces = _out_steps(scr[j % 2], w, y_ref.at[j * TT:(j + 1) * TT, :])
    for piece in out_pieces:
        piece()

    @pl.when(s % steps_per_seq == steps_per_seq - 1)
    def _():
        b = s // steps_per_seq
        for j in range(POOL_BUF):
            pool_out[0, j, pl.ds(b, 1), :] = hist_scr[1 + j:2 + j, :]
        c_out[0, 0] = c_scr[...]
        n_out[0, 0] = n_scr[0:N_HEADS, :]
        m_out[pl.ds(b, 1), :] = _m_row(m_scr)


def _prep_kernel(wt_ref, wgt_ref, xs_ref, norm1_ref, wout_ref, wmain_ref, wgate_ref, proj_ref, woutb_ref,
                 xn_scr, *, n_seg):
    i = pl.program_id(0)

    @pl.when(i == 0)
    def _():
        g = jnp.concatenate([wgt_ref[...], jnp.zeros((GATE_PAD - 8, D_MODEL), F32)], axis=0)
        wgate_ref[...] = g.T.astype(BF16)
        xn_scr[...] = _rms(xs_ref[...], norm1_ref[...]).astype(BF16)

    woutb_ref[...] = wout_ref[...].astype(BF16)

    @pl.when(i < n_seg)
    def _():
        wb = wt_ref[...].T.astype(BF16)
        wmain_ref[...] = wb
        proj_ref[...] = _dot(xn_scr[...], wb)


def _expand(seq_col, vals):
    out = None
    for i, val in enumerate(vals):
        pick = jnp.where(seq_col == i, val, 0.0)
        out = pick if out is None else out + pick
    return out


def _sample_kernel(x_ref, proj_ref, pool_ref, c_ref, n_ref, m_ref, *refs, seq_len, pos0):
    w = LayerW(None, *refs[:8])
    y_ref, pool_out, c_out, n_out, m_out = refs[8:13]
    ext_scr, ycat_scr = refs[13:15]
    G = SAMPLE_GROUP
    T = G * seq_len
    x = x_ref[...]
    xn = _rms(x, w.norm1[...]).astype(BF16)
    causal = _seq_mask(T, seq_len)
    g_col = _gate_pre(xn, w)
    b_col_all = _cumsum(*_cumsum_operands(g_col, causal))
    r_row_all = _to_rows(g_col - pltpu.roll(b_col_all, GATE_PAD - N_HEADS, axis=1))
    seq_col = lax.broadcasted_iota(jnp.int32, (T, 1), 0) >> (seq_len.bit_length() - 1)
    pos_col = jnp.full((seq_len, 1), float(pos0), F32) + lax.broadcasted_iota(
        jnp.int32, (seq_len, 1), 0).astype(F32)

    pooled_rows = []
    for i in range(G):
        rows = slice(i * seq_len, (i + 1) * seq_len)
        u_i = proj_ref[rows, OFF_U:OFF_U + 1024]
        ext_scr[0:1, :] = jnp.zeros((1, 1024), F32)
        for j in range(POOL_BUF):
            ext_scr[1 + j:2 + j, :] = pool_ref[0, j, i:i + 1, :]
        ext_scr[HIST:HIST + seq_len, :] = u_i
        ext = ext_scr[...]
        pooled_rows.append(_pooled(ext, u_i, pos_col))
        for j in range(POOL_BUF):
            pool_out[0, j, i:i + 1, :] = ext[seq_len + 1 + j:seq_len + 2 + j, :]
    mixed = _pool_mix([jnp.concatenate(p, axis=0) for p in zip(*pooled_rows)], w.pool)
    y_a = mixed * w.pscale[...] * _silu(proj_ref[:, OFF_ZA:OFF_ZA + 1024])
    ycat_scr[:, 0:1024] = y_a.astype(BF16)

    lane4 = lax.broadcasted_iota(jnp.int32, (1, N_HEADS), 1)
    last = [(i + 1) * seq_len - 1 for i in range(G)]
    m_new_heads = [None] * N_HEADS

    def head(h):
        hc = slice(h * 256, (h + 1) * 256)
        q = proj_ref[:, OFF_Q + h * 256:OFF_Q + (h + 1) * 256]
        k = proj_ref[:, OFF_K + h * 256:OFF_K + (h + 1) * 256]
        v = proj_ref[:, OFF_V + h * 256:OFF_V + (h + 1) * 256]
        qb, kb, vb = q.astype(BF16), k.astype(BF16), v.astype(BF16)
        c_old = [c_ref[0, i, h] for i in range(G)]
        c_old_b = [c.astype(BF16) for c in c_old]
        yield
        q_cs = [_dot(qb, cb) for cb in c_old_b]
        ig_col = g_col[:, h:h + 1]
        b_col = b_col_all[:, 4 + h:5 + h]
        r_row = r_row_all[h:h + 1, :]
        m_prev = [m_ref[0, i:i + 1, h:h + 1] for i in range(G)]
        a_col = b_col + _expand(seq_col, m_prev)
        m_t, inter, s = _intra_weights(qb, kb, b_col, a_col, r_row, causal)
        sb = s.astype(BF16)
        yield
        sv = _dot(sb, vb)
        n_old = [n_ref[0, i, h:h + 1, :] for i in range(G)]
        num = inter * _expand(seq_col, q_cs) + sv
        qn = inter * jnp.sum(q * _expand(seq_col, n_old), axis=-1, keepdims=True) + jnp.sum(
            s, axis=-1, keepdims=True)
        ht = num * (1.0 / jnp.maximum(jnp.abs(qn), jnp.exp(-m_t)))
        m_new = [m_t[r:r + 1, :] for r in last]
        b_last = [b_col[r:r + 1, :] for r in last]
        w_end = jnp.exp(_expand(seq_col, b_last) - b_col + ig_col - _expand(seq_col, m_new)) * K_SCALE
        kw = k * w_end
        kwb = kw.astype(BF16)
        v_seq = [jnp.where(seq_col == i, vb, jnp.zeros_like(vb)) for i in range(G)]
        yield
        updates = [_dot_tn(kwb, v_i) for v_i in v_seq]
        for i in range(G):
            decay = jnp.exp(b_last[i] + m_prev[i] - m_new[i])
            c_out[0, i, h] = decay * c_old[i] + updates[i]
            n_out[0, i, h:h + 1, :] = decay * n_old[i] + jnp.sum(
                kw[i * seq_len:(i + 1) * seq_len, :], axis=0, keepdims=True)
        m_new_heads[h] = m_new
        hn = _head_norm(ht, w.mhln[:, hc])
        o = proj_ref[:, OFF_O + h * 256:OFF_O + (h + 1) * 256]
        zb = proj_ref[:, OFF_ZB + h * 256:OFF_ZB + (h + 1) * 256]
        ycat_scr[:, 1024 + h * 256:1024 + (h + 1) * 256] = (hn * jax.nn.sigmoid(o) * _silu(zb)).astype(BF16)

    running = [head(h) for h in range(N_HEADS)]
    while running:
        running = [g for g in running if next(g, g) is not g]
    for i in range(G):
        row = jnp.zeros((1, N_HEADS), F32)
        for h in range(N_HEADS):
            row = jnp.where(lane4 == h, m_new_heads[h][i], row)
        m_out[0, i:i + 1, :] = row

    y = _dot(ycat_scr[...], w.out[...])
    y_ref[...] = _rms(x + y, w.normf[...])


def _const_spec(shape):
    nd = len(shape)
    return pl.BlockSpec(shape, lambda *_: (0,) * nd, pipeline_mode=pl.Buffered(1))


def _params(sem):
    return pltpu.CompilerParams(dimension_semantics=sem, vmem_limit_bytes=VMEM_LIMIT)


def _tile_scratch(T):
    return [pltpu.VMEM((T, D_MAIN), F32), pltpu.VMEM((T, D_MODEL), F32), pltpu.VMEM((T, GATE_PAD), F32),
            pltpu.VMEM((T, 2048), BF16), pltpu.VMEM((T, D_MODEL), BF16)]


def _state_scratch():
    return [pltpu.VMEM((HIST, D_POOL), F32), pltpu.VMEM((N_HEADS, HEAD_DIM, HEAD_DIM), F32),
            pltpu.VMEM((8, HEAD_DIM), F32), pltpu.VMEM((8, 128), F32)]


def kernel(x_prompt, x_sample, state_pool, state_C, state_n, state_m, meta_tokens, norm1_w, w_in,
           b_if, w_pool, pool_scale, mhln_w, w_out, normf_w):
    B, S, _ = x_prompt.shape
    SB, SL, _ = x_sample.shape
    TT = PROMPT_TILE
    assert norm1_w.shape[0] == 1, "single layer"
    assert STEP_TILES % 2 == 0 and S % (STEP_TILES * TT) == 0 and SB % SAMPLE_GROUP == 0

    w_in_t = jnp.swapaxes(w_in[0], 0, 1)
    norm1 = norm1_w[0].reshape(1, D_MODEL)
    n_tok = SB * SL
    xs = x_sample.reshape(n_tok, D_MODEL)
    n_seg = D_MAIN // 1024
    wout_rows = w_out.shape[1] // (n_seg + 1)
    seg = lambda i: jnp.minimum(i, n_seg - 1)
    w_main, w_gate, proj_s, wout = pl.pallas_call(
        functools.partial(_prep_kernel, n_seg=n_seg),
        grid=(n_seg + 1,),
        in_specs=[pl.BlockSpec((1024, D_MODEL), lambda i: (seg(i), 0)),
                  pl.BlockSpec((8, D_MODEL), lambda i: (D_MAIN // 8, 0)),
                  _const_spec(xs.shape), _const_spec(norm1.shape),
                  pl.BlockSpec((wout_rows, D_MODEL), lambda i: (i, 0))],
        out_specs=(pl.BlockSpec((D_MODEL, 1024), lambda i: (0, seg(i))),
                   pl.BlockSpec((D_MODEL, GATE_PAD), lambda i: (0, 0)),
                   pl.BlockSpec((n_tok, 1024), lambda i: (0, seg(i))),
                   pl.BlockSpec((wout_rows, D_MODEL), lambda i: (i, 0))),
        out_shape=(jax.ShapeDtypeStruct((D_MODEL, D_MAIN), BF16),
                   jax.ShapeDtypeStruct((D_MODEL, GATE_PAD), BF16),
                   jax.ShapeDtypeStruct((n_tok, D_MAIN), F32),
                   jax.ShapeDtypeStruct(w_out.shape[1:], BF16)),
        scratch_shapes=[pltpu.VMEM((n_tok, D_MODEL), BF16)],
        compiler_params=_params(("arbitrary",)),
        name="prep",
    )(w_in_t, w_in_t, xs, norm1, w_out[0])
    bias_row = jnp.pad(b_if[0], (0, GATE_PAD - 8)).reshape(1, GATE_PAD)
    wpool = w_pool[0].astype(BF16)
    pscale = pool_scale[0].reshape(1, D_POOL)
    mhln = mhln_w[0].reshape(1, D_MLSTM)
    normf = normf_w.reshape(1, D_MODEL)
    layer_w = LayerW(w_main, w_gate, bias_row, norm1, wpool, pscale, mhln, wout, normf)
    layer_specs = [_const_spec(a.shape) for a in layer_w]

    n_tiles = B * S // TT
    steps = n_tiles // STEP_TILES
    steps_per_seq = S // (STEP_TILES * TT)
    xp = x_prompt.reshape(B * S, D_MODEL)
    y_p, pool_p, c_p, n_p, m_p = pl.pallas_call(
        functools.partial(_prompt_kernel, steps_per_seq=steps_per_seq),
        grid=(steps,),
        in_specs=[_const_spec(meta_tokens.shape),
                  pl.BlockSpec((TT, D_MODEL), lambda s: (0, 0))]
                 + [pl.BlockSpec((TT, D_MODEL),
                                 lambda s, j=j: (jnp.minimum(STEP_TILES * s + 1 + j, n_tiles - 1), 0))
                    for j in range(STEP_TILES)]
                 + layer_specs,
        out_specs=(pl.BlockSpec((STEP_TILES * TT, D_MODEL), lambda s: (s, 0)),
                   pl.BlockSpec((1, POOL_BUF, B, D_POOL), lambda s: (0, 0, 0, 0)),
                   pl.BlockSpec((1, 1, N_HEADS, HEAD_DIM, HEAD_DIM),
                                lambda s: (0, s // steps_per_seq, 0, 0, 0)),
                   pl.BlockSpec((1, 1, N_HEADS, HEAD_DIM), lambda s: (0, s // steps_per_seq, 0, 0)),
                   pl.BlockSpec((B, 128), lambda s: (0, 0))),
        out_shape=(jax.ShapeDtypeStruct((B * S, D_MODEL), F32),
                   jax.ShapeDtypeStruct((1, POOL_BUF, B, D_POOL), F32),
                   jax.ShapeDtypeStruct((1, B, N_HEADS, HEAD_DIM, HEAD_DIM), F32),
                   jax.ShapeDtypeStruct((1, B, N_HEADS, HEAD_DIM), F32),
                   jax.ShapeDtypeStruct((B, 128), F32)),
        scratch_shapes=_tile_scratch(TT) + _tile_scratch(TT) + _tile_scratch(N_META)
        + _state_scratch() + _state_scratch(),
        compiler_params=_params(("arbitrary",)),
        name="prompt",
    )(meta_tokens, xp, *([xp] * STEP_TILES), *layer_w)
    y_prompt = y_p.reshape(B, S, D_MODEL)
    pool_p = jnp.swapaxes(pool_p, 1, 2)
    m_p = m_p[:, :N_HEADS].reshape(1, B, N_HEADS)

    G = SAMPLE_GROUP
    GT = G * SL
    pool_in = jnp.swapaxes(state_pool, 1, 2)
    sample_w = layer_w[1:]
    y_s, pool_s, c_s, n_s, m_s = pl.pallas_call(
        functools.partial(_sample_kernel, seq_len=SL, pos0=PAST_LEN),
        grid=(SB // G,),
        in_specs=[pl.BlockSpec((GT, D_MODEL), lambda i: (i, 0)),
                  pl.BlockSpec((GT, D_MAIN), lambda i: (i, 0)),
                  pl.BlockSpec((1, POOL_BUF, G, D_POOL), lambda i: (0, 0, i, 0)),
                  pl.BlockSpec((1, G, N_HEADS, HEAD_DIM, HEAD_DIM), lambda i: (0, i, 0, 0, 0)),
                  pl.BlockSpec((1, G, N_HEADS, HEAD_DIM), lambda i: (0, i, 0, 0)),
                  pl.BlockSpec((1, G, N_HEADS), lambda i: (0, i, 0))]
                 + [_const_spec(a.shape) for a in sample_w],
        out_specs=(pl.BlockSpec((GT, D_MODEL), lambda i: (i, 0)),
                   pl.BlockSpec((1, POOL_BUF, G, D_POOL), lambda i: (0, 0, i, 0)),
                   pl.BlockSpec((1, G, N_HEADS, HEAD_DIM, HEAD_DIM), lambda i: (0, i, 0, 0, 0)),
                   pl.BlockSpec((1, G, N_HEADS, HEAD_DIM), lambda i: (0, i, 0, 0)),
                   pl.BlockSpec((1, G, N_HEADS), lambda i: (0, i, 0))),
        out_shape=(jax.ShapeDtypeStruct((n_tok, D_MODEL), F32),
                   jax.ShapeDtypeStruct(pool_in.shape, F32),
                   jax.ShapeDtypeStruct(state_C.shape, F32),
                   jax.ShapeDtypeStruct(state_n.shape, F32),
                   jax.ShapeDtypeStruct(state_m.shape, F32)),
        scratch_shapes=[pltpu.VMEM((HIST + SL, D_POOL), F32), pltpu.VMEM((GT, 2048), BF16)],
        compiler_params=_params(("arbitrary",)),
        name="sample",
    )(xs, proj_s, pool_in, state_C, state_n, state_m, *sample_w)
    y_sample = y_s.reshape(SB, SL, D_MODEL)
    pool_s = jnp.swapaxes(pool_s, 1, 2)

    return (y_prompt, y_sample, pool_p, c_p, n_p, m_p, pool_s, c_s, n_s, m_s)
```

```python
import collections
import functools

import jax
import jax.numpy as jnp
from jax import lax
from jax.experimental import pallas as pl
from jax.experimental.pallas import tpu as pltpu

D_MODEL = 1024
D_POOL = 1024
D_MLSTM = 1024
N_HEADS = 4
HEAD_DIM = 256
POOL_WINDOWS = (2, 4, 8, 16)
POOL_BUF = 15
HIST = 16
N_META = 16
PAST_LEN = 16384
EPS = 1e-6
D_MAIN = 2 * D_POOL + 5 * D_MLSTM
GATE_PAD = 128
K_SCALE = HEAD_DIM ** -0.5

OFF_U, OFF_ZA, OFF_Q, OFF_K, OFF_V, OFF_O, OFF_ZB = (i * 1024 for i in range(7))
TILE_F32_COLS, TILE_BF16_COLS = 3072, 4096
TILE_SEG = {OFF_U: (0, 0, None), OFF_Q: (0, 1024, None), OFF_K: (0, 2048, None),
            OFF_ZA: (1, 0, "silu"), OFF_V: (1, 1024, None), OFF_O: (1, 2048, "sigmoid"), OFF_ZB: (1, 3072, "silu")}

PROMPT_TILE = 256
STEP_TILES = 4
SAMPLE_GROUP = 8
VMEM_LIMIT = 62 * 1024 * 1024

F32 = jnp.float32
BF16 = jnp.bfloat16

LayerW = collections.namedtuple("LayerW", "main gate bias norm1 pool pscale mhln out normf")


def _dot(a, b):
    return jnp.dot(a, b, preferred_element_type=F32)


def _dot_nt(a, b):
    return lax.dot_general(a, b, (((1,), (1,)), ((), ())), preferred_element_type=F32)


def _dot_tn(a, b):
    return lax.dot_general(a, b, (((0,), (0,)), ((), ())), preferred_element_type=F32)


def _rms(x, w):
    return x * lax.rsqrt(jnp.mean(x * x, axis=-1, keepdims=True) + EPS) * w


def _log_sigmoid(x):
    return jnp.minimum(x, 0.0) - jnp.log1p(jnp.exp(-jnp.abs(x)))


def _silu(x):
    return x * jax.nn.sigmoid(x)


def _split3(x):
    hi = x.astype(BF16)
    r = x - hi.astype(F32)
    mid = r.astype(BF16)
    lo = (r - mid.astype(F32)).astype(BF16)
    return hi, mid, lo


def _seq_mask(T, seq_len):
    row = lax.broadcasted_iota(jnp.int32, (T, T), 0)
    col = lax.broadcasted_iota(jnp.int32, (T, T), 1)
    causal = col <= row
    if seq_len < T:
        shift = seq_len.bit_length() - 1
        assert 1 << shift == seq_len
        causal = causal & ((row >> shift) == (col >> shift))
    return causal


def _to_rows(cols):
    T = cols.shape[0]
    pad = -T % 128
    if pad:
        cols = jnp.concatenate([cols, jnp.zeros((pad, cols.shape[1]), cols.dtype)], axis=0)
    return cols.T[:, 0:T]


def _gate_pre(xn, w):
    return _dot(xn, w.gate[...]) + w.bias[...]


def _cumsum_operands(g_col, causal):
    tri = jnp.where(causal, 1.0, 0.0).astype(BF16)
    return tri, _split3(_log_sigmoid(g_col))


def _cumsum(tri, parts):
    return sum(_dot(tri, p) for p in parts)


def _intra_weights(qb, kb, b_col, a_col, r_row, causal):
    dm = jnp.where(causal, b_col + r_row, -jnp.inf)
    m_t = jnp.maximum(a_col, jnp.max(dm, axis=-1, keepdims=True))
    w = jnp.exp(dm - m_t)
    inter = jnp.exp(a_col - m_t)
    return m_t, inter, _dot_nt(qb, kb) * (w * K_SCALE)


def _head_norm(ht, w_row):
    mu = jnp.mean(ht, axis=-1, keepdims=True)
    d = ht - mu
    var = jnp.mean(d * d, axis=-1, keepdims=True)
    return d * lax.rsqrt(var + EPS) * w_row


def _window_sums(ext):
    s2 = ext + pltpu.roll(ext, 1, axis=0)
    s4 = s2[:, 256:] + pltpu.roll(s2[:, 256:], 2, axis=0)
    s8 = s4[:, 256:] + pltpu.roll(s4[:, 256:], 4, axis=0)
    s16 = s8[:, 256:] + pltpu.roll(s8[:, 256:], 8, axis=0)
    return [s2[HIST:, 0:256], s4[HIST:, 0:256], s8[HIST:, 0:256], s16[HIST:, 0:256]]


def _pooled(ext, u, pos_col):
    sums = _window_sums(ext)
    return [sums[g] * (1.0 / jnp.minimum(float(w), pos_col + 1.0)) - u[:, g * 256:(g + 1) * 256]
            for g, w in enumerate(POOL_WINDOWS)]


def _pool_mix(pooled, wpool_ref):
    return jnp.concatenate([_dot(p.astype(BF16), wpool_ref[g]) for g, p in enumerate(pooled)], axis=-1)


def _inproj_steps(x_ref, scr, w):
    proj32_scr, x_scr, gcol_scr, _, xn_scr, proj16_scr = scr

    def norm():
        x = x_ref[...]
        x_scr[...] = x
        xn_scr[...] = _rms(x, w.norm1[...]).astype(BF16)

    def piece(c0):
        seg0 = c0 // 1024 * 1024
        buf, off, act = TILE_SEG[seg0]
        dst = (proj32_scr, proj16_scr)[buf]
        cols = slice(off + c0 - seg0, off + c0 - seg0 + PIECE_COLS)
        act = {None: lambda p: p, "silu": _silu, "sigmoid": jax.nn.sigmoid}[act]

        def run():
            dst[:, cols] = act(_dot(xn_scr[...], w.main[:, c0:c0 + PIECE_COLS])).astype(dst.dtype)
        return run

    def gates():
        gcol_scr[...] = _gate_pre(xn_scr[...], w)

    return [norm] + [piece(c0) for c0 in range(0, D_MAIN, PIECE_COLS)] + [gates]


def _mid_steps(scr, pos0, w, state):
    proj32_scr, x_scr, gcol_scr, ycat_scr, _, proj16_scr = scr
    hist_scr, c_scr, n_scr, m_scr = state
    T = x_scr.shape[0]

    def seg(seg0, h=None):
        buf, off, _ = TILE_SEG[seg0]
        src = (proj32_scr, proj16_scr)[buf]
        if h is None:
            return src[:, off:off + 1024]
        return src[:, off + h * 256:off + (h + 1) * 256]

    causal = _seq_mask(T, T)
    g_col = gcol_scr[...]
    tri, lf_parts = _cumsum_operands(g_col, causal)
    yield
    b_col_all = _cumsum(tri, lf_parts)
    r_row_all = _to_rows(g_col - pltpu.roll(b_col_all, GATE_PAD - N_HEADS, axis=1))

    u = seg(OFF_U)
    ext = jnp.concatenate([hist_scr[...], u], axis=0)
    pos_col = (lax.broadcasted_iota(jnp.int32, (T, 1), 0) + pos0).astype(F32)
    pooled = _pooled(ext, u, pos_col)
    hist_scr[...] = ext[T:T + HIST, :]
    yield
    mixed = _pool_mix(pooled, w.pool)
    ycat_scr[:, 0:1024] = (mixed * w.pscale[...] * seg(OFF_ZA).astype(F32)).astype(BF16)

    def head(h):
        q, k, vb = seg(OFF_Q, h), seg(OFF_K, h), seg(OFF_V, h)
        qb, kb = q.astype(BF16), k.astype(BF16)
        ig_col = g_col[:, h:h + 1]
        b_col = b_col_all[:, 4 + h:5 + h]
        r_row = r_row_all[h:h + 1, :]
        m_prev = m_scr[h:h + 1, 0:1]
        a_col = b_col + m_prev
        c_old = c_scr[h]
        n_old = n_scr[h:h + 1, :]
        q_c = _dot(qb, c_old.astype(BF16))
        m_t, inter, s = _intra_weights(qb, kb, b_col, a_col, r_row, causal)
        sb = s.astype(BF16)
        yield
        num = inter * q_c + _dot(sb, vb)
        qn = inter * jnp.sum(q * n_old, axis=-1, keepdims=True) + jnp.sum(s, axis=-1, keepdims=True)
        ht = num * (1.0 / jnp.maximum(jnp.abs(qn), jnp.exp(-m_t)))
        m_new = m_t[T - 1:T, :]
        b_last = b_col[T - 1:T, :]
        w_end = jnp.exp(b_last - b_col + ig_col - m_new) * K_SCALE
        decay = jnp.exp(b_last + m_prev - m_new)
        kw = k * w_end
        kwb = kw.astype(BF16)
        yield
        c_scr[h] = decay * c_old + _dot_tn(kwb, vb)
        n_scr[h:h + 1, :] = decay * n_old + jnp.sum(kw, axis=0, keepdims=True)
        m_scr[h:h + 1, :] = jnp.broadcast_to(m_new, (1, 128))
        hn = _head_norm(ht, w.mhln[:, h * 256:(h + 1) * 256])
        gate = seg(OFF_O, h).astype(F32) * seg(OFF_ZB, h).astype(F32)
        ycat_scr[:, 1024 + h * 256:1024 + (h + 1) * 256] = (hn * gate).astype(BF16)

    for pair in range(0, N_HEADS, HEADS_IN_FLIGHT):
        running = [head(h) for h in range(pair, pair + HEADS_IN_FLIGHT)]
        while running:
            alive = []
            for g in running:
                if next(g, g) is not g:
                    alive.append(g)
                yield
            running = alive


def _out_steps(scr, w, y_ref):
    _, x_scr, _, ycat_scr = scr[:4]

    def residual():
        y_ref[...] = x_scr[...]

    def piece(c0):
        def run():
            cols = slice(c0, c0 + OUT_COLS)
            y_ref[:, cols] = y_ref[:, cols] + _dot(ycat_scr[...], w.out[:, cols])
        return run

    def norm():
        y_ref[...] = _rms(y_ref[...], w.normf[...])

    return [residual] + [piece(c0) for c0 in range(0, D_MODEL, OUT_COLS)] + [norm]


PIECE_COLS = 512
OUT_COLS = 256
HEADS_IN_FLIGHT = 2
MID_YIELDS = 2 + 3 * N_HEADS


def _run_interleaved(mid, lead, at_yield):
    assert len(at_yield) == MID_YIELDS
    for piece in lead:
        piece()
    for pieces in at_yield:
        next(mid)
        for piece in pieces:
            piece()
    for _ in mid:
        raise AssertionError("unexpected extra yield")


def _spread(pieces, yields):
    base, extra = divmod(len(pieces), yields)
    groups, start = [], 0
    for i in range(yields):
        stop = start + base + (i < extra)
        groups.append(pieces[start:stop])
        start = stop
    return groups


def _m_row(m_scr):
    lane = lax.broadcasted_iota(jnp.int32, (1, 128), 1)
    row = jnp.zeros((1, 128), F32)
    for h in range(N_HEADS):
        row = jnp.where(lane == h, m_scr[h:h + 1, :], row)
    return row


def _prompt_kernel(meta_ref, x0_ref, *refs, steps_per_seq):
    x_next = refs[:STEP_TILES]
    refs = refs[STEP_TILES:]
    w = LayerW(*refs[:9])
    y_ref, pool_out, c_out, n_out, m_out = refs[9:14]
    scr = (refs[14:20], refs[20:26])
    scr0, scr_meta = scr[0], refs[26:32]
    state, state_meta = refs[32:36], refs[36:40]
    hist_scr, c_scr, n_scr, m_scr = state
    s = pl.program_id(0)
    TT = PROMPT_TILE

    @pl.when(s == 0)
    def _():
        for ref in state_meta:
            ref[...] = jnp.zeros_like(ref)
        for p in _inproj_steps(meta_ref, scr_meta, w):
            p()
        for _ in _mid_steps(scr_meta, 0, w, state_meta):
            pass
        for p in _inproj_steps(x0_ref, scr0, w):
            p()

    @pl.when(s % steps_per_seq == 0)
    def _():
        for ref, ref_meta in zip(state, state_meta):
            ref[...] = ref_meta[...]

    out_pieces = []
    for j in range(STEP_TILES):
        norm, *proj = _inproj_steps(x_next[j], scr[1 - j % 2], w)
        if out_pieces:
            residual, o0, o1, o2, o3, out_norm = out_pieces
            lead = [residual, o0, norm, o1]
            at_yield = [[o2], [o3]] + _spread(proj, MID_YIELDS - 2)
            at_yield[3] = at_yield[3] + [out_norm]
        else:
            lead = [norm, proj[0]]
            at_yield = _spread(proj[1:], MID_YIELDS)
        _run_interleaved(_mid_steps(scr[j % 2], N_META, w, state), lead, at_yield)
        out_pieces = _out_steps(scr[j % 2], w, y_ref.at[j * TT:(j + 1) * TT, :])
    for piece in out_pieces:
        piece()

    @pl.when(s % steps_per_seq == steps_per_seq - 1)
    def _():
        b = s // steps_per_seq
        for j in range(POOL_BUF):
            pool_out[0, j, pl.ds(b, 1), :] = hist_scr[1 + j:2 + j, :]
        c_out[0, 0] = c_scr[...]
        n_out[0, 0] = n_scr[0:N_HEADS, :]
        m_out[pl.ds(b, 1), :] = _m_row(m_scr)


def _prep_kernel(wt_ref, wgt_ref, xs_ref, norm1_ref, wout_ref, wmain_ref, wgate_ref, proj_ref, woutb_ref,
                 xn_scr, *, n_seg):
    i = pl.program_id(0)

    @pl.when(i == 0)
    def _():
        g = jnp.concatenate([wgt_ref[...], jnp.zeros((GATE_PAD - 8, D_MODEL), F32)], axis=0)
        wgate_ref[...] = g.T.astype(BF16)
        xn_scr[...] = _rms(xs_ref[...], norm1_ref[...]).astype(BF16)

    woutb_ref[...] = wout_ref[...].astype(BF16)

    @pl.when(i < n_seg)
    def _():
        wb = wt_ref[...].T.astype(BF16)
        wmain_ref[...] = wb
        proj_ref[...] = _dot(xn_scr[...], wb)


def _expand(seq_col, vals):
    out = None
    for i, val in enumerate(vals):
        pick = jnp.where(seq_col == i, val, 0.0)
        out = pick if out is None else out + pick
    return out


def _sample_kernel(x_ref, proj_ref, pool_ref, c_ref, n_ref, m_ref, *refs, seq_len, pos0):
    w = LayerW(None, *refs[:8])
    y_ref, pool_out, c_out, n_out, m_out = refs[8:13]
    ext_scr, ycat_scr = refs[13:15]
    G = SAMPLE_GROUP
    T = G * seq_len
    x = x_ref[...]
    xn = _rms(x, w.norm1[...]).astype(BF16)
    causal = _seq_mask(T, seq_len)
    g_col = _gate_pre(xn, w)
    b_col_all = _cumsum(*_cumsum_operands(g_col, causal))
    r_row_all = _to_rows(g_col - pltpu.roll(b_col_all, GATE_PAD - N_HEADS, axis=1))
    seq_col = lax.broadcasted_iota(jnp.int32, (T, 1), 0) >> (seq_len.bit_length() - 1)
    pos_col = jnp.full((seq_len, 1), float(pos0), F32) + lax.broadcasted_iota(
        jnp.int32, (seq_len, 1), 0).astype(F32)

    pooled_rows = []
    for i in range(G):
        rows = slice(i * seq_len, (i + 1) * seq_len)
        u_i = proj_ref[rows, OFF_U:OFF_U + 1024]
        ext_scr[0:1, :] = jnp.zeros((1, 1024), F32)
        for j in range(POOL_BUF):
            ext_scr[1 + j:2 + j, :] = pool_ref[0, j, i:i + 1, :]
        ext_scr[HIST:HIST + seq_len, :] = u_i
        ext = ext_scr[...]
        pooled_rows.append(_pooled(ext, u_i, pos_col))
        for j in range(POOL_BUF):
            pool_out[0, j, i:i + 1, :] = ext[seq_len + 1 + j:seq_len + 2 + j, :]
    mixed = _pool_mix([jnp.concatenate(p, axis=0) for p in zip(*pooled_rows)], w.pool)
    y_a = mixed * w.pscale[...] * _silu(proj_ref[:, OFF_ZA:OFF_ZA + 1024])
    ycat_scr[:, 0:1024] = y_a.astype(BF16)

    lane4 = lax.broadcasted_iota(jnp.int32, (1, N_HEADS), 1)
    last = [(i + 1) * seq_len - 1 for i in range(G)]
    m_new_heads = [None] * N_HEADS

    def head(h):
        hc = slice(h * 256, (h + 1) * 256)
        q = proj_ref[:, OFF_Q + h * 256:OFF_Q + (h + 1) * 256]
        k = proj_ref[:, OFF_K + h * 256:OFF_K + (h + 1) * 256]
        v = proj_ref[:, OFF_V + h * 256:OFF_V + (h + 1) * 256]
        qb, kb, vb = q.astype(BF16), k.astype(BF16), v.astype(BF16)
        c_old = [c_ref[0, i, h] for i in range(G)]
        c_old_b = [c.astype(BF16) for c in c_old]
        yield
        q_cs = [_dot(qb, cb) for cb in c_old_b]
        ig_col = g_col[:, h:h + 1]
        b_col = b_col_all[:, 4 + h:5 + h]
        r_row = r_row_all[h:h + 1, :]
        m_prev = [m_ref[0, i:i + 1, h:h + 1] for i in range(G)]
        a_col = b_col + _expand(seq_col, m_prev)
        m_t, inter, s = _intra_weights(qb, kb, b_col, a_col, r_row, causal)
        sb = s.astype(BF16)
        yield
        sv = _dot(sb, vb)
        n_old = [n_ref[0, i, h:h + 1, :] for i in range(G)]
        num = inter * _expand(seq_col, q_cs) + sv
        qn = inter * jnp.sum(q * _expand(seq_col, n_old), axis=-1, keepdims=True) + jnp.sum(
            s, axis=-1, keepdims=True)
        ht = num * (1.0 / jnp.maximum(jnp.abs(qn), jnp.exp(-m_t)))
        m_new = [m_t[r:r + 1, :] for r in last]
        b_last = [b_col[r:r + 1, :] for r in last]
        w_end = jnp.exp(_expand(seq_col, b_last) - b_col + ig_col - _expand(seq_col, m_new)) * K_SCALE
        kw = k * w_end
        kwb = kw.astype(BF16)
        v_seq = [jnp.where(seq_col == i, vb, jnp.zeros_like(vb)) for i in range(G)]
        yield
        updates = [_dot_tn(kwb, v_i) for v_i in v_seq]
        for i in range(G):
            decay = jnp.exp(b_last[i] + m_prev[i] - m_new[i])
            c_out[0, i, h] = decay * c_old[i] + updates[i]
            n_out[0, i, h:h + 1, :] = decay * n_old[i] + jnp.sum(
                kw[i * seq_len:(i + 1) * seq_len, :], axis=0, keepdims=True)
        m_new_heads[h] = m_new
        hn = _head_norm(ht, w.mhln[:, hc])
        o = proj_ref[:, OFF_O + h * 256:OFF_O + (h + 1) * 256]
        zb = proj_ref[:, OFF_ZB + h * 256:OFF_ZB + (h + 1) * 256]
        ycat_scr[:, 1024 + h * 256:1024 + (h + 1) * 256] = (hn * jax.nn.sigmoid(o) * _silu(zb)).astype(BF16)

    running = [head(h) for h in range(N_HEADS)]
    while running:
        running = [g for g in running if next(g, g) is not g]
    for i in range(G):
        row = jnp.zeros((1, N_HEADS), F32)
        for h in range(N_HEADS):
            row = jnp.where(lane4 == h, m_new_heads[h][i], row)
        m_out[0, i:i + 1, :] = row

    y = _dot(ycat_scr[...], w.out[...])
    y_ref[...] = _rms(x + y, w.normf[...])


def _const_spec(shape):
    nd = len(shape)
    return pl.BlockSpec(shape, lambda *_: (0,) * nd, pipeline_mode=pl.Buffered(1))


def _params(sem):
    return pltpu.CompilerParams(dimension_semantics=sem, vmem_limit_bytes=VMEM_LIMIT)


def _tile_scratch(T):
    return [pltpu.VMEM((T, TILE_F32_COLS), F32), pltpu.VMEM((T, D_MODEL), F32), pltpu.VMEM((T, GATE_PAD), F32),
            pltpu.VMEM((T, 2048), BF16), pltpu.VMEM((T, D_MODEL), BF16), pltpu.VMEM((T, TILE_BF16_COLS), BF16)]


def _state_scratch():
    return [pltpu.VMEM((HIST, D_POOL), F32), pltpu.VMEM((N_HEADS, HEAD_DIM, HEAD_DIM), F32),
            pltpu.VMEM((8, HEAD_DIM), F32), pltpu.VMEM((8, 128), F32)]


def kernel(x_prompt, x_sample, state_pool, state_C, state_n, state_m, meta_tokens, norm1_w, w_in,
           b_if, w_pool, pool_scale, mhln_w, w_out, normf_w):
    B, S, _ = x_prompt.shape
    SB, SL, _ = x_sample.shape
    TT = PROMPT_TILE
    assert norm1_w.shape[0] == 1, "single layer"
    assert STEP_TILES % 2 == 0 and S % (STEP_TILES * TT) == 0 and SB % SAMPLE_GROUP == 0

    w_in_t = jnp.swapaxes(w_in[0], 0, 1)
    norm1 = norm1_w[0].reshape(1, D_MODEL)
    n_tok = SB * SL
    xs = x_sample.reshape(n_tok, D_MODEL)
    n_seg = D_MAIN // 1024
    wout_rows = w_out.shape[1] // (n_seg + 1)
    seg = lambda i: jnp.minimum(i, n_seg - 1)
    w_main, w_gate, proj_s, wout = pl.pallas_call(
        functools.partial(_prep_kernel, n_seg=n_seg),
        grid=(n_seg + 1,),
        in_specs=[pl.BlockSpec((1024, D_MODEL), lambda i: (seg(i), 0)),
                  pl.BlockSpec((8, D_MODEL), lambda i: (D_MAIN // 8, 0)),
                  _const_spec(xs.shape), _const_spec(norm1.shape),
                  pl.BlockSpec((wout_rows, D_MODEL), lambda i: (i, 0))],
        out_specs=(pl.BlockSpec((D_MODEL, 1024), lambda i: (0, seg(i))),
                   pl.BlockSpec((D_MODEL, GATE_PAD), lambda i: (0, 0)),
                   pl.BlockSpec((n_tok, 1024), lambda i: (0, seg(i))),
                   pl.BlockSpec((wout_rows, D_MODEL), lambda i: (i, 0))),
        out_shape=(jax.ShapeDtypeStruct((D_MODEL, D_MAIN), BF16),
                   jax.ShapeDtypeStruct((D_MODEL, GATE_PAD), BF16),
                   jax.ShapeDtypeStruct((n_tok, D_MAIN), F32),
                   jax.ShapeDtypeStruct(w_out.shape[1:], BF16)),
        scratch_shapes=[pltpu.VMEM((n_tok, D_MODEL), BF16)],
        compiler_params=_params(("arbitrary",)),
        name="prep",
    )(w_in_t, w_in_t, xs, norm1, w_out[0])
    bias_row = jnp.pad(b_if[0], (0, GATE_PAD - 8)).reshape(1, GATE_PAD)
    wpool = w_pool[0].astype(BF16)
    pscale = pool_scale[0].reshape(1, D_POOL)
    mhln = mhln_w[0].reshape(1, D_MLSTM)
    normf = normf_w.reshape(1, D_MODEL)
    layer_w = LayerW(w_main, w_gate, bias_row, norm1, wpool, pscale, mhln, wout, normf)
    layer_specs = [_const_spec(a.shape) for a in layer_w]

    n_tiles = B * S // TT
    steps = n_tiles // STEP_TILES
    steps_per_seq = S // (STEP_TILES * TT)
    xp = x_prompt.reshape(B * S, D_MODEL)
    y_p, pool_p, c_p, n_p, m_p = pl.pallas_call(
        functools.partial(_prompt_kernel, steps_per_seq=steps_per_seq),
        grid=(steps,),
        in_specs=[_const_spec(meta_tokens.shape),
                  pl.BlockSpec((TT, D_MODEL), lambda s: (0, 0))]
                 + [pl.BlockSpec((TT, D_MODEL),
                                 lambda s, j=j: (jnp.minimum(STEP_TILES * s + 1 + j, n_tiles - 1), 0))
                    for j in range(STEP_TILES)]
                 + layer_specs,
        out_specs=(pl.BlockSpec((STEP_TILES * TT, D_MODEL), lambda s: (s, 0)),
                   pl.BlockSpec((1, POOL_BUF, B, D_POOL), lambda s: (0, 0, 0, 0)),
                   pl.BlockSpec((1, 1, N_HEADS, HEAD_DIM, HEAD_DIM),
                                lambda s: (0, s // steps_per_seq, 0, 0, 0)),
                   pl.BlockSpec((1, 1, N_HEADS, HEAD_DIM), lambda s: (0, s // steps_per_seq, 0, 0)),
                   pl.BlockSpec((B, 128), lambda s: (0, 0))),
        out_shape=(jax.ShapeDtypeStruct((B * S, D_MODEL), F32),
                   jax.ShapeDtypeStruct((1, POOL_BUF, B, D_POOL), F32),
                   jax.ShapeDtypeStruct((1, B, N_HEADS, HEAD_DIM, HEAD_DIM), F32),
                   jax.ShapeDtypeStruct((1, B, N_HEADS, HEAD_DIM), F32),
                   jax.ShapeDtypeStruct((B, 128), F32)),
        scratch_shapes=_tile_scratch(TT) + _tile_scratch(TT) + _tile_scratch(N_META)
        + _state_scratch() + _state_scratch(),
        compiler_params=_params(("arbitrary",)),
        name="prompt",
    )(meta_tokens, xp, *([xp] * STEP_TILES), *layer_w)
    y_prompt = y_p.reshape(B, S, D_MODEL)
    pool_p = jnp.swapaxes(pool_p, 1, 2)
    m_p = m_p[:, :N_HEADS].reshape(1, B, N_HEADS)

    G = SAMPLE_GROUP
    GT = G * SL
    pool_in = jnp.swapaxes(state_pool, 1, 2)
    sample_w = layer_w[1:]
    y_s, pool_s, c_s, n_s, m_s = pl.pallas_call(
        functools.partial(_sample_kernel, seq_len=SL, pos0=PAST_LEN),
        grid=(SB // G,),
        in_specs=[pl.BlockSpec((GT, D_MODEL), lambda i: (i, 0)),
                  pl.BlockSpec((GT, D_MAIN), lambda i: (i, 0)),
                  pl.BlockSpec((1, POOL_BUF, G, D_POOL), lambda i: (0, 0, i, 0)),
                  pl.BlockSpec((1, G, N_HEADS, HEAD_DIM, HEAD_DIM), lambda i: (0, i, 0, 0, 0)),
                  pl.BlockSpec((1, G, N_HEADS, HEAD_DIM), lambda i: (0, i, 0, 0)),
                  pl.BlockSpec((1, G, N_HEADS), lambda i: (0, i, 0))]
                 + [_const_spec(a.shape) for a in sample_w],
        out_specs=(pl.BlockSpec((GT, D_MODEL), lambda i: (i, 0)),
                   pl.BlockSpec((1, POOL_BUF, G, D_POOL), lambda i: (0, 0, i, 0)),
                   pl.BlockSpec((1, G, N_HEADS, HEAD_DIM, HEAD_DIM), lambda i: (0, i, 0, 0, 0)),
                   pl.BlockSpec((1, G, N_HEADS, HEAD_DIM), lambda i: (0, i, 0, 0)),
                   pl.BlockSpec((1, G, N_HEADS), lambda i: (0, i, 0))),
        out_shape=(jax.ShapeDtypeStruct((n_tok, D_MODEL), F32),
                   jax.ShapeDtypeStruct(pool_in.shape, F32),
                   jax.ShapeDtypeStruct(state_C.shape, F32),
                   jax.ShapeDtypeStruct(state_n.shape, F32),
                   jax.ShapeDtypeStruct(state_m.shape, F32)),
        scratch_shapes=[pltpu.VMEM((HIST + SL, D_POOL), F32), pltpu.VMEM((GT, 2048), BF16)],
        compiler_params=_params(("arbitrary",)),
        name="sample",
    )(xs, proj_s, pool_in, state_C, state_n, state_m, *sample_w)
    y_sample = y_s.reshape(SB, SL, D_MODEL)
    pool_s = jnp.swapaxes(pool_s, 1, 2)

    return (y_prompt, y_sample, pool_p, c_p, n_p, m_p, pool_s, c_s, n_s, m_s)
```

```python
import collections
import functools

import jax
import jax.numpy as jnp
from jax import lax
from jax.experimental import pallas as pl
from jax.experimental.pallas import tpu as pltpu

D_MODEL = 1024
D_POOL = 1024
D_MLSTM = 1024
N_HEADS = 4
HEAD_DIM = 256
POOL_WINDOWS = (2, 4, 8, 16)
POOL_BUF = 15
HIST = 16
N_META = 16
PAST_LEN = 16384
EPS = 1e-6
D_MAIN = 2 * D_POOL + 5 * D_MLSTM
GATE_PAD = 128
K_SCALE = HEAD_DIM ** -0.5

OFF_U, OFF_ZA, OFF_Q, OFF_K, OFF_V, OFF_O, OFF_ZB = (i * 1024 for i in range(7))
TILE_F32_COLS, TILE_BF16_COLS = 3072, 4096
TILE_SEG = {OFF_U: (0, 0, None), OFF_Q: (0, 1024, None), OFF_K: (0, 2048, None),
            OFF_ZA: (1, 0, "silu"), OFF_V: (1, 1024, None), OFF_O: (1, 2048, "sigmoid"), OFF_ZB: (1, 3072, "silu")}

PROMPT_TILE = 256
STEP_TILES = 2
SAMPLE_GROUP = 8
VMEM_LIMIT = 60000 * 1024

F32 = jnp.float32
BF16 = jnp.bfloat16

LayerW = collections.namedtuple("LayerW", "main gate bias norm1 pool pscale mhln out normf")


def _dot(a, b):
    return jnp.dot(a, b, preferred_element_type=F32)


def _dot_nt(a, b):
    return lax.dot_general(a, b, (((1,), (1,)), ((), ())), preferred_element_type=F32)


def _dot_tn(a, b):
    return lax.dot_general(a, b, (((0,), (0,)), ((), ())), preferred_element_type=F32)


def _rms(x, w):
    return x * lax.rsqrt(jnp.mean(x * x, axis=-1, keepdims=True) + EPS) * w


def _log_sigmoid(x):
    return jnp.minimum(x, 0.0) - jnp.log1p(jnp.exp(-jnp.abs(x)))


def _silu(x):
    return x * jax.nn.sigmoid(x)


def _split3(x):
    hi = x.astype(BF16)
    r = x - hi.astype(F32)
    mid = r.astype(BF16)
    lo = (r - mid.astype(F32)).astype(BF16)
    return hi, mid, lo


def _seq_mask(T, seq_len):
    row = lax.broadcasted_iota(jnp.int32, (T, T), 0)
    col = lax.broadcasted_iota(jnp.int32, (T, T), 1)
    causal = col <= row
    if seq_len < T:
        shift = seq_len.bit_length() - 1
        assert 1 << shift == seq_len
        causal = causal & ((row >> shift) == (col >> shift))
    return causal


def _to_rows(cols):
    T = cols.shape[0]
    pad = -T % 128
    if pad:
        cols = jnp.concatenate([cols, jnp.zeros((pad, cols.shape[1]), cols.dtype)], axis=0)
    return cols.T[:, 0:T]


def _gate_pre(xn, w):
    return _dot(xn, w.gate[...]) + w.bias[...]


def _cumsum_operands(g_col, causal):
    tri = jnp.where(causal, 1.0, 0.0).astype(BF16)
    return tri, _split3(_log_sigmoid(g_col))


def _cumsum(tri, parts):
    return sum(_dot(tri, p) for p in parts)


def _intra_weights(qb, kb, b_col, a_col, r_row, causal):
    dm = jnp.where(causal, b_col + r_row, -jnp.inf)
    m_t = jnp.maximum(a_col, jnp.max(dm, axis=-1, keepdims=True))
    w = jnp.exp(dm - m_t)
    inter = jnp.exp(a_col - m_t)
    return m_t, inter, _dot_nt(qb, kb) * (w * K_SCALE)


def _head_norm(ht, w_row):
    mu = jnp.mean(ht, axis=-1, keepdims=True)
    d = ht - mu
    var = jnp.mean(d * d, axis=-1, keepdims=True)
    return d * lax.rsqrt(var + EPS) * w_row


def _window_sums(ext):
    s2 = ext + pltpu.roll(ext, 1, axis=0)
    s4 = s2[:, 256:] + pltpu.roll(s2[:, 256:], 2, axis=0)
    s8 = s4[:, 256:] + pltpu.roll(s4[:, 256:], 4, axis=0)
    s16 = s8[:, 256:] + pltpu.roll(s8[:, 256:], 8, axis=0)
    return [s2[HIST:, 0:256], s4[HIST:, 0:256], s8[HIST:, 0:256], s16[HIST:, 0:256]]


def _pooled(ext, u, pos_col):
    sums = _window_sums(ext)
    return [sums[g] * (1.0 / jnp.minimum(float(w), pos_col + 1.0)) - u[:, g * 256:(g + 1) * 256]
            for g, w in enumerate(POOL_WINDOWS)]


def _pool_mix(pooled, wpool_ref):
    return jnp.concatenate([_dot(p.astype(BF16), wpool_ref[g]) for g, p in enumerate(pooled)], axis=-1)


def _inproj_steps(x_ref, scr, w):
    proj32_scr, x_scr, gcol_scr, _, xn_scr, proj16_scr = scr

    def norm():
        x = x_ref[...]
        x_scr[...] = x
        xn_scr[...] = _rms(x, w.norm1[...]).astype(BF16)

    def piece(c0):
        seg0 = c0 // 1024 * 1024
        buf, off, act = TILE_SEG[seg0]
        dst = (proj32_scr, proj16_scr)[buf]
        cols = slice(off + c0 - seg0, off + c0 - seg0 + PIECE_COLS)
        act = {None: lambda p: p, "silu": _silu, "sigmoid": jax.nn.sigmoid}[act]

        def run():
            dst[:, cols] = act(_dot(xn_scr[...], w.main[:, c0:c0 + PIECE_COLS])).astype(dst.dtype)
        return run

    def gates():
        gcol_scr[...] = _gate_pre(xn_scr[...], w)

    return [norm] + [piece(c0) for c0 in range(0, D_MAIN, PIECE_COLS)] + [gates]


def _mid_steps(scr, pos0, w, state):
    proj32_scr, x_scr, gcol_scr, ycat_scr, _, proj16_scr = scr
    hist_scr, c_scr, n_scr, m_scr = state
    T = x_scr.shape[0]

    def seg(seg0, h=None):
        buf, off, _ = TILE_SEG[seg0]
        src = (proj32_scr, proj16_scr)[buf]
        if h is None:
            return src[:, off:off + 1024]
        return src[:, off + h * 256:off + (h + 1) * 256]

    causal = _seq_mask(T, T)
    g_col = gcol_scr[...]
    tri, lf_parts = _cumsum_operands(g_col, causal)
    yield
    b_col_all = _cumsum(tri, lf_parts)
    r_row_all = _to_rows(g_col - pltpu.roll(b_col_all, GATE_PAD - N_HEADS, axis=1))

    u = seg(OFF_U)
    ext = jnp.concatenate([hist_scr[...], u], axis=0)
    pos_col = (lax.broadcasted_iota(jnp.int32, (T, 1), 0) + pos0).astype(F32)
    pooled = _pooled(ext, u, pos_col)
    hist_scr[...] = ext[T:T + HIST, :]
    yield
    mixed = _pool_mix(pooled, w.pool)
    ycat_scr[:, 0:1024] = (mixed * w.pscale[...] * seg(OFF_ZA).astype(F32)).astype(BF16)

    def head(h):
        q, k, vb = seg(OFF_Q, h), seg(OFF_K, h), seg(OFF_V, h)
        qb, kb = q.astype(BF16), k.astype(BF16)
        ig_col = g_col[:, h:h + 1]
        b_col = b_col_all[:, 4 + h:5 + h]
        r_row = r_row_all[h:h + 1, :]
        m_prev = m_scr[h:h + 1, 0:1]
        a_col = b_col + m_prev
        c_old = c_scr[h]
        n_old = n_scr[h:h + 1, :]
        q_c = _dot(qb, c_old.astype(BF16))
        m_t, inter, s = _intra_weights(qb, kb, b_col, a_col, r_row, causal)
        sb = s.astype(BF16)
        yield
        num = inter * q_c + _dot(sb, vb)
        qn = inter * jnp.sum(q * n_old, axis=-1, keepdims=True) + jnp.sum(s, axis=-1, keepdims=True)
        ht = num * (1.0 / jnp.maximum(jnp.abs(qn), jnp.exp(-m_t)))
        m_new = m_t[T - 1:T, :]
        b_last = b_col[T - 1:T, :]
        w_end = jnp.exp(b_last - b_col + ig_col - m_new) * K_SCALE
        decay = jnp.exp(b_last + m_prev - m_new)
        kw = k * w_end
        kwb = kw.astype(BF16)
        yield
        c_scr[h] = decay * c_old + _dot_tn(kwb, vb)
        n_scr[h:h + 1, :] = decay * n_old + jnp.sum(kw, axis=0, keepdims=True)
        m_scr[h:h + 1, :] = jnp.broadcast_to(m_new, (1, 128))
        hn = _head_norm(ht, w.mhln[:, h * 256:(h + 1) * 256])
        gate = seg(OFF_O, h).astype(F32) * seg(OFF_ZB, h).astype(F32)
        ycat_scr[:, 1024 + h * 256:1024 + (h + 1) * 256] = (hn * gate).astype(BF16)

    for pair in range(0, N_HEADS, HEADS_IN_FLIGHT):
        running = [head(h) for h in range(pair, pair + HEADS_IN_FLIGHT)]
        while running:
            alive = []
            for g in running:
                if next(g, g) is not g:
                    alive.append(g)
                yield
            running = alive


def _out_steps(scr, w, y_ref):
    _, x_scr, _, ycat_scr = scr[:4]

    def residual():
        y_ref[...] = x_scr[...]

    def piece(c0):
        def run():
            cols = slice(c0, c0 + OUT_COLS)
            y_ref[:, cols] = y_ref[:, cols] + _dot(ycat_scr[...], w.out[:, cols])
        return run

    def norm():
        y_ref[...] = _rms(y_ref[...], w.normf[...])

    return [residual] + [piece(c0) for c0 in range(0, D_MODEL, OUT_COLS)] + [norm]


PIECE_COLS = 512
OUT_COLS = 256
HEADS_IN_FLIGHT = 2
MID_YIELDS = 2 + 3 * N_HEADS


def _run_interleaved(mid, lead, at_yield):
    assert len(at_yield) == MID_YIELDS
    for piece in lead:
        piece()
    for pieces in at_yield:
        next(mid)
        for piece in pieces:
            piece()
    for _ in mid:
        raise AssertionError("unexpected extra yield")


def _spread(pieces, yields):
    base, extra = divmod(len(pieces), yields)
    groups, start = [], 0
    for i in range(yields):
        stop = start + base + (i < extra)
        groups.append(pieces[start:stop])
        start = stop
    return groups


def _m_row(m_scr):
    lane = lax.broadcasted_iota(jnp.int32, (1, 128), 1)
    row = jnp.zeros((1, 128), F32)
    for h in range(N_HEADS):
        row = jnp.where(lane == h, m_scr[h:h + 1, :], row)
    return row


def _prompt_kernel(meta_ref, x0_ref, *refs, steps_per_seq):
    x_next = refs[:STEP_TILES]
    refs = refs[STEP_TILES:]
    w = LayerW(*refs[:9])
    y_ref, pool_out, c_out, n_out, m_out = refs[9:14]
    scr = (refs[14:20], refs[20:26])
    scr0, scr_meta = scr[0], refs[26:32]
    state, state_meta = refs[32:36], refs[36:40]
    hist_scr, c_scr, n_scr, m_scr = state
    s = pl.program_id(0)
    TT = PROMPT_TILE

    @pl.when(s == 0)
    def _():
        for ref in state_meta:
            ref[...] = jnp.zeros_like(ref)
        for p in _inproj_steps(meta_ref, scr_meta, w):
            p()
        for _ in _mid_steps(scr_meta, 0, w, state_meta):
            pass
        for p in _inproj_steps(x0_ref, scr0, w):
            p()

    @pl.when(s % steps_per_seq == 0)
    def _():
        for ref, ref_meta in zip(state, state_meta):
            ref[...] = ref_meta[...]

    out_pieces = []
    for j in range(STEP_TILES):
        norm, *proj = _inproj_steps(x_next[j], scr[1 - j % 2], w)
        if out_pieces:
            residual, o0, o1, o2, o3, out_norm = out_pieces
            lead = [residual, o0, norm, o1]
            at_yield = [[o2], [o3]] + _spread(proj, MID_YIELDS - 2)
            at_yield[3] = at_yield[3] + [out_norm]
        else:
            lead = [norm, proj[0]]
            at_yield = _spread(proj[1:], MID_YIELDS)
        _run_interleaved(_mid_steps(scr[j % 2], N_META, w, state), lead, at_yield)
        out_pieces = _out_steps(scr[j % 2], w, y_ref.at[j * TT:(j + 1) * TT, :])
    for piece in out_pieces:
        piece()

    @pl.when(s % steps_per_seq == steps_per_seq - 1)
    def _():
        b = s // steps_per_seq
        for j in range(POOL_BUF):
            pool_out[0, j, pl.ds(b, 1), :] = hist_scr[1 + j:2 + j, :]
        c_out[0, 0] = c_scr[...]
        n_out[0, 0] = n_scr[0:N_HEADS, :]
        m_out[pl.ds(b, 1), :] = _m_row(m_scr)


def _prep_kernel(wt_ref, wgt_ref, xs_ref, norm1_ref, wout_ref, wmain_ref, wgate_ref, proj_ref, woutb_ref,
                 xn_scr, *, n_seg):
    i = pl.program_id(0)

    @pl.when(i == 0)
    def _():
        g = jnp.concatenate([wgt_ref[...], jnp.zeros((GATE_PAD - 8, D_MODEL), F32)], axis=0)
        wgate_ref[...] = g.T.astype(BF16)
        xn_scr[...] = _rms(xs_ref[...], norm1_ref[...]).astype(BF16)

    woutb_ref[...] = wout_ref[...].astype(BF16)

    @pl.when(i < n_seg)
    def _():
        wb = wt_ref[...].T.astype(BF16)
        wmain_ref[...] = wb
        proj_ref[...] = _dot(xn_scr[...], wb)


def _expand(seq_col, vals):
    out = None
    for i, val in enumerate(vals):
        pick = jnp.where(seq_col == i, val, 0.0)
        out = pick if out is None else out + pick
    return out


def _sample_kernel(x_ref, proj_ref, pool_ref, c_ref, n_ref, m_ref, *refs, seq_len, pos0):
    w = LayerW(None, *refs[:8])
    y_ref, pool_out, c_out, n_out, m_out = refs[8:13]
    ext_scr, ycat_scr = refs[13:15]
    G = SAMPLE_GROUP
    T = G * seq_len
    x = x_ref[...]
    xn = _rms(x, w.norm1[...]).astype(BF16)
    causal = _seq_mask(T, seq_len)
    g_col = _gate_pre(xn, w)
    b_col_all = _cumsum(*_cumsum_operands(g_col, causal))
    r_row_all = _to_rows(g_col - pltpu.roll(b_col_all, GATE_PAD - N_HEADS, axis=1))
    seq_col = lax.broadcasted_iota(jnp.int32, (T, 1), 0) >> (seq_len.bit_length() - 1)
    pos_col = jnp.full((seq_len, 1), float(pos0), F32) + lax.broadcasted_iota(
        jnp.int32, (seq_len, 1), 0).astype(F32)

    pooled_rows = []
    for i in range(G):
        rows = slice(i * seq_len, (i + 1) * seq_len)
        u_i = proj_ref[rows, OFF_U:OFF_U + 1024]
        ext_scr[0:1, :] = jnp.zeros((1, 1024), F32)
        for j in range(POOL_BUF):
            ext_scr[1 + j:2 + j, :] = pool_ref[0, j, i:i + 1, :]
        ext_scr[HIST:HIST + seq_len, :] = u_i
        ext = ext_scr[...]
        pooled_rows.append(_pooled(ext, u_i, pos_col))
        for j in range(POOL_BUF):
            pool_out[0, j, i:i + 1, :] = ext[seq_len + 1 + j:seq_len + 2 + j, :]
    mixed = _pool_mix([jnp.concatenate(p, axis=0) for p in zip(*pooled_rows)], w.pool)
    y_a = mixed * w.pscale[...] * _silu(proj_ref[:, OFF_ZA:OFF_ZA + 1024])
    ycat_scr[:, 0:1024] = y_a.astype(BF16)

    lane4 = lax.broadcasted_iota(jnp.int32, (1, N_HEADS), 1)
    last = [(i + 1) * seq_len - 1 for i in range(G)]
    m_new_heads = [None] * N_HEADS

    def head(h):
        hc = slice(h * 256, (h + 1) * 256)
        q = proj_ref[:, OFF_Q + h * 256:OFF_Q + (h + 1) * 256]
        k = proj_ref[:, OFF_K + h * 256:OFF_K + (h + 1) * 256]
        v = proj_ref[:, OFF_V + h * 256:OFF_V + (h + 1) * 256]
        qb, kb, vb = q.astype(BF16), k.astype(BF16), v.astype(BF16)
        c_old = [c_ref[0, i, h] for i in range(G)]
        c_old_b = [c.astype(BF16) for c in c_old]
        yield
        q_cs = [_dot(qb, cb) for cb in c_old_b]
        ig_col = g_col[:, h:h + 1]
        b_col = b_col_all[:, 4 + h:5 + h]
        r_row = r_row_all[h:h + 1, :]
        m_prev = [m_ref[0, i:i + 1, h:h + 1] for i in range(G)]
        a_col = b_col + _expand(seq_col, m_prev)
        m_t, inter, s = _intra_weights(qb, kb, b_col, a_col, r_row, causal)
        sb = s.astype(BF16)
        yield
        sv = _dot(sb, vb)
        n_old = [n_ref[0, i, h:h + 1, :] for i in range(G)]
        num = inter * _expand(seq_col, q_cs) + sv
        qn = inter * jnp.sum(q * _expand(seq_col, n_old), axis=-1, keepdims=True) + jnp.sum(
            s, axis=-1, keepdims=True)
        ht = num * (1.0 / jnp.maximum(jnp.abs(qn), jnp.exp(-m_t)))
        m_new = [m_t[r:r + 1, :] for r in last]
        b_last = [b_col[r:r + 1, :] for r in last]
        w_end = jnp.exp(_expand(seq_col, b_last) - b_col + ig_col - _expand(seq_col, m_new)) * K_SCALE
        kw = k * w_end
        kwb = kw.astype(BF16)
        v_seq = [jnp.where(seq_col == i, vb, jnp.zeros_like(vb)) for i in range(G)]
        yield
        updates = [_dot_tn(kwb, v_i) for v_i in v_seq]
        for i in range(G):
            decay = jnp.exp(b_last[i] + m_prev[i] - m_new[i])
            c_out[0, i, h] = decay * c_old[i] + updates[i]
            n_out[0, i, h:h + 1, :] = decay * n_old[i] + jnp.sum(
                kw[i * seq_len:(i + 1) * seq_len, :], axis=0, keepdims=True)
        m_new_heads[h] = m_new
        hn = _head_norm(ht, w.mhln[:, hc])
        o = proj_ref[:, OFF_O + h * 256:OFF_O + (h + 1) * 256]
        zb = proj_ref[:, OFF_ZB + h * 256:OFF_ZB + (h + 1) * 256]
        ycat_scr[:, 1024 + h * 256:1024 + (h + 1) * 256] = (hn * jax.nn.sigmoid(o) * _silu(zb)).astype(BF16)

    running = [head(h) for h in range(N_HEADS)]
    while running:
        running = [g for g in running if next(g, g) is not g]
    for i in range(G):
        row = jnp.zeros((1, N_HEADS), F32)
        for h in range(N_HEADS):
            row = jnp.where(lane4 == h, m_new_heads[h][i], row)
        m_out[0, i:i + 1, :] = row

    y = _dot(ycat_scr[...], w.out[...])
    y_ref[...] = _rms(x + y, w.normf[...])


def _const_spec(shape):
    nd = len(shape)
    return pl.BlockSpec(shape, lambda *_: (0,) * nd, pipeline_mode=pl.Buffered(1))


def _params(sem):
    return pltpu.CompilerParams(dimension_semantics=sem, vmem_limit_bytes=VMEM_LIMIT)


def _tile_scratch(T):
    return [pltpu.VMEM((T, TILE_F32_COLS), F32), pltpu.VMEM((T, D_MODEL), F32), pltpu.VMEM((T, GATE_PAD), F32),
            pltpu.VMEM((T, 2048), BF16), pltpu.VMEM((T, D_MODEL), BF16), pltpu.VMEM((T, TILE_BF16_COLS), BF16)]


def _state_scratch():
    return [pltpu.VMEM((HIST, D_POOL), F32), pltpu.VMEM((N_HEADS, HEAD_DIM, HEAD_DIM), F32),
            pltpu.VMEM((8, HEAD_DIM), F32), pltpu.VMEM((8, 128), F32)]


def kernel(x_prompt, x_sample, state_pool, state_C, state_n, state_m, meta_tokens, norm1_w, w_in,
           b_if, w_pool, pool_scale, mhln_w, w_out, normf_w):
    B, S, _ = x_prompt.shape
    SB, SL, _ = x_sample.shape
    TT = PROMPT_TILE
    assert norm1_w.shape[0] == 1, "single layer"
    assert STEP_TILES % 2 == 0 and S % (STEP_TILES * TT) == 0 and SB % SAMPLE_GROUP == 0

    w_in_t = jnp.swapaxes(w_in[0], 0, 1)
    norm1 = norm1_w[0].reshape(1, D_MODEL)
    n_tok = SB * SL
    xs = x_sample.reshape(n_tok, D_MODEL)
    n_seg = D_MAIN // 1024
    wout_rows = w_out.shape[1] // (n_seg + 1)
    seg = lambda i: jnp.minimum(i, n_seg - 1)
    w_main, w_gate, proj_s, wout = pl.pallas_call(
        functools.partial(_prep_kernel, n_seg=n_seg),
        grid=(n_seg + 1,),
        in_specs=[pl.BlockSpec((1024, D_MODEL), lambda i: (seg(i), 0)),
                  pl.BlockSpec((8, D_MODEL), lambda i: (D_MAIN // 8, 0)),
                  _const_spec(xs.shape), _const_spec(norm1.shape),
                  pl.BlockSpec((wout_rows, D_MODEL), lambda i: (i, 0))],
        out_specs=(pl.BlockSpec((D_MODEL, 1024), lambda i: (0, seg(i))),
                   pl.BlockSpec((D_MODEL, GATE_PAD), lambda i: (0, 0)),
                   pl.BlockSpec((n_tok, 1024), lambda i: (0, seg(i))),
                   pl.BlockSpec((wout_rows, D_MODEL), lambda i: (i, 0))),
        out_shape=(jax.ShapeDtypeStruct((D_MODEL, D_MAIN), BF16),
                   jax.ShapeDtypeStruct((D_MODEL, GATE_PAD), BF16),
                   jax.ShapeDtypeStruct((n_tok, D_MAIN), F32),
                   jax.ShapeDtypeStruct(w_out.shape[1:], BF16)),
        scratch_shapes=[pltpu.VMEM((n_tok, D_MODEL), BF16)],
        compiler_params=_params(("arbitrary",)),
        name="prep",
    )(w_in_t, w_in_t, xs, norm1, w_out[0])
    bias_row = jnp.pad(b_if[0], (0, GATE_PAD - 8)).reshape(1, GATE_PAD)
    wpool = w_pool[0].astype(BF16)
    pscale = pool_scale[0].reshape(1, D_POOL)
    mhln = mhln_w[0].reshape(1, D_MLSTM)
    normf = normf_w.reshape(1, D_MODEL)
    layer_w = LayerW(w_main, w_gate, bias_row, norm1, wpool, pscale, mhln, wout, normf)
    layer_specs = [_const_spec(a.shape) for a in layer_w]

    n_tiles = B * S // TT
    steps = n_tiles // STEP_TILES
    steps_per_seq = S // (STEP_TILES * TT)
    xp = x_prompt.reshape(B * S, D_MODEL)
    y_p, pool_p, c_p, n_p, m_p = pl.pallas_call(
        functools.partial(_prompt_kernel, steps_per_seq=steps_per_seq),
        grid=(steps,),
        in_specs=[_const_spec(meta_tokens.shape),
                  pl.BlockSpec((TT, D_MODEL), lambda s: (0, 0))]
                 + [pl.BlockSpec((TT, D_MODEL),
                                 lambda s, j=j: (jnp.minimum(STEP_TILES * s + 1 + j, n_tiles - 1), 0))
                    for j in range(STEP_TILES)]
                 + layer_specs,
        out_specs=(pl.BlockSpec((STEP_TILES * TT, D_MODEL), lambda s: (s, 0)),
                   pl.BlockSpec((1, POOL_BUF, B, D_POOL), lambda s: (0, 0, 0, 0)),
                   pl.BlockSpec((1, 1, N_HEADS, HEAD_DIM, HEAD_DIM),
                                lambda s: (0, s // steps_per_seq, 0, 0, 0)),
                   pl.BlockSpec((1, 1, N_HEADS, HEAD_DIM), lambda s: (0, s // steps_per_seq, 0, 0)),
                   pl.BlockSpec((B, 128), lambda s: (0, 0))),
        out_shape=(jax.ShapeDtypeStruct((B * S, D_MODEL), F32),
                   jax.ShapeDtypeStruct((1, POOL_BUF, B, D_POOL), F32),
                   jax.ShapeDtypeStruct((1, B, N_HEADS, HEAD_DIM, HEAD_DIM), F32),
                   jax.ShapeDtypeStruct((1, B, N_HEADS, HEAD_DIM), F32),
                   jax.ShapeDtypeStruct((B, 128), F32)),
        scratch_shapes=_tile_scratch(TT) + _tile_scratch(TT) + _tile_scratch(N_META)
        + _state_scratch() + _state_scratch(),
        compiler_params=_params(("arbitrary",)),
        name="prompt",
    )(meta_tokens, xp, *([xp] * STEP_TILES), *layer_w)
    y_prompt = y_p.reshape(B, S, D_MODEL)
    pool_p = jnp.swapaxes(pool_p, 1, 2)
    m_p = m_p[:, :N_HEADS].reshape(1, B, N_HEADS)

    G = SAMPLE_GROUP
    GT = G * SL
    pool_in = jnp.swapaxes(state_pool, 1, 2)
    sample_w = layer_w[1:]
    y_s, pool_s, c_s, n_s, m_s = pl.pallas_call(
        functools.partial(_sample_kernel, seq_len=SL, pos0=PAST_LEN),
        grid=(SB // G,),
        in_specs=[pl.BlockSpec((GT, D_MODEL), lambda i: (i, 0)),
                  pl.BlockSpec((GT, D_MAIN), lambda i: (i, 0)),
                  pl.BlockSpec((1, POOL_BUF, G, D_POOL), lambda i: (0, 0, i, 0)),
                  pl.BlockSpec((1, G, N_HEADS, HEAD_DIM, HEAD_DIM), lambda i: (0, i, 0, 0, 0)),
                  pl.BlockSpec((1, G, N_HEADS, HEAD_DIM), lambda i: (0, i, 0, 0)),
                  pl.BlockSpec((1, G, N_HEADS), lambda i: (0, i, 0))]
                 + [_const_spec(a.shape) for a in sample_w],
        out_specs=(pl.BlockSpec((GT, D_MODEL), lambda i: (i, 0)),
                   pl.BlockSpec((1, POOL_BUF, G, D_POOL), lambda i: (0, 0, i, 0)),
                   pl.BlockSpec((1, G, N_HEADS, HEAD_DIM, HEAD_DIM), lambda i: (0, i, 0, 0, 0)),
                   pl.BlockSpec((1, G, N_HEADS, HEAD_DIM), lambda i: (0, i, 0, 0)),
                   pl.BlockSpec((1, G, N_HEADS), lambda i: (0, i, 0))),
        out_shape=(jax.ShapeDtypeStruct((n_tok, D_MODEL), F32),
                   jax.ShapeDtypeStruct(pool_in.shape, F32),
                   jax.ShapeDtypeStruct(state_C.shape, F32),
                   jax.ShapeDtypeStruct(state_n.shape, F32),
                   jax.ShapeDtypeStruct(state_m.shape, F32)),
        scratch_shapes=[pltpu.VMEM((HIST + SL, D_POOL), F32), pltpu.VMEM((GT, 2048), BF16)],
        compiler_params=_params(("arbitrary",)),
        name="sample",
    )(xs, proj_s, pool_in, state_C, state_n, state_m, *sample_w)
    y_sample = y_s.reshape(SB, SL, D_MODEL)
    pool_s = jnp.swapaxes(pool_s, 1, 2)

    return (y_prompt, y_sample, pool_p, c_p, n_p, m_p, pool_s, c_s, n_s, m_s)
```

```python
import collections
import functools

import jax
import jax.numpy as jnp
from jax import lax
from jax.experimental import pallas as pl
from jax.experimental.pallas import tpu as pltpu

D_MODEL = 1024
D_POOL = 1024
D_MLSTM = 1024
N_HEADS = 4
HEAD_DIM = 256
POOL_WINDOWS = (2, 4, 8, 16)
POOL_BUF = 15
HIST = 16
N_META = 16
PAST_LEN = 16384
EPS = 1e-6
D_MAIN = 2 * D_POOL + 5 * D_MLSTM
GATE_PAD = 128
K_SCALE = HEAD_DIM ** -0.5

OFF_U, OFF_ZA, OFF_Q, OFF_K, OFF_V, OFF_O, OFF_ZB = (i * 1024 for i in range(7))

PROMPT_TILE = 256
STEP_TILES = 2
SAMPLE_GROUP = 8
VMEM_LIMIT = 60000 * 1024

F32 = jnp.float32
BF16 = jnp.bfloat16

LayerW = collections.namedtuple("LayerW", "main gate bias norm1 pool pscale mhln out normf")


def _dot(a, b):
    return jnp.dot(a, b, preferred_element_type=F32)


def _dot_nt(a, b):
    return lax.dot_general(a, b, (((1,), (1,)), ((), ())), preferred_element_type=F32)


def _dot_tn(a, b):
    return lax.dot_general(a, b, (((0,), (0,)), ((), ())), preferred_element_type=F32)


def _rms(x, w):
    return x * lax.rsqrt(jnp.mean(x * x, axis=-1, keepdims=True) + EPS) * w


def _log_sigmoid(x):
    return jnp.minimum(x, 0.0) - jnp.log1p(jnp.exp(-jnp.abs(x)))


def _silu(x):
    return x * jax.nn.sigmoid(x)


def _split3(x):
    hi = x.astype(BF16)
    r = x - hi.astype(F32)
    mid = r.astype(BF16)
    lo = (r - mid.astype(F32)).astype(BF16)
    return hi, mid, lo


def _seq_mask(T, seq_len):
    row = lax.broadcasted_iota(jnp.int32, (T, T), 0)
    col = lax.broadcasted_iota(jnp.int32, (T, T), 1)
    causal = col <= row
    if seq_len < T:
        shift = seq_len.bit_length() - 1
        assert 1 << shift == seq_len
        causal = causal & ((row >> shift) == (col >> shift))
    return causal


def _to_rows(cols):
    T = cols.shape[0]
    pad = -T % 128
    if pad:
        cols = jnp.concatenate([cols, jnp.zeros((pad, cols.shape[1]), cols.dtype)], axis=0)
    return cols.T[:, 0:T]


def _gate_pre(xn, w):
    return _dot(xn, w.gate[...]) + w.bias[...]


def _cumsum_operands(g_col, causal):
    tri = jnp.where(causal, 1.0, 0.0).astype(BF16)
    return tri, _split3(_log_sigmoid(g_col))


def _cumsum(tri, parts):
    return sum(_dot(tri, p) for p in parts)


def _intra_weights(qb, kb, b_col, a_col, r_row, causal):
    dm = jnp.where(causal, b_col + r_row, -jnp.inf)
    m_t = jnp.maximum(a_col, jnp.max(dm, axis=-1, keepdims=True))
    w = jnp.exp(dm - m_t)
    inter = jnp.exp(a_col - m_t)
    return m_t, inter, _dot_nt(qb, kb) * (w * K_SCALE)


def _head_norm(ht, w_row):
    mu = jnp.mean(ht, axis=-1, keepdims=True)
    d = ht - mu
    var = jnp.mean(d * d, axis=-1, keepdims=True)
    return d * lax.rsqrt(var + EPS) * w_row


def _window_sums(ext):
    s2 = ext + pltpu.roll(ext, 1, axis=0)
    s4 = s2[:, 256:] + pltpu.roll(s2[:, 256:], 2, axis=0)
    s8 = s4[:, 256:] + pltpu.roll(s4[:, 256:], 4, axis=0)
    s16 = s8[:, 256:] + pltpu.roll(s8[:, 256:], 8, axis=0)
    return [s2[HIST:, 0:256], s4[HIST:, 0:256], s8[HIST:, 0:256], s16[HIST:, 0:256]]


def _pooled(ext, u, pos_col):
    sums = _window_sums(ext)
    return [sums[g] * (1.0 / jnp.minimum(float(w), pos_col + 1.0)) - u[:, g * 256:(g + 1) * 256]
            for g, w in enumerate(POOL_WINDOWS)]


def _pool_mix(pooled, wpool_ref):
    return jnp.concatenate([_dot(p.astype(BF16), wpool_ref[g]) for g, p in enumerate(pooled)], axis=-1)


def _inproj_steps(x_ref, scr, w):
    proj_scr, x_scr, gcol_scr, _, xn_scr = scr

    def norm():
        x = x_ref[...]
        x_scr[...] = x
        xn_scr[...] = _rms(x, w.norm1[...]).astype(BF16)

    def piece(c0):
        act = {OFF_ZA: _silu, OFF_O: jax.nn.sigmoid, OFF_ZB: _silu}.get(c0 // 1024 * 1024, lambda p: p)

        def run():
            proj_scr[:, c0:c0 + PIECE_COLS] = act(_dot(xn_scr[...], w.main[:, c0:c0 + PIECE_COLS]))
        return run

    def gates():
        gcol_scr[...] = _gate_pre(xn_scr[...], w)

    return [norm] + [piece(c0) for c0 in range(0, D_MAIN, PIECE_COLS)] + [gates]


def _mid_steps(scr, pos0, w, state):
    proj_scr, x_scr, gcol_scr, ycat_scr, _ = scr
    hist_scr, c_scr, n_scr, m_scr = state
    T = x_scr.shape[0]

    def seg(off, h=None):
        if h is None:
            return proj_scr[:, off:off + 1024]
        return proj_scr[:, off + h * 256:off + (h + 1) * 256]

    causal = _seq_mask(T, T)
    g_col = gcol_scr[...]
    tri, lf_parts = _cumsum_operands(g_col, causal)
    yield
    b_col_all = _cumsum(tri, lf_parts)
    r_row_all = _to_rows(g_col - pltpu.roll(b_col_all, GATE_PAD - N_HEADS, axis=1))

    u = seg(OFF_U)
    ext = jnp.concatenate([hist_scr[...], u], axis=0)
    pos_col = (lax.broadcasted_iota(jnp.int32, (T, 1), 0) + pos0).astype(F32)
    pooled = _pooled(ext, u, pos_col)
    hist_scr[...] = ext[T:T + HIST, :]
    yield
    mixed = _pool_mix(pooled, w.pool)
    ycat_scr[:, 0:1024] = (mixed * w.pscale[...] * seg(OFF_ZA)).astype(BF16)

    def head(h):
        q, k, v = seg(OFF_Q, h), seg(OFF_K, h), seg(OFF_V, h)
        qb, kb, vb = q.astype(BF16), k.astype(BF16), v.astype(BF16)
        ig_col = g_col[:, h:h + 1]
        b_col = b_col_all[:, 4 + h:5 + h]
        r_row = r_row_all[h:h + 1, :]
        m_prev = m_scr[h:h + 1, 0:1]
        a_col = b_col + m_prev
        c_old = c_scr[h]
        n_old = n_scr[h:h + 1, :]
        q_c = _dot(qb, c_old.astype(BF16))
        m_t, inter, s = _intra_weights(qb, kb, b_col, a_col, r_row, causal)
        sb = s.astype(BF16)
        yield
        num = inter * q_c + _dot(sb, vb)
        qn = inter * jnp.sum(q * n_old, axis=-1, keepdims=True) + jnp.sum(s, axis=-1, keepdims=True)
        ht = num * (1.0 / jnp.maximum(jnp.abs(qn), jnp.exp(-m_t)))
        m_new = m_t[T - 1:T, :]
        b_last = b_col[T - 1:T, :]
        w_end = jnp.exp(b_last - b_col + ig_col - m_new) * K_SCALE
        decay = jnp.exp(b_last + m_prev - m_new)
        kw = k * w_end
        kwb = kw.astype(BF16)
        yield
        c_scr[h] = decay * c_old + _dot_tn(kwb, vb)
        n_scr[h:h + 1, :] = decay * n_old + jnp.sum(kw, axis=0, keepdims=True)
        m_scr[h:h + 1, :] = jnp.broadcast_to(m_new, (1, 128))
        hn = _head_norm(ht, w.mhln[:, h * 256:(h + 1) * 256])
        ycat_scr[:, 1024 + h * 256:1024 + (h + 1) * 256] = (hn * seg(OFF_O, h) * seg(OFF_ZB, h)).astype(BF16)

    for pair in range(0, N_HEADS, HEADS_IN_FLIGHT):
        running = [head(h) for h in range(pair, pair + HEADS_IN_FLIGHT)]
        while running:
            alive = []
            for g in running:
                if next(g, g) is not g:
                    alive.append(g)
                yield
            running = alive


def _out_steps(scr, w, y_ref):
    _, x_scr, _, ycat_scr, _ = scr

    def residual():
        y_ref[...] = x_scr[...]

    def piece(c0):
        def run():
            cols = slice(c0, c0 + OUT_COLS)
            y_ref[:, cols] = y_ref[:, cols] + _dot(ycat_scr[...], w.out[:, cols])
        return run

    def norm():
        y_ref[...] = _rms(y_ref[...], w.normf[...])

    return [residual] + [piece(c0) for c0 in range(0, D_MODEL, OUT_COLS)] + [norm]


PIECE_COLS = 256
OUT_COLS = 256
HEADS_IN_FLIGHT = 2
MID_YIELDS = 2 + 3 * N_HEADS


def _run_interleaved(mid, lead, at_yield):
    assert len(at_yield) == MID_YIELDS
    for piece in lead:
        piece()
    for pieces in at_yield:
        next(mid)
        for piece in pieces:
            piece()
    for _ in mid:
        raise AssertionError("unexpected extra yield")


def _spread(pieces, yields):
    base, extra = divmod(len(pieces), yields)
    groups, start = [], 0
    for i in range(yields):
        stop = start + base + (i < extra)
        groups.append(pieces[start:stop])
        start = stop
    return groups


def _m_row(m_scr):
    lane = lax.broadcasted_iota(jnp.int32, (1, 128), 1)
    row = jnp.zeros((1, 128), F32)
    for h in range(N_HEADS):
        row = jnp.where(lane == h, m_scr[h:h + 1, :], row)
    return row


def _prompt_kernel(meta_ref, x0_ref, *refs, steps_per_seq):
    x_next = refs[:STEP_TILES]
    refs = refs[STEP_TILES:]
    w = LayerW(*refs[:9])
    y_ref, pool_out, c_out, n_out, m_out = refs[9:14]
    scr = (refs[14:19], refs[19:24])
    scr0, scr_meta = scr[0], refs[24:29]
    state, state_meta = refs[29:33], refs[33:37]
    hist_scr, c_scr, n_scr, m_scr = state
    s = pl.program_id(0)
    TT = PROMPT_TILE

    @pl.when(s == 0)
    def _():
        for ref in state_meta:
            ref[...] = jnp.zeros_like(ref)
        for p in _inproj_steps(meta_ref, scr_meta, w):
            p()
        for _ in _mid_steps(scr_meta, 0, w, state_meta):
            pass
        for p in _inproj_steps(x0_ref, scr0, w):
            p()

    @pl.when(s % steps_per_seq == 0)
    def _():
        for ref, ref_meta in zip(state, state_meta):
            ref[...] = ref_meta[...]

    out_pieces = []
    for j in range(STEP_TILES):
        norm, *proj = _inproj_steps(x_next[j], scr[1 - j % 2], w)
        if out_pieces:
            residual, o0, o1, o2, o3, out_norm = out_pieces
            lead = [residual, o0, norm, o1]
            at_yield = [[o2], [o3]] + _spread(proj, MID_YIELDS - 2)
            at_yield[3] = at_yield[3] + [out_norm]
        else:
            lead = [norm, proj[0]]
            at_yield = _spread(proj[1:], MID_YIELDS)
        _run_interleaved(_mid_steps(scr[j % 2], N_META, w, state), lead, at_yield)
        out_pieces = _out_steps(scr[j % 2], w, y_ref.at[j * TT:(j + 1) * TT, :])
    for piece in out_pieces:
        piece()

    @pl.when(s % steps_per_seq == steps_per_seq - 1)
    def _():
        b = s // steps_per_seq
        for j in range(POOL_BUF):
            pool_out[0, j, pl.ds(b, 1), :] = hist_scr[1 + j:2 + j, :]
        c_out[0, 0] = c_scr[...]
        n_out[0, 0] = n_scr[0:N_HEADS, :]
        m_out[pl.ds(b, 1), :] = _m_row(m_scr)


def _prep_kernel(wt_ref, wgt_ref, xs_ref, norm1_ref, wout_ref, wmain_ref, wgate_ref, proj_ref, woutb_ref,
                 xn_scr, *, n_seg):
    i = pl.program_id(0)

    @pl.when(i == 0)
    def _():
        g = jnp.concatenate([wgt_ref[...], jnp.zeros((GATE_PAD - 8, D_MODEL), F32)], axis=0)
        wgate_ref[...] = g.T.astype(BF16)
        xn_scr[...] = _rms(xs_ref[...], norm1_ref[...]).astype(BF16)

    woutb_ref[...] = wout_ref[...].astype(BF16)

    @pl.when(i < n_seg)
    def _():
        wb = wt_ref[...].T.astype(BF16)
        wmain_ref[...] = wb
        proj_ref[...] = _dot(xn_scr[...], wb)


def _expand(seq_col, vals):
    out = None
    for i, val in enumerate(vals):
        pick = jnp.where(seq_col == i, val, 0.0)
        out = pick if out is None else out + pick
    return out


def _sample_kernel(x_ref, proj_ref, pool_ref, c_ref, n_ref, m_ref, *refs, seq_len, pos0):
    w = LayerW(None, *refs[:8])
    y_ref, pool_out, c_out, n_out, m_out = refs[8:13]
    ext_scr, ycat_scr = refs[13:15]
    G = SAMPLE_GROUP
    T = G * seq_len
    x = x_ref[...]
    xn = _rms(x, w.norm1[...]).astype(BF16)
    causal = _seq_mask(T, seq_len)
    g_col = _gate_pre(xn, w)
    b_col_all = _cumsum(*_cumsum_operands(g_col, causal))
    r_row_all = _to_rows(g_col - pltpu.roll(b_col_all, GATE_PAD - N_HEADS, axis=1))
    seq_col = lax.broadcasted_iota(jnp.int32, (T, 1), 0) >> (seq_len.bit_length() - 1)
    pos_col = jnp.full((seq_len, 1), float(pos0), F32) + lax.broadcasted_iota(
        jnp.int32, (seq_len, 1), 0).astype(F32)

    pooled_rows = []
    for i in range(G):
        rows = slice(i * seq_len, (i + 1) * seq_len)
        u_i = proj_ref[rows, OFF_U:OFF_U + 1024]
        ext_scr[0:1, :] = jnp.zeros((1, 1024), F32)
        for j in range(POOL_BUF):
            ext_scr[1 + j:2 + j, :] = pool_ref[0, j, i:i + 1, :]
        ext_scr[HIST:HIST + seq_len, :] = u_i
        ext = ext_scr[...]
        pooled_rows.append(_pooled(ext, u_i, pos_col))
        for j in range(POOL_BUF):
            pool_out[0, j, i:i + 1, :] = ext[seq_len + 1 + j:seq_len + 2 + j, :]
    mixed = _pool_mix([jnp.concatenate(p, axis=0) for p in zip(*pooled_rows)], w.pool)
    y_a = mixed * w.pscale[...] * _silu(proj_ref[:, OFF_ZA:OFF_ZA + 1024])
    ycat_scr[:, 0:1024] = y_a.astype(BF16)

    lane4 = lax.broadcasted_iota(jnp.int32, (1, N_HEADS), 1)
    last = [(i + 1) * seq_len - 1 for i in range(G)]
    m_new_heads = [None] * N_HEADS

    def head(h):
        hc = slice(h * 256, (h + 1) * 256)
        q = proj_ref[:, OFF_Q + h * 256:OFF_Q + (h + 1) * 256]
        k = proj_ref[:, OFF_K + h * 256:OFF_K + (h + 1) * 256]
        v = proj_ref[:, OFF_V + h * 256:OFF_V + (h + 1) * 256]
        qb, kb, vb = q.astype(BF16), k.astype(BF16), v.astype(BF16)
        c_old = [c_ref[0, i, h] for i in range(G)]
        c_old_b = [c.astype(BF16) for c in c_old]
        yield
        q_cs = [_dot(qb, cb) for cb in c_old_b]
        ig_col = g_col[:, h:h + 1]
        b_col = b_col_all[:, 4 + h:5 + h]
        r_row = r_row_all[h:h + 1, :]
        m_prev = [m_ref[0, i:i + 1, h:h + 1] for i in range(G)]
        a_col = b_col + _expand(seq_col, m_prev)
        m_t, inter, s = _intra_weights(qb, kb, b_col, a_col, r_row, causal)
        sb = s.astype(BF16)
        yield
        sv = _dot(sb, vb)
        n_old = [n_ref[0, i, h:h + 1, :] for i in range(G)]
        num = inter * _expand(seq_col, q_cs) + sv
        qn = inter * jnp.sum(q * _expand(seq_col, n_old), axis=-1, keepdims=True) + jnp.sum(
            s, axis=-1, keepdims=True)
        ht = num * (1.0 / jnp.maximum(jnp.abs(qn), jnp.exp(-m_t)))
        m_new = [m_t[r:r + 1, :] for r in last]
        b_last = [b_col[r:r + 1, :] for r in last]
        w_end = jnp.exp(_expand(seq_col, b_last) - b_col + ig_col - _expand(seq_col, m_new)) * K_SCALE
        kw = k * w_end
        kwb = kw.astype(BF16)
        v_seq = [jnp.where(seq_col == i, vb, jnp.zeros_like(vb)) for i in range(G)]
        yield
        updates = [_dot_tn(kwb, v_i) for v_i in v_seq]
        for i in range(G):
            decay = jnp.exp(b_last[i] + m_prev[i] - m_new[i])
            c_out[0, i, h] = decay * c_old[i] + updates[i]
            n_out[0, i, h:h + 1, :] = decay * n_old[i] + jnp.sum(
                kw[i * seq_len:(i + 1) * seq_len, :], axis=0, keepdims=True)
        m_new_heads[h] = m_new
        hn = _head_norm(ht, w.mhln[:, hc])
        o = proj_ref[:, OFF_O + h * 256:OFF_O + (h + 1) * 256]
        zb = proj_ref[:, OFF_ZB + h * 256:OFF_ZB + (h + 1) * 256]
        ycat_scr[:, 1024 + h * 256:1024 + (h + 1) * 256] = (hn * jax.nn.sigmoid(o) * _silu(zb)).astype(BF16)

    running = [head(h) for h in range(N_HEADS)]
    while running:
        running = [g for g in running if next(g, g) is not g]
    for i in range(G):
        row = jnp.zeros((1, N_HEADS), F32)
        for h in range(N_HEADS):
            row = jnp.where(lane4 == h, m_new_heads[h][i], row)
        m_out[0, i:i + 1, :] = row

    y = _dot(ycat_scr[...], w.out[...])
    y_ref[...] = _rms(x + y, w.normf[...])


def _const_spec(shape):
    nd = len(shape)
    return pl.BlockSpec(shape, lambda *_: (0,) * nd, pipeline_mode=pl.Buffered(1))


def _params(sem):
    return pltpu.CompilerParams(dimension_semantics=sem, vmem_limit_bytes=VMEM_LIMIT)


def _tile_scratch(T):
    return [pltpu.VMEM((T, D_MAIN), F32), pltpu.VMEM((T, D_MODEL), F32), pltpu.VMEM((T, GATE_PAD), F32),
            pltpu.VMEM((T, 2048), BF16), pltpu.VMEM((T, D_MODEL), BF16)]


def _state_scratch():
    return [pltpu.VMEM((HIST, D_POOL), F32), pltpu.VMEM((N_HEADS, HEAD_DIM, HEAD_DIM), F32),
            pltpu.VMEM((8, HEAD_DIM), F32), pltpu.VMEM((8, 128), F32)]


def kernel(x_prompt, x_sample, state_pool, state_C, state_n, state_m, meta_tokens, norm1_w, w_in,
           b_if, w_pool, pool_scale, mhln_w, w_out, normf_w):
    B, S, _ = x_prompt.shape
    SB, SL, _ = x_sample.shape
    TT = PROMPT_TILE
    assert norm1_w.shape[0] == 1, "single layer"
    assert STEP_TILES % 2 == 0 and S % (STEP_TILES * TT) == 0 and SB % SAMPLE_GROUP == 0

    w_in_t = jnp.swapaxes(w_in[0], 0, 1)
    norm1 = norm1_w[0].reshape(1, D_MODEL)
    n_tok = SB * SL
    xs = x_sample.reshape(n_tok, D_MODEL)
    n_seg = D_MAIN // 1024
    wout_rows = w_out.shape[1] // (n_seg + 1)
    seg = lambda i: jnp.minimum(i, n_seg - 1)
    w_main, w_gate, proj_s, wout = pl.pallas_call(
        functools.partial(_prep_kernel, n_seg=n_seg),
        grid=(n_seg + 1,),
        in_specs=[pl.BlockSpec((1024, D_MODEL), lambda i: (seg(i), 0)),
                  pl.BlockSpec((8, D_MODEL), lambda i: (D_MAIN // 8, 0)),
                  _const_spec(xs.shape), _const_spec(norm1.shape),
                  pl.BlockSpec((wout_rows, D_MODEL), lambda i: (i, 0))],
        out_specs=(pl.BlockSpec((D_MODEL, 1024), lambda i: (0, seg(i))),
                   pl.BlockSpec((D_MODEL, GATE_PAD), lambda i: (0, 0)),
                   pl.BlockSpec((n_tok, 1024), lambda i: (0, seg(i))),
                   pl.BlockSpec((wout_rows, D_MODEL), lambda i: (i, 0))),
        out_shape=(jax.ShapeDtypeStruct((D_MODEL, D_MAIN), BF16),
                   jax.ShapeDtypeStruct((D_MODEL, GATE_PAD), BF16),
                   jax.ShapeDtypeStruct((n_tok, D_MAIN), F32),
                   jax.ShapeDtypeStruct(w_out.shape[1:], BF16)),
        scratch_shapes=[pltpu.VMEM((n_tok, D_MODEL), BF16)],
        compiler_params=_params(("arbitrary",)),
        name="prep",
    )(w_in_t, w_in_t, xs, norm1, w_out[0])
    bias_row = jnp.pad(b_if[0], (0, GATE_PAD - 8)).reshape(1, GATE_PAD)
    wpool = w_pool[0].astype(BF16)
    pscale = pool_scale[0].reshape(1, D_POOL)
    mhln = mhln_w[0].reshape(1, D_MLSTM)
    normf = normf_w.reshape(1, D_MODEL)
    layer_w = LayerW(w_main, w_gate, bias_row, norm1, wpool, pscale, mhln, wout, normf)
    layer_specs = [_const_spec(a.shape) for a in layer_w]

    n_tiles = B * S // TT
    steps = n_tiles // STEP_TILES
    steps_per_seq = S // (STEP_TILES * TT)
    xp = x_prompt.reshape(B * S, D_MODEL)
    y_p, pool_p, c_p, n_p, m_p = pl.pallas_call(
        functools.partial(_prompt_kernel, steps_per_seq=steps_per_seq),
        grid=(steps,),
        in_specs=[_const_spec(meta_tokens.shape),
                  pl.BlockSpec((TT, D_MODEL), lambda s: (0, 0))]
                 + [pl.BlockSpec((TT, D_MODEL),
                                 lambda s, j=j: (jnp.minimum(STEP_TILES * s + 1 + j, n_tiles - 1), 0))
                    for j in range(STEP_TILES)]
                 + layer_specs,
        out_specs=(pl.BlockSpec((STEP_TILES * TT, D_MODEL), lambda s: (s, 0)),
                   pl.BlockSpec((1, POOL_BUF, B, D_POOL), lambda s: (0, 0, 0, 0)),
                   pl.BlockSpec((1, 1, N_HEADS, HEAD_DIM, HEAD_DIM),
                                lambda s: (0, s // steps_per_seq, 0, 0, 0)),
                   pl.BlockSpec((1, 1, N_HEADS, HEAD_DIM), lambda s: (0, s // steps_per_seq, 0, 0)),
                   pl.BlockSpec((B, 128), lambda s: (0, 0))),
        out_shape=(jax.ShapeDtypeStruct((B * S, D_MODEL), F32),
                   jax.ShapeDtypeStruct((1, POOL_BUF, B, D_POOL), F32),
                   jax.ShapeDtypeStruct((1, B, N_HEADS, HEAD_DIM, HEAD_DIM), F32),
                   jax.ShapeDtypeStruct((1, B, N_HEADS, HEAD_DIM), F32),
                   jax.ShapeDtypeStruct((B, 128), F32)),
        scratch_shapes=_tile_scratch(TT) + _tile_scratch(TT) + _tile_scratch(N_META)
        + _state_scratch() + _state_scratch(),
        compiler_params=_params(("arbitrary",)),
        name="prompt",
    )(meta_tokens, xp, *([xp] * STEP_TILES), *layer_w)
    y_prompt = y_p.reshape(B, S, D_MODEL)
    pool_p = jnp.swapaxes(pool_p, 1, 2)
    m_p = m_p[:, :N_HEADS].reshape(1, B, N_HEADS)

    G = SAMPLE_GROUP
    GT = G * SL
    pool_in = jnp.swapaxes(state_pool, 1, 2)
    sample_w = layer_w[1:]
    y_s, pool_s, c_s, n_s, m_s = pl.pallas_call(
        functools.partial(_sample_kernel, seq_len=SL, pos0=PAST_LEN),
        grid=(SB // G,),
        in_specs=[pl.BlockSpec((GT, D_MODEL), lambda i: (i, 0)),
                  pl.BlockSpec((GT, D_MAIN), lambda i: (i, 0)),
                  pl.BlockSpec((1, POOL_BUF, G, D_POOL), lambda i: (0, 0, i, 0)),
                  pl.BlockSpec((1, G, N_HEADS, HEAD_DIM, HEAD_DIM), lambda i: (0, i, 0, 0, 0)),
                  pl.BlockSpec((1, G, N_HEADS, HEAD_DIM), lambda i: (0, i, 0, 0)),
                  pl.BlockSpec((1, G, N_HEADS), lambda i: (0, i, 0))]
                 + [_const_spec(a.shape) for a in sample_w],
        out_specs=(pl.BlockSpec((GT, D_MODEL), lambda i: (i, 0)),
                   pl.BlockSpec((1, POOL_BUF, G, D_POOL), lambda i: (0, 0, i, 0)),
                   pl.BlockSpec((1, G, N_HEADS, HEAD_DIM, HEAD_DIM), lambda i: (0, i, 0, 0, 0)),
                   pl.BlockSpec((1, G, N_HEADS, HEAD_DIM), lambda i: (0, i, 0, 0)),
                   pl.BlockSpec((1, G, N_HEADS), lambda i: (0, i, 0))),
        out_shape=(jax.ShapeDtypeStruct((n_tok, D_MODEL), F32),
                   jax.ShapeDtypeStruct(pool_in.shape, F32),
                   jax.ShapeDtypeStruct(state_C.shape, F32),
                   jax.ShapeDtypeStruct(state_n.shape, F32),
                   jax.ShapeDtypeStruct(state_m.shape, F32)),
        scratch_shapes=[pltpu.VMEM((HIST + SL, D_POOL), F32), pltpu.VMEM((GT, 2048), BF16)],
        compiler_params=_params(("arbitrary",)),
        name="sample",
    )(xs, proj_s, pool_in, state_C, state_n, state_m, *sample_w)
    y_sample = y_s.reshape(SB, SL, D_MODEL)
    pool_s = jnp.swapaxes(pool_s, 1, 2)

    return (y_prompt, y_sample, pool_p, c_p, n_p, m_p, pool_s, c_s, n_s, m_s)
```

```python
import collections
import functools

import jax
import jax.numpy as jnp
from jax import lax
from jax.experimental import pallas as pl
from jax.experimental.pallas import tpu as pltpu

D_MODEL = 1024
D_POOL = 1024
D_MLSTM = 1024
N_HEADS = 4
HEAD_DIM = 256
POOL_WINDOWS = (2, 4, 8, 16)
POOL_BUF = 15
HIST = 16
N_META = 16
PAST_LEN = 16384
EPS = 1e-6
D_MAIN = 2 * D_POOL + 5 * D_MLSTM
GATE_PAD = 128
K_SCALE = HEAD_DIM ** -0.5

OFF_U, OFF_ZA, OFF_Q, OFF_K, OFF_V, OFF_O, OFF_ZB = (i * 1024 for i in range(7))

PROMPT_TILE = 256
STEP_TILES = 2
SAMPLE_GROUP = 8
VMEM_LIMIT = 60000 * 1024

F32 = jnp.float32
BF16 = jnp.bfloat16

LayerW = collections.namedtuple("LayerW", "main gate bias norm1 pool pscale mhln out normf")


def _dot(a, b):
    return jnp.dot(a, b, preferred_element_type=F32)


def _dot_nt(a, b):
    return lax.dot_general(a, b, (((1,), (1,)), ((), ())), preferred_element_type=F32)


def _dot_tn(a, b):
    return lax.dot_general(a, b, (((0,), (0,)), ((), ())), preferred_element_type=F32)


def _rms(x, w):
    return x * lax.rsqrt(jnp.mean(x * x, axis=-1, keepdims=True) + EPS) * w


def _log_sigmoid(x):
    return jnp.minimum(x, 0.0) - jnp.log1p(jnp.exp(-jnp.abs(x)))


def _silu(x):
    return x * jax.nn.sigmoid(x)


def _split3(x):
    hi = x.astype(BF16)
    r = x - hi.astype(F32)
    mid = r.astype(BF16)
    lo = (r - mid.astype(F32)).astype(BF16)
    return hi, mid, lo


def _seq_mask(T, seq_len):
    row = lax.broadcasted_iota(jnp.int32, (T, T), 0)
    col = lax.broadcasted_iota(jnp.int32, (T, T), 1)
    causal = col <= row
    if seq_len < T:
        shift = seq_len.bit_length() - 1
        assert 1 << shift == seq_len
        causal = causal & ((row >> shift) == (col >> shift))
    return causal


def _to_rows(cols):
    T = cols.shape[0]
    pad = -T % 128
    if pad:
        cols = jnp.concatenate([cols, jnp.zeros((pad, cols.shape[1]), cols.dtype)], axis=0)
    return cols.T[:, 0:T]


def _gate_pre(xn, w):
    return _dot(xn, w.gate[...]) + w.bias[...]


def _cumsum_operands(g_col, causal):
    tri = jnp.where(causal, 1.0, 0.0).astype(BF16)
    return tri, _split3(_log_sigmoid(g_col))


def _cumsum(tri, parts):
    return sum(_dot(tri, p) for p in parts)


def _intra_weights(qb, kb, b_col, a_col, r_row, causal):
    dm = jnp.where(causal, b_col + r_row, -jnp.inf)
    m_t = jnp.maximum(a_col, jnp.max(dm, axis=-1, keepdims=True))
    w = jnp.exp(dm - m_t)
    inter = jnp.exp(a_col - m_t)
    return m_t, inter, _dot_nt(qb, kb) * (w * K_SCALE)


def _head_norm(ht, w_row):
    mu = jnp.mean(ht, axis=-1, keepdims=True)
    d = ht - mu
    var = jnp.mean(d * d, axis=-1, keepdims=True)
    return d * lax.rsqrt(var + EPS) * w_row


def _window_sums(ext):
    s2 = ext + pltpu.roll(ext, 1, axis=0)
    s4 = s2[:, 256:] + pltpu.roll(s2[:, 256:], 2, axis=0)
    s8 = s4[:, 256:] + pltpu.roll(s4[:, 256:], 4, axis=0)
    s16 = s8[:, 256:] + pltpu.roll(s8[:, 256:], 8, axis=0)
    return [s2[HIST:, 0:256], s4[HIST:, 0:256], s8[HIST:, 0:256], s16[HIST:, 0:256]]


def _pooled(ext, u, pos_col):
    sums = _window_sums(ext)
    return [sums[g] * (1.0 / jnp.minimum(float(w), pos_col + 1.0)) - u[:, g * 256:(g + 1) * 256]
            for g, w in enumerate(POOL_WINDOWS)]


def _pool_mix(pooled, wpool_ref):
    return jnp.concatenate([_dot(p.astype(BF16), wpool_ref[g]) for g, p in enumerate(pooled)], axis=-1)


def _inproj_steps(x_ref, scr, w):
    proj_scr, x_scr, gcol_scr, _, xn_scr = scr

    def norm():
        x = x_ref[...]
        x_scr[...] = x
        xn_scr[...] = _rms(x, w.norm1[...]).astype(BF16)

    def piece(c0):
        act = {OFF_ZA: _silu, OFF_O: jax.nn.sigmoid, OFF_ZB: _silu}.get(c0 // 1024 * 1024, lambda p: p)

        def run():
            proj_scr[:, c0:c0 + PIECE_COLS] = act(_dot(xn_scr[...], w.main[:, c0:c0 + PIECE_COLS]))
        return run

    def gates():
        gcol_scr[...] = _gate_pre(xn_scr[...], w)

    return [norm] + [piece(c0) for c0 in range(0, D_MAIN, PIECE_COLS)] + [gates]


def _mid_steps(scr, pos0, w, state):
    proj_scr, x_scr, gcol_scr, ycat_scr, _ = scr
    hist_scr, c_scr, n_scr, m_scr = state
    T = x_scr.shape[0]

    def seg(off, h=None):
        if h is None:
            return proj_scr[:, off:off + 1024]
        return proj_scr[:, off + h * 256:off + (h + 1) * 256]

    causal = _seq_mask(T, T)
    g_col = gcol_scr[...]
    tri, lf_parts = _cumsum_operands(g_col, causal)
    yield
    b_col_all = _cumsum(tri, lf_parts)
    r_row_all = _to_rows(g_col - pltpu.roll(b_col_all, GATE_PAD - N_HEADS, axis=1))

    u = seg(OFF_U)
    ext = jnp.concatenate([hist_scr[...], u], axis=0)
    pos_col = (lax.broadcasted_iota(jnp.int32, (T, 1), 0) + pos0).astype(F32)
    pooled = _pooled(ext, u, pos_col)
    hist_scr[...] = ext[T:T + HIST, :]
    yield
    mixed = _pool_mix(pooled, w.pool)
    ycat_scr[:, 0:1024] = (mixed * w.pscale[...] * seg(OFF_ZA)).astype(BF16)

    def head(h):
        q, k, v = seg(OFF_Q, h), seg(OFF_K, h), seg(OFF_V, h)
        qb, kb, vb = q.astype(BF16), k.astype(BF16), v.astype(BF16)
        ig_col = g_col[:, h:h + 1]
        b_col = b_col_all[:, 4 + h:5 + h]
        r_row = r_row_all[h:h + 1, :]
        m_prev = m_scr[h:h + 1, 0:1]
        a_col = b_col + m_prev
        c_old = c_scr[h]
        n_old = n_scr[h:h + 1, :]
        q_c = _dot(qb, c_old.astype(BF16))
        m_t, inter, s = _intra_weights(qb, kb, b_col, a_col, r_row, causal)
        sb = s.astype(BF16)
        yield
        num = inter * q_c + _dot(sb, vb)
        qn = inter * jnp.sum(q * n_old, axis=-1, keepdims=True) + jnp.sum(s, axis=-1, keepdims=True)
        ht = num * (1.0 / jnp.maximum(jnp.abs(qn), jnp.exp(-m_t)))
        m_new = m_t[T - 1:T, :]
        b_last = b_col[T - 1:T, :]
        w_end = jnp.exp(b_last - b_col + ig_col - m_new) * K_SCALE
        decay = jnp.exp(b_last + m_prev - m_new)
        kw = k * w_end
        kwb = kw.astype(BF16)
        yield
        c_scr[h] = decay * c_old + _dot_tn(kwb, vb)
        n_scr[h:h + 1, :] = decay * n_old + jnp.sum(kw, axis=0, keepdims=True)
        m_scr[h:h + 1, :] = jnp.broadcast_to(m_new, (1, 128))
        hn = _head_norm(ht, w.mhln[:, h * 256:(h + 1) * 256])
        ycat_scr[:, 1024 + h * 256:1024 + (h + 1) * 256] = (hn * seg(OFF_O, h) * seg(OFF_ZB, h)).astype(BF16)

    for pair in range(0, N_HEADS, HEADS_IN_FLIGHT):
        running = [head(h) for h in range(pair, pair + HEADS_IN_FLIGHT)]
        while running:
            alive = []
            for g in running:
                if next(g, g) is not g:
                    alive.append(g)
                yield
            running = alive


def _out_steps(scr, w, y_ref):
    _, x_scr, _, ycat_scr, _ = scr

    def residual():
        y_ref[...] = x_scr[...]

    def piece(c0):
        def run():
            cols = slice(c0, c0 + OUT_COLS)
            y_ref[:, cols] = y_ref[:, cols] + _dot(ycat_scr[...], w.out[:, cols])
        return run

    def norm():
        y_ref[...] = _rms(y_ref[...], w.normf[...])

    return [residual] + [piece(c0) for c0 in range(0, D_MODEL, OUT_COLS)] + [norm]


PIECE_COLS = 256
OUT_COLS = 256
HEADS_IN_FLIGHT = 2
MID_YIELDS = 2 + 3 * N_HEADS


def _run_interleaved(mid, lead, at_yield):
    assert len(at_yield) == MID_YIELDS
    for piece in lead:
        piece()
    for pieces in at_yield:
        next(mid)
        for piece in pieces:
            piece()
    for _ in mid:
        raise AssertionError("unexpected extra yield")


def _spread(pieces, yields):
    base, extra = divmod(len(pieces), yields)
    groups, start = [], 0
    for i in range(yields):
        stop = start + base + (i < extra)
        groups.append(pieces[start:stop])
        start = stop
    return groups


def _m_row(m_scr):
    lane = lax.broadcasted_iota(jnp.int32, (1, 128), 1)
    row = jnp.zeros((1, 128), F32)
    for h in range(N_HEADS):
        row = jnp.where(lane == h, m_scr[h:h + 1, :], row)
    return row


def _prompt_kernel(meta_ref, x0_ref, *refs, steps_per_seq):
    x_next = refs[:STEP_TILES]
    refs = refs[STEP_TILES:]
    w = LayerW(*refs[:9])
    y_ref, pool_out, c_out, n_out, m_out = refs[9:14]
    scr = (refs[14:19], refs[19:24])
    scr0, scr_meta = scr[0], refs[24:29]
    state, state_meta = refs[29:33], refs[33:37]
    hist_scr, c_scr, n_scr, m_scr = state
    s = pl.program_id(0)
    TT = PROMPT_TILE

    @pl.when(s == 0)
    def _():
        for ref in state_meta:
            ref[...] = jnp.zeros_like(ref)
        for p in _inproj_steps(meta_ref, scr_meta, w):
            p()
        for _ in _mid_steps(scr_meta, 0, w, state_meta):
            pass
        for p in _inproj_steps(x0_ref, scr0, w):
            p()

    @pl.when(s % steps_per_seq == 0)
    def _():
        for ref, ref_meta in zip(state, state_meta):
            ref[...] = ref_meta[...]

    out_pieces = []
    for j in range(STEP_TILES):
        norm, *proj = _inproj_steps(x_next[j], scr[1 - j % 2], w)
        if out_pieces:
            residual, o0, o1, o2, o3, out_norm = out_pieces
            lead = [residual, o0, norm, o1]
            at_yield = [[o2] + proj[0:2], [o3] + proj[2:4]] + _spread(proj[4:], MID_YIELDS - 2)
            at_yield[3] = at_yield[3] + [out_norm]
        else:
            lead = [norm, proj[0]]
            at_yield = _spread(proj[1:], MID_YIELDS)
        _run_interleaved(_mid_steps(scr[j % 2], N_META, w, state), lead, at_yield)
        out_pieces = _out_steps(scr[j % 2], w, y_ref.at[j * TT:(j + 1) * TT, :])
    for piece in out_pieces:
        piece()

    @pl.when(s % steps_per_seq == steps_per_seq - 1)
    def _():
        b = s // steps_per_seq
        for j in range(POOL_BUF):
            pool_out[0, j, pl.ds(b, 1), :] = hist_scr[1 + j:2 + j, :]
        c_out[0, 0] = c_scr[...]
        n_out[0, 0] = n_scr[0:N_HEADS, :]
        m_out[pl.ds(b, 1), :] = _m_row(m_scr)


def _prep_kernel(wt_ref, wgt_ref, xs_ref, norm1_ref, wout_ref, wmain_ref, wgate_ref, proj_ref, woutb_ref,
                 xn_scr, *, n_seg):
    i = pl.program_id(0)

    @pl.when(i == 0)
    def _():
        g = jnp.concatenate([wgt_ref[...], jnp.zeros((GATE_PAD - 8, D_MODEL), F32)], axis=0)
        wgate_ref[...] = g.T.astype(BF16)
        xn_scr[...] = _rms(xs_ref[...], norm1_ref[...]).astype(BF16)

    woutb_ref[...] = wout_ref[...].astype(BF16)

    @pl.when(i < n_seg)
    def _():
        wb = wt_ref[...].T.astype(BF16)
        wmain_ref[...] = wb
        proj_ref[...] = _dot(xn_scr[...], wb)


def _expand(seq_col, vals):
    out = None
    for i, val in enumerate(vals):
        pick = jnp.where(seq_col == i, val, 0.0)
        out = pick if out is None else out + pick
    return out


def _sample_kernel(x_ref, proj_ref, pool_ref, c_ref, n_ref, m_ref, *refs, seq_len, pos0):
    w = LayerW(None, *refs[:8])
    y_ref, pool_out, c_out, n_out, m_out = refs[8:13]
    ext_scr, ycat_scr = refs[13:15]
    G = SAMPLE_GROUP
    T = G * seq_len
    x = x_ref[...]
    xn = _rms(x, w.norm1[...]).astype(BF16)
    causal = _seq_mask(T, seq_len)
    g_col = _gate_pre(xn, w)
    b_col_all = _cumsum(*_cumsum_operands(g_col, causal))
    r_row_all = _to_rows(g_col - pltpu.roll(b_col_all, GATE_PAD - N_HEADS, axis=1))
    seq_col = lax.broadcasted_iota(jnp.int32, (T, 1), 0) >> (seq_len.bit_length() - 1)
    pos_col = jnp.full((seq_len, 1), float(pos0), F32) + lax.broadcasted_iota(
        jnp.int32, (seq_len, 1), 0).astype(F32)

    pooled_rows = []
    for i in range(G):
        rows = slice(i * seq_len, (i + 1) * seq_len)
        u_i = proj_ref[rows, OFF_U:OFF_U + 1024]
        ext_scr[0:1, :] = jnp.zeros((1, 1024), F32)
        for j in range(POOL_BUF):
            ext_scr[1 + j:2 + j, :] = pool_ref[0, j, i:i + 1, :]
        ext_scr[HIST:HIST + seq_len, :] = u_i
        ext = ext_scr[...]
        pooled_rows.append(_pooled(ext, u_i, pos_col))
        for j in range(POOL_BUF):
            pool_out[0, j, i:i + 1, :] = ext[seq_len + 1 + j:seq_len + 2 + j, :]
    mixed = _pool_mix([jnp.concatenate(p, axis=0) for p in zip(*pooled_rows)], w.pool)
    y_a = mixed * w.pscale[...] * _silu(proj_ref[:, OFF_ZA:OFF_ZA + 1024])
    ycat_scr[:, 0:1024] = y_a.astype(BF16)

    lane4 = lax.broadcasted_iota(jnp.int32, (1, N_HEADS), 1)
    last = [(i + 1) * seq_len - 1 for i in range(G)]
    m_new_heads = [None] * N_HEADS

    def head(h):
        hc = slice(h * 256, (h + 1) * 256)
        q = proj_ref[:, OFF_Q + h * 256:OFF_Q + (h + 1) * 256]
        k = proj_ref[:, OFF_K + h * 256:OFF_K + (h + 1) * 256]
        v = proj_ref[:, OFF_V + h * 256:OFF_V + (h + 1) * 256]
        qb, kb, vb = q.astype(BF16), k.astype(BF16), v.astype(BF16)
        c_old = [c_ref[0, i, h] for i in range(G)]
        c_old_b = [c.astype(BF16) for c in c_old]
        yield
        q_cs = [_dot(qb, cb) for cb in c_old_b]
        ig_col = g_col[:, h:h + 1]
        b_col = b_col_all[:, 4 + h:5 + h]
        r_row = r_row_all[h:h + 1, :]
        m_prev = [m_ref[0, i:i + 1, h:h + 1] for i in range(G)]
        a_col = b_col + _expand(seq_col, m_prev)
        m_t, inter, s = _intra_weights(qb, kb, b_col, a_col, r_row, causal)
        sb = s.astype(BF16)
        yield
        sv = _dot(sb, vb)
        n_old = [n_ref[0, i, h:h + 1, :] for i in range(G)]
        num = inter * _expand(seq_col, q_cs) + sv
        qn = inter * jnp.sum(q * _expand(seq_col, n_old), axis=-1, keepdims=True) + jnp.sum(
            s, axis=-1, keepdims=True)
        ht = num * (1.0 / jnp.maximum(jnp.abs(qn), jnp.exp(-m_t)))
        m_new = [m_t[r:r + 1, :] for r in last]
        b_last = [b_col[r:r + 1, :] for r in last]
        w_end = jnp.exp(_expand(seq_col, b_last) - b_col + ig_col - _expand(seq_col, m_new)) * K_SCALE
        kw = k * w_end
        kwb = kw.astype(BF16)
        v_seq = [jnp.where(seq_col == i, vb, jnp.zeros_like(vb)) for i in range(G)]
        yield
        updates = [_dot_tn(kwb, v_i) for v_i in v_seq]
        for i in range(G):
            decay = jnp.exp(b_last[i] + m_prev[i] - m_new[i])
            c_out[0, i, h] = decay * c_old[i] + updates[i]
            n_out[0, i, h:h + 1, :] = decay * n_old[i] + jnp.sum(
                kw[i * seq_len:(i + 1) * seq_len, :], axis=0, keepdims=True)
        m_new_heads[h] = m_new
        hn = _head_norm(ht, w.mhln[:, hc])
        o = proj_ref[:, OFF_O + h * 256:OFF_O + (h + 1) * 256]
        zb = proj_ref[:, OFF_ZB + h * 256:OFF_ZB + (h + 1) * 256]
        ycat_scr[:, 1024 + h * 256:1024 + (h + 1) * 256] = (hn * jax.nn.sigmoid(o) * _silu(zb)).astype(BF16)

    running = [head(h) for h in range(N_HEADS)]
    while running:
        running = [g for g in running if next(g, g) is not g]
    for i in range(G):
        row = jnp.zeros((1, N_HEADS), F32)
        for h in range(N_HEADS):
            row = jnp.where(lane4 == h, m_new_heads[h][i], row)
        m_out[0, i:i + 1, :] = row

    y = _dot(ycat_scr[...], w.out[...])
    y_ref[...] = _rms(x + y, w.normf[...])


def _const_spec(shape):
    nd = len(shape)
    return pl.BlockSpec(shape, lambda *_: (0,) * nd, pipeline_mode=pl.Buffered(1))


def _params(sem):
    return pltpu.CompilerParams(dimension_semantics=sem, vmem_limit_bytes=VMEM_LIMIT)


def _tile_scratch(T):
    return [pltpu.VMEM((T, D_MAIN), F32), pltpu.VMEM((T, D_MODEL), F32), pltpu.VMEM((T, GATE_PAD), F32),
            pltpu.VMEM((T, 2048), BF16), pltpu.VMEM((T, D_MODEL), BF16)]


def _state_scratch():
    return [pltpu.VMEM((HIST, D_POOL), F32), pltpu.VMEM((N_HEADS, HEAD_DIM, HEAD_DIM), F32),
            pltpu.VMEM((8, HEAD_DIM), F32), pltpu.VMEM((8, 128), F32)]


def kernel(x_prompt, x_sample, state_pool, state_C, state_n, state_m, meta_tokens, norm1_w, w_in,
           b_if, w_pool, pool_scale, mhln_w, w_out, normf_w):
    B, S, _ = x_prompt.shape
    SB, SL, _ = x_sample.shape
    TT = PROMPT_TILE
    assert norm1_w.shape[0] == 1, "single layer"
    assert STEP_TILES % 2 == 0 and S % (STEP_TILES * TT) == 0 and SB % SAMPLE_GROUP == 0

    w_in_t = jnp.swapaxes(w_in[0], 0, 1)
    norm1 = norm1_w[0].reshape(1, D_MODEL)
    n_tok = SB * SL
    xs = x_sample.reshape(n_tok, D_MODEL)
    n_seg = D_MAIN // 1024
    wout_rows = w_out.shape[1] // (n_seg + 1)
    seg = lambda i: jnp.minimum(i, n_seg - 1)
    w_main, w_gate, proj_s, wout = pl.pallas_call(
        functools.partial(_prep_kernel, n_seg=n_seg),
        grid=(n_seg + 1,),
        in_specs=[pl.BlockSpec((1024, D_MODEL), lambda i: (seg(i), 0)),
                  pl.BlockSpec((8, D_MODEL), lambda i: (D_MAIN // 8, 0)),
                  _const_spec(xs.shape), _const_spec(norm1.shape),
                  pl.BlockSpec((wout_rows, D_MODEL), lambda i: (i, 0))],
        out_specs=(pl.BlockSpec((D_MODEL, 1024), lambda i: (0, seg(i))),
                   pl.BlockSpec((D_MODEL, GATE_PAD), lambda i: (0, 0)),
                   pl.BlockSpec((n_tok, 1024), lambda i: (0, seg(i))),
                   pl.BlockSpec((wout_rows, D_MODEL), lambda i: (i, 0))),
        out_shape=(jax.ShapeDtypeStruct((D_MODEL, D_MAIN), BF16),
                   jax.ShapeDtypeStruct((D_MODEL, GATE_PAD), BF16),
                   jax.ShapeDtypeStruct((n_tok, D_MAIN), F32),
                   jax.ShapeDtypeStruct(w_out.shape[1:], BF16)),
        scratch_shapes=[pltpu.VMEM((n_tok, D_MODEL), BF16)],
        compiler_params=_params(("arbitrary",)),
        name="prep",
    )(w_in_t, w_in_t, xs, norm1, w_out[0])
    bias_row = jnp.pad(b_if[0], (0, GATE_PAD - 8)).reshape(1, GATE_PAD)
    wpool = w_pool[0].astype(BF16)
    pscale = pool_scale[0].reshape(1, D_POOL)
    mhln = mhln_w[0].reshape(1, D_MLSTM)
    normf = normf_w.reshape(1, D_MODEL)
    layer_w = LayerW(w_main, w_gate, bias_row, norm1, wpool, pscale, mhln, wout, normf)
    layer_specs = [_const_spec(a.shape) for a in layer_w]

    n_tiles = B * S // TT
    steps = n_tiles // STEP_TILES
    steps_per_seq = S // (STEP_TILES * TT)
    xp = x_prompt.reshape(B * S, D_MODEL)
    y_p, pool_p, c_p, n_p, m_p = pl.pallas_call(
        functools.partial(_prompt_kernel, steps_per_seq=steps_per_seq),
        grid=(steps,),
        in_specs=[_const_spec(meta_tokens.shape),
                  pl.BlockSpec((TT, D_MODEL), lambda s: (0, 0))]
                 + [pl.BlockSpec((TT, D_MODEL),
                                 lambda s, j=j: (jnp.minimum(STEP_TILES * s + 1 + j, n_tiles - 1), 0))
                    for j in range(STEP_TILES)]
                 + layer_specs,
        out_specs=(pl.BlockSpec((STEP_TILES * TT, D_MODEL), lambda s: (s, 0)),
                   pl.BlockSpec((1, POOL_BUF, B, D_POOL), lambda s: (0, 0, 0, 0)),
                   pl.BlockSpec((1, 1, N_HEADS, HEAD_DIM, HEAD_DIM),
                                lambda s: (0, s // steps_per_seq, 0, 0, 0)),
                   pl.BlockSpec((1, 1, N_HEADS, HEAD_DIM), lambda s: (0, s // steps_per_seq, 0, 0)),
                   pl.BlockSpec((B, 128), lambda s: (0, 0))),
        out_shape=(jax.ShapeDtypeStruct((B * S, D_MODEL), F32),
                   jax.ShapeDtypeStruct((1, POOL_BUF, B, D_POOL), F32),
                   jax.ShapeDtypeStruct((1, B, N_HEADS, HEAD_DIM, HEAD_DIM), F32),
                   jax.ShapeDtypeStruct((1, B, N_HEADS, HEAD_DIM), F32),
                   jax.ShapeDtypeStruct((B, 128), F32)),
        scratch_shapes=_tile_scratch(TT) + _tile_scratch(TT) + _tile_scratch(N_META)
        + _state_scratch() + _state_scratch(),
        compiler_params=_params(("arbitrary",)),
        name="prompt",
    )(meta_tokens, xp, *([xp] * STEP_TILES), *layer_w)
    y_prompt = y_p.reshape(B, S, D_MODEL)
    pool_p = jnp.swapaxes(pool_p, 1, 2)
    m_p = m_p[:, :N_HEADS].reshape(1, B, N_HEADS)

    G = SAMPLE_GROUP
    GT = G * SL
    pool_in = jnp.swapaxes(state_pool, 1, 2)
    sample_w = layer_w[1:]
    y_s, pool_s, c_s, n_s, m_s = pl.pallas_call(
        functools.partial(_sample_kernel, seq_len=SL, pos0=PAST_LEN),
        grid=(SB // G,),
        in_specs=[pl.BlockSpec((GT, D_MODEL), lambda i: (i, 0)),
                  pl.BlockSpec((GT, D_MAIN), lambda i: (i, 0)),
                  pl.BlockSpec((1, POOL_BUF, G, D_POOL), lambda i: (0, 0, i, 0)),
                  pl.BlockSpec((1, G, N_HEADS, HEAD_DIM, HEAD_DIM), lambda i: (0, i, 0, 0, 0)),
                  pl.BlockSpec((1, G, N_HEADS, HEAD_DIM), lambda i: (0, i, 0, 0)),
                  pl.BlockSpec((1, G, N_HEADS), lambda i: (0, i, 0))]
                 + [_const_spec(a.shape) for a in sample_w],
        out_specs=(pl.BlockSpec((GT, D_MODEL), lambda i: (i, 0)),
                   pl.BlockSpec((1, POOL_BUF, G, D_POOL), lambda i: (0, 0, i, 0)),
                   pl.BlockSpec((1, G, N_HEADS, HEAD_DIM, HEAD_DIM), lambda i: (0, i, 0, 0, 0)),
                   pl.BlockSpec((1, G, N_HEADS, HEAD_DIM), lambda i: (0, i, 0, 0)),
                   pl.BlockSpec((1, G, N_HEADS), lambda i: (0, i, 0))),
        out_shape=(jax.ShapeDtypeStruct((n_tok, D_MODEL), F32),
                   jax.ShapeDtypeStruct(pool_in.shape, F32),
                   jax.ShapeDtypeStruct(state_C.shape, F32),
                   jax.ShapeDtypeStruct(state_n.shape, F32),
                   jax.ShapeDtypeStruct(state_m.shape, F32)),
        scratch_shapes=[pltpu.VMEM((HIST + SL, D_POOL), F32), pltpu.VMEM((GT, 2048), BF16)],
        compiler_params=_params(("arbitrary",)),
        name="sample",
    )(xs, proj_s, pool_in, state_C, state_n, state_m, *sample_w)
    y_sample = y_s.reshape(SB, SL, D_MODEL)
    pool_s = jnp.swapaxes(pool_s, 1, 2)

    return (y_prompt, y_sample, pool_p, c_p, n_p, m_p, pool_s, c_s, n_s, m_s)
```

```python
import collections
import functools

import jax
import jax.numpy as jnp
from jax import lax
from jax.experimental import pallas as pl
from jax.experimental.pallas import tpu as pltpu

D_MODEL = 1024
D_POOL = 1024
D_MLSTM = 1024
N_HEADS = 4
HEAD_DIM = 256
POOL_WINDOWS = (2, 4, 8, 16)
POOL_BUF = 15
HIST = 16
N_META = 16
PAST_LEN = 16384
EPS = 1e-6
D_MAIN = 2 * D_POOL + 5 * D_MLSTM
GATE_PAD = 128
K_SCALE = HEAD_DIM ** -0.5

OFF_U, OFF_ZA, OFF_Q, OFF_K, OFF_V, OFF_O, OFF_ZB = (i * 1024 for i in range(7))

PROMPT_TILE = 256
STEP_TILES = 2
SAMPLE_GROUP = 8
VMEM_LIMIT = 60000 * 1024

F32 = jnp.float32
BF16 = jnp.bfloat16

LayerW = collections.namedtuple("LayerW", "main gate bias norm1 pool pscale mhln out normf")


def _dot(a, b):
    return jnp.dot(a, b, preferred_element_type=F32)


def _dot_nt(a, b):
    return lax.dot_general(a, b, (((1,), (1,)), ((), ())), preferred_element_type=F32)


def _dot_tn(a, b):
    return lax.dot_general(a, b, (((0,), (0,)), ((), ())), preferred_element_type=F32)


def _rms(x, w):
    return x * lax.rsqrt(jnp.mean(x * x, axis=-1, keepdims=True) + EPS) * w


def _log_sigmoid(x):
    return jnp.minimum(x, 0.0) - jnp.log1p(jnp.exp(-jnp.abs(x)))


def _silu(x):
    return x * jax.nn.sigmoid(x)


def _split3(x):
    hi = x.astype(BF16)
    r = x - hi.astype(F32)
    mid = r.astype(BF16)
    lo = (r - mid.astype(F32)).astype(BF16)
    return hi, mid, lo


def _seq_mask(T, seq_len):
    row = lax.broadcasted_iota(jnp.int32, (T, T), 0)
    col = lax.broadcasted_iota(jnp.int32, (T, T), 1)
    causal = col <= row
    if seq_len < T:
        shift = seq_len.bit_length() - 1
        assert 1 << shift == seq_len
        causal = causal & ((row >> shift) == (col >> shift))
    return causal


def _to_rows(cols):
    T = cols.shape[0]
    pad = -T % 128
    if pad:
        cols = jnp.concatenate([cols, jnp.zeros((pad, cols.shape[1]), cols.dtype)], axis=0)
    return cols.T[:, 0:T]


def _gate_pre(xn, w):
    return _dot(xn, w.gate[...]) + w.bias[...]


def _cumsum_operands(g_col, causal):
    tri = jnp.where(causal, 1.0, 0.0).astype(BF16)
    return tri, _split3(_log_sigmoid(g_col))


def _cumsum(tri, parts):
    return sum(_dot(tri, p) for p in parts)


def _intra_weights(qb, kb, b_col, a_col, r_row, causal):
    dm = jnp.where(causal, b_col + r_row, -jnp.inf)
    m_t = jnp.maximum(a_col, jnp.max(dm, axis=-1, keepdims=True))
    w = jnp.exp(dm - m_t)
    inter = jnp.exp(a_col - m_t)
    return m_t, inter, _dot_nt(qb, kb) * (w * K_SCALE)


def _head_norm(ht, w_row):
    mu = jnp.mean(ht, axis=-1, keepdims=True)
    d = ht - mu
    var = jnp.mean(d * d, axis=-1, keepdims=True)
    return d * lax.rsqrt(var + EPS) * w_row


def _window_sums(ext):
    s2 = ext + pltpu.roll(ext, 1, axis=0)
    s4 = s2[:, 256:] + pltpu.roll(s2[:, 256:], 2, axis=0)
    s8 = s4[:, 256:] + pltpu.roll(s4[:, 256:], 4, axis=0)
    s16 = s8[:, 256:] + pltpu.roll(s8[:, 256:], 8, axis=0)
    return [s2[HIST:, 0:256], s4[HIST:, 0:256], s8[HIST:, 0:256], s16[HIST:, 0:256]]


def _pooled(ext, u, pos_col):
    sums = _window_sums(ext)
    return [sums[g] * (1.0 / jnp.minimum(float(w), pos_col + 1.0)) - u[:, g * 256:(g + 1) * 256]
            for g, w in enumerate(POOL_WINDOWS)]


def _pool_mix(pooled, wpool_ref):
    return jnp.concatenate([_dot(p.astype(BF16), wpool_ref[g]) for g, p in enumerate(pooled)], axis=-1)


def _inproj_steps(x_ref, scr, w):
    proj_scr, x_scr, gcol_scr, _, xn_scr = scr

    def norm():
        x = x_ref[...]
        x_scr[...] = x
        xn_scr[...] = _rms(x, w.norm1[...]).astype(BF16)

    def piece(c0):
        act = {OFF_ZA: _silu, OFF_O: jax.nn.sigmoid, OFF_ZB: _silu}.get(c0 // 1024 * 1024, lambda p: p)

        def run():
            proj_scr[:, c0:c0 + PIECE_COLS] = act(_dot(xn_scr[...], w.main[:, c0:c0 + PIECE_COLS]))
        return run

    def gates():
        gcol_scr[...] = _gate_pre(xn_scr[...], w)

    return [norm] + [piece(c0) for c0 in range(0, D_MAIN, PIECE_COLS)] + [gates]


def _mid_steps(scr, pos0, w, state):
    proj_scr, x_scr, gcol_scr, ycat_scr, _ = scr
    hist_scr, c_scr, n_scr, m_scr = state
    T = x_scr.shape[0]

    def seg(off, h=None):
        if h is None:
            return proj_scr[:, off:off + 1024]
        return proj_scr[:, off + h * 256:off + (h + 1) * 256]

    causal = _seq_mask(T, T)
    g_col = gcol_scr[...]
    tri, lf_parts = _cumsum_operands(g_col, causal)
    yield
    b_col_all = _cumsum(tri, lf_parts)
    r_row_all = _to_rows(g_col - pltpu.roll(b_col_all, GATE_PAD - N_HEADS, axis=1))

    u = seg(OFF_U)
    ext = jnp.concatenate([hist_scr[...], u], axis=0)
    pos_col = (lax.broadcasted_iota(jnp.int32, (T, 1), 0) + pos0).astype(F32)
    pooled = _pooled(ext, u, pos_col)
    hist_scr[...] = ext[T:T + HIST, :]
    yield
    mixed = _pool_mix(pooled, w.pool)
    ycat_scr[:, 0:1024] = (mixed * w.pscale[...] * seg(OFF_ZA)).astype(BF16)

    def head(h):
        q, k, v = seg(OFF_Q, h), seg(OFF_K, h), seg(OFF_V, h)
        qb, kb, vb = q.astype(BF16), k.astype(BF16), v.astype(BF16)
        ig_col = g_col[:, h:h + 1]
        b_col = b_col_all[:, 4 + h:5 + h]
        r_row = r_row_all[h:h + 1, :]
        m_prev = m_scr[h:h + 1, 0:1]
        a_col = b_col + m_prev
        c_old = c_scr[h]
        n_old = n_scr[h:h + 1, :]
        q_c = _dot(qb, c_old.astype(BF16))
        m_t, inter, s = _intra_weights(qb, kb, b_col, a_col, r_row, causal)
        sb = s.astype(BF16)
        yield
        num = inter * q_c + _dot(sb, vb)
        qn = inter * jnp.sum(q * n_old, axis=-1, keepdims=True) + jnp.sum(s, axis=-1, keepdims=True)
        ht = num * (1.0 / jnp.maximum(jnp.abs(qn), jnp.exp(-m_t)))
        m_new = m_t[T - 1:T, :]
        b_last = b_col[T - 1:T, :]
        w_end = jnp.exp(b_last - b_col + ig_col - m_new) * K_SCALE
        decay = jnp.exp(b_last + m_prev - m_new)
        kw = k * w_end
        kwb = kw.astype(BF16)
        yield
        c_scr[h] = decay * c_old + _dot_tn(kwb, vb)
        n_scr[h:h + 1, :] = decay * n_old + jnp.sum(kw, axis=0, keepdims=True)
        m_scr[h:h + 1, :] = jnp.broadcast_to(m_new, (1, 128))
        hn = _head_norm(ht, w.mhln[:, h * 256:(h + 1) * 256])
        ycat_scr[:, 1024 + h * 256:1024 + (h + 1) * 256] = (hn * seg(OFF_O, h) * seg(OFF_ZB, h)).astype(BF16)

    for pair in range(0, N_HEADS, HEADS_IN_FLIGHT):
        running = [head(h) for h in range(pair, pair + HEADS_IN_FLIGHT)]
        while running:
            alive = []
            for g in running:
                if next(g, g) is not g:
                    alive.append(g)
                yield
            running = alive


def _out_steps(scr, w, y_ref):
    _, x_scr, _, ycat_scr, _ = scr

    def residual():
        y_ref[...] = x_scr[...]

    def piece(c0):
        def run():
            cols = slice(c0, c0 + OUT_COLS)
            y_ref[:, cols] = y_ref[:, cols] + _dot(ycat_scr[...], w.out[:, cols])
        return run

    def norm():
        y_ref[...] = _rms(y_ref[...], w.normf[...])

    return [residual] + [piece(c0) for c0 in range(0, D_MODEL, OUT_COLS)] + [norm]


PIECE_COLS = 256
OUT_COLS = 256
HEADS_IN_FLIGHT = 2
MID_YIELDS = 2 + 3 * N_HEADS


def _run_interleaved(mid, lead, at_yield):
    assert len(at_yield) == MID_YIELDS
    for piece in lead:
        piece()
    for pieces in at_yield:
        next(mid)
        for piece in pieces:
            piece()
    for _ in mid:
        raise AssertionError("unexpected extra yield")


def _spread(pieces, yields):
    base, extra = divmod(len(pieces), yields)
    groups, start = [], 0
    for i in range(yields):
        stop = start + base + (i < extra)
        groups.append(pieces[start:stop])
        start = stop
    return groups


def _m_row(m_scr):
    lane = lax.broadcasted_iota(jnp.int32, (1, 128), 1)
    row = jnp.zeros((1, 128), F32)
    for h in range(N_HEADS):
        row = jnp.where(lane == h, m_scr[h:h + 1, :], row)
    return row


def _prompt_kernel(meta_ref, x0_ref, *refs, steps_per_seq):
    x_next = refs[:STEP_TILES]
    refs = refs[STEP_TILES:]
    w = LayerW(*refs[:9])
    y_ref, pool_out, c_out, n_out, m_out = refs[9:14]
    scr = (refs[14:19], refs[19:24])
    scr0, scr_meta = scr[0], refs[24:29]
    state, state_meta = refs[29:33], refs[33:37]
    hist_scr, c_scr, n_scr, m_scr = state
    s = pl.program_id(0)
    TT = PROMPT_TILE

    @pl.when(s == 0)
    def _():
        for ref in state_meta:
            ref[...] = jnp.zeros_like(ref)
        for p in _inproj_steps(meta_ref, scr_meta, w):
            p()
        for _ in _mid_steps(scr_meta, 0, w, state_meta):
            pass
        for p in _inproj_steps(x0_ref, scr0, w):
            p()

    @pl.when(s % steps_per_seq == 0)
    def _():
        for ref, ref_meta in zip(state, state_meta):
            ref[...] = ref_meta[...]

    out_pieces = []
    for j in range(STEP_TILES):
        norm, *proj = _inproj_steps(x_next[j], scr[1 - j % 2], w)
        if out_pieces:
            residual, o0, o1, o2, o3, out_norm = out_pieces
            lead = [residual, norm]
            at_yield = [[], [o0, o1]] + _spread(proj, MID_YIELDS - 2)
            at_yield[2] = [o2] + at_yield[2]
            at_yield[3] = [o3] + at_yield[3]
            at_yield[4] = at_yield[4] + [out_norm]
        else:
            lead = [norm, proj[0]]
            at_yield = _spread(proj[1:], MID_YIELDS)
        _run_interleaved(_mid_steps(scr[j % 2], N_META, w, state), lead, at_yield)
        out_pieces = _out_steps(scr[j % 2], w, y_ref.at[j * TT:(j + 1) * TT, :])
    for piece in out_pieces:
        piece()

    @pl.when(s % steps_per_seq == steps_per_seq - 1)
    def _():
        b = s // steps_per_seq
        for j in range(POOL_BUF):
            pool_out[0, j, pl.ds(b, 1), :] = hist_scr[1 + j:2 + j, :]
        c_out[0, 0] = c_scr[...]
        n_out[0, 0] = n_scr[0:N_HEADS, :]
        m_out[pl.ds(b, 1), :] = _m_row(m_scr)


def _prep_kernel(wt_ref, wgt_ref, xs_ref, norm1_ref, wout_ref, wmain_ref, wgate_ref, proj_ref, woutb_ref,
                 xn_scr, *, n_seg):
    i = pl.program_id(0)

    @pl.when(i == 0)
    def _():
        g = jnp.concatenate([wgt_ref[...], jnp.zeros((GATE_PAD - 8, D_MODEL), F32)], axis=0)
        wgate_ref[...] = g.T.astype(BF16)
        xn_scr[...] = _rms(xs_ref[...], norm1_ref[...]).astype(BF16)

    woutb_ref[...] = wout_ref[...].astype(BF16)

    @pl.when(i < n_seg)
    def _():
        wb = wt_ref[...].T.astype(BF16)
        wmain_ref[...] = wb
        proj_ref[...] = _dot(xn_scr[...], wb)


def _expand(seq_col, vals):
    out = None
    for i, val in enumerate(vals):
        pick = jnp.where(seq_col == i, val, 0.0)
        out = pick if out is None else out + pick
    return out


def _sample_kernel(x_ref, proj_ref, pool_ref, c_ref, n_ref, m_ref, *refs, seq_len, pos0):
    w = LayerW(None, *refs[:8])
    y_ref, pool_out, c_out, n_out, m_out = refs[8:13]
    ext_scr, ycat_scr = refs[13:15]
    G = SAMPLE_GROUP
    T = G * seq_len
    x = x_ref[...]
    xn = _rms(x, w.norm1[...]).astype(BF16)
    causal = _seq_mask(T, seq_len)
    g_col = _gate_pre(xn, w)
    b_col_all = _cumsum(*_cumsum_operands(g_col, causal))
    r_row_all = _to_rows(g_col - pltpu.roll(b_col_all, GATE_PAD - N_HEADS, axis=1))
    seq_col = lax.broadcasted_iota(jnp.int32, (T, 1), 0) >> (seq_len.bit_length() - 1)
    pos_col = jnp.full((seq_len, 1), float(pos0), F32) + lax.broadcasted_iota(
        jnp.int32, (seq_len, 1), 0).astype(F32)

    pooled_rows = []
    for i in range(G):
        rows = slice(i * seq_len, (i + 1) * seq_len)
        u_i = proj_ref[rows, OFF_U:OFF_U + 1024]
        ext_scr[0:1, :] = jnp.zeros((1, 1024), F32)
        for j in range(POOL_BUF):
            ext_scr[1 + j:2 + j, :] = pool_ref[0, j, i:i + 1, :]
        ext_scr[HIST:HIST + seq_len, :] = u_i
        ext = ext_scr[...]
        pooled_rows.append(_pooled(ext, u_i, pos_col))
        for j in range(POOL_BUF):
            pool_out[0, j, i:i + 1, :] = ext[seq_len + 1 + j:seq_len + 2 + j, :]
    mixed = _pool_mix([jnp.concatenate(p, axis=0) for p in zip(*pooled_rows)], w.pool)
    y_a = mixed * w.pscale[...] * _silu(proj_ref[:, OFF_ZA:OFF_ZA + 1024])
    ycat_scr[:, 0:1024] = y_a.astype(BF16)

    lane4 = lax.broadcasted_iota(jnp.int32, (1, N_HEADS), 1)
    last = [(i + 1) * seq_len - 1 for i in range(G)]
    m_new_heads = [None] * N_HEADS

    def head(h):
        hc = slice(h * 256, (h + 1) * 256)
        q = proj_ref[:, OFF_Q + h * 256:OFF_Q + (h + 1) * 256]
        k = proj_ref[:, OFF_K + h * 256:OFF_K + (h + 1) * 256]
        v = proj_ref[:, OFF_V + h * 256:OFF_V + (h + 1) * 256]
        qb, kb, vb = q.astype(BF16), k.astype(BF16), v.astype(BF16)
        c_old = [c_ref[0, i, h] for i in range(G)]
        c_old_b = [c.astype(BF16) for c in c_old]
        yield
        q_cs = [_dot(qb, cb) for cb in c_old_b]
        ig_col = g_col[:, h:h + 1]
        b_col = b_col_all[:, 4 + h:5 + h]
        r_row = r_row_all[h:h + 1, :]
        m_prev = [m_ref[0, i:i + 1, h:h + 1] for i in range(G)]
        a_col = b_col + _expand(seq_col, m_prev)
        m_t, inter, s = _intra_weights(qb, kb, b_col, a_col, r_row, causal)
        sb = s.astype(BF16)
        yield
        sv = _dot(sb, vb)
        n_old = [n_ref[0, i, h:h + 1, :] for i in range(G)]
        num = inter * _expand(seq_col, q_cs) + sv
        qn = inter * jnp.sum(q * _expand(seq_col, n_old), axis=-1, keepdims=True) + jnp.sum(
            s, axis=-1, keepdims=True)
        ht = num * (1.0 / jnp.maximum(jnp.abs(qn), jnp.exp(-m_t)))
        m_new = [m_t[r:r + 1, :] for r in last]
        b_last = [b_col[r:r + 1, :] for r in last]
        w_end = jnp.exp(_expand(seq_col, b_last) - b_col + ig_col - _expand(seq_col, m_new)) * K_SCALE
        kw = k * w_end
        kwb = kw.astype(BF16)
        v_seq = [jnp.where(seq_col == i, vb, jnp.zeros_like(vb)) for i in range(G)]
        yield
        updates = [_dot_tn(kwb, v_i) for v_i in v_seq]
        for i in range(G):
            decay = jnp.exp(b_last[i] + m_prev[i] - m_new[i])
            c_out[0, i, h] = decay * c_old[i] + updates[i]
            n_out[0, i, h:h + 1, :] = decay * n_old[i] + jnp.sum(
                kw[i * seq_len:(i + 1) * seq_len, :], axis=0, keepdims=True)
        m_new_heads[h] = m_new
        hn = _head_norm(ht, w.mhln[:, hc])
        o = proj_ref[:, OFF_O + h * 256:OFF_O + (h + 1) * 256]
        zb = proj_ref[:, OFF_ZB + h * 256:OFF_ZB + (h + 1) * 256]
        ycat_scr[:, 1024 + h * 256:1024 + (h + 1) * 256] = (hn * jax.nn.sigmoid(o) * _silu(zb)).astype(BF16)

    running = [head(h) for h in range(N_HEADS)]
    while running:
        running = [g for g in running if next(g, g) is not g]
    for i in range(G):
        row = jnp.zeros((1, N_HEADS), F32)
        for h in range(N_HEADS):
            row = jnp.where(lane4 == h, m_new_heads[h][i], row)
        m_out[0, i:i + 1, :] = row

    y = _dot(ycat_scr[...], w.out[...])
    y_ref[...] = _rms(x + y, w.normf[...])


def _const_spec(shape):
    nd = len(shape)
    return pl.BlockSpec(shape, lambda *_: (0,) * nd, pipeline_mode=pl.Buffered(1))


def _params(sem):
    return pltpu.CompilerParams(dimension_semantics=sem, vmem_limit_bytes=VMEM_LIMIT)


def _tile_scratch(T):
    return [pltpu.VMEM((T, D_MAIN), F32), pltpu.VMEM((T, D_MODEL), F32), pltpu.VMEM((T, GATE_PAD), F32),
            pltpu.VMEM((T, 2048), BF16), pltpu.VMEM((T, D_MODEL), BF16)]


def _state_scratch():
    return [pltpu.VMEM((HIST, D_POOL), F32), pltpu.VMEM((N_HEADS, HEAD_DIM, HEAD_DIM), F32),
            pltpu.VMEM((8, HEAD_DIM), F32), pltpu.VMEM((8, 128), F32)]


def kernel(x_prompt, x_sample, state_pool, state_C, state_n, state_m, meta_tokens, norm1_w, w_in,
           b_if, w_pool, pool_scale, mhln_w, w_out, normf_w):
    B, S, _ = x_prompt.shape
    SB, SL, _ = x_sample.shape
    TT = PROMPT_TILE
    assert norm1_w.shape[0] == 1, "single layer"
    assert STEP_TILES % 2 == 0 and S % (STEP_TILES * TT) == 0 and SB % SAMPLE_GROUP == 0

    w_in_t = jnp.swapaxes(w_in[0], 0, 1)
    norm1 = norm1_w[0].reshape(1, D_MODEL)
    n_tok = SB * SL
    xs = x_sample.reshape(n_tok, D_MODEL)
    n_seg = D_MAIN // 1024
    wout_rows = w_out.shape[1] // (n_seg + 1)
    seg = lambda i: jnp.minimum(i, n_seg - 1)
    w_main, w_gate, proj_s, wout = pl.pallas_call(
        functools.partial(_prep_kernel, n_seg=n_seg),
        grid=(n_seg + 1,),
        in_specs=[pl.BlockSpec((1024, D_MODEL), lambda i: (seg(i), 0)),
                  pl.BlockSpec((8, D_MODEL), lambda i: (D_MAIN // 8, 0)),
                  _const_spec(xs.shape), _const_spec(norm1.shape),
                  pl.BlockSpec((wout_rows, D_MODEL), lambda i: (i, 0))],
        out_specs=(pl.BlockSpec((D_MODEL, 1024), lambda i: (0, seg(i))),
                   pl.BlockSpec((D_MODEL, GATE_PAD), lambda i: (0, 0)),
                   pl.BlockSpec((n_tok, 1024), lambda i: (0, seg(i))),
                   pl.BlockSpec((wout_rows, D_MODEL), lambda i: (i, 0))),
        out_shape=(jax.ShapeDtypeStruct((D_MODEL, D_MAIN), BF16),
                   jax.ShapeDtypeStruct((D_MODEL, GATE_PAD), BF16),
                   jax.ShapeDtypeStruct((n_tok, D_MAIN), F32),
                   jax.ShapeDtypeStruct(w_out.shape[1:], BF16)),
        scratch_shapes=[pltpu.VMEM((n_tok, D_MODEL), BF16)],
        compiler_params=_params(("arbitrary",)),
        name="prep",
    )(w_in_t, w_in_t, xs, norm1, w_out[0])
    bias_row = jnp.pad(b_if[0], (0, GATE_PAD - 8)).reshape(1, GATE_PAD)
    wpool = w_pool[0].astype(BF16)
    pscale = pool_scale[0].reshape(1, D_POOL)
    mhln = mhln_w[0].reshape(1, D_MLSTM)
    normf = normf_w.reshape(1, D_MODEL)
    layer_w = LayerW(w_main, w_gate, bias_row, norm1, wpool, pscale, mhln, wout, normf)
    layer_specs = [_const_spec(a.shape) for a in layer_w]

    n_tiles = B * S // TT
    steps = n_tiles // STEP_TILES
    steps_per_seq = S // (STEP_TILES * TT)
    xp = x_prompt.reshape(B * S, D_MODEL)
    y_p, pool_p, c_p, n_p, m_p = pl.pallas_call(
        functools.partial(_prompt_kernel, steps_per_seq=steps_per_seq),
        grid=(steps,),
        in_specs=[_const_spec(meta_tokens.shape),
                  pl.BlockSpec((TT, D_MODEL), lambda s: (0, 0))]
                 + [pl.BlockSpec((TT, D_MODEL),
                                 lambda s, j=j: (jnp.minimum(STEP_TILES * s + 1 + j, n_tiles - 1), 0))
                    for j in range(STEP_TILES)]
                 + layer_specs,
        out_specs=(pl.BlockSpec((STEP_TILES * TT, D_MODEL), lambda s: (s, 0)),
                   pl.BlockSpec((1, POOL_BUF, B, D_POOL), lambda s: (0, 0, 0, 0)),
                   pl.BlockSpec((1, 1, N_HEADS, HEAD_DIM, HEAD_DIM),
                                lambda s: (0, s // steps_per_seq, 0, 0, 0)),
                   pl.BlockSpec((1, 1, N_HEADS, HEAD_DIM), lambda s: (0, s // steps_per_seq, 0, 0)),
                   pl.BlockSpec((B, 128), lambda s: (0, 0))),
        out_shape=(jax.ShapeDtypeStruct((B * S, D_MODEL), F32),
                   jax.ShapeDtypeStruct((1, POOL_BUF, B, D_POOL), F32),
                   jax.ShapeDtypeStruct((1, B, N_HEADS, HEAD_DIM, HEAD_DIM), F32),
                   jax.ShapeDtypeStruct((1, B, N_HEADS, HEAD_DIM), F32),
                   jax.ShapeDtypeStruct((B, 128), F32)),
        scratch_shapes=_tile_scratch(TT) + _tile_scratch(TT) + _tile_scratch(N_META)
        + _state_scratch() + _state_scratch(),
        compiler_params=_params(("arbitrary",)),
        name="prompt",
    )(meta_tokens, xp, *([xp] * STEP_TILES), *layer_w)
    y_prompt = y_p.reshape(B, S, D_MODEL)
    pool_p = jnp.swapaxes(pool_p, 1, 2)
    m_p = m_p[:, :N_HEADS].reshape(1, B, N_HEADS)

    G = SAMPLE_GROUP
    GT = G * SL
    pool_in = jnp.swapaxes(state_pool, 1, 2)
    sample_w = layer_w[1:]
    y_s, pool_s, c_s, n_s, m_s = pl.pallas_call(
        functools.partial(_sample_kernel, seq_len=SL, pos0=PAST_LEN),
        grid=(SB // G,),
        in_specs=[pl.BlockSpec((GT, D_MODEL), lambda i: (i, 0)),
                  pl.BlockSpec((GT, D_MAIN), lambda i: (i, 0)),
                  pl.BlockSpec((1, POOL_BUF, G, D_POOL), lambda i: (0, 0, i, 0)),
                  pl.BlockSpec((1, G, N_HEADS, HEAD_DIM, HEAD_DIM), lambda i: (0, i, 0, 0, 0)),
                  pl.BlockSpec((1, G, N_HEADS, HEAD_DIM), lambda i: (0, i, 0, 0)),
                  pl.BlockSpec((1, G, N_HEADS), lambda i: (0, i, 0))]
                 + [_const_spec(a.shape) for a in sample_w],
        out_specs=(pl.BlockSpec((GT, D_MODEL), lambda i: (i, 0)),
                   pl.BlockSpec((1, POOL_BUF, G, D_POOL), lambda i: (0, 0, i, 0)),
                   pl.BlockSpec((1, G, N_HEADS, HEAD_DIM, HEAD_DIM), lambda i: (0, i, 0, 0, 0)),
                   pl.BlockSpec((1, G, N_HEADS, HEAD_DIM), lambda i: (0, i, 0, 0)),
                   pl.BlockSpec((1, G, N_HEADS), lambda i: (0, i, 0))),
        out_shape=(jax.ShapeDtypeStruct((n_tok, D_MODEL), F32),
                   jax.ShapeDtypeStruct(pool_in.shape, F32),
                   jax.ShapeDtypeStruct(state_C.shape, F32),
                   jax.ShapeDtypeStruct(state_n.shape, F32),
                   jax.ShapeDtypeStruct(state_m.shape, F32)),
        scratch_shapes=[pltpu.VMEM((HIST + SL, D_POOL), F32), pltpu.VMEM((GT, 2048), BF16)],
        compiler_params=_params(("arbitrary",)),
        name="sample",
    )(xs, proj_s, pool_in, state_C, state_n, state_m, *sample_w)
    y_sample = y_s.reshape(SB, SL, D_MODEL)
    pool_s = jnp.swapaxes(pool_s, 1, 2)

    return (y_prompt, y_sample, pool_p, c_p, n_p, m_p, pool_s, c_s, n_s, m_s)
```

```python
import collections
import functools

import jax
import jax.numpy as jnp
from jax import lax
from jax.experimental import pallas as pl
from jax.experimental.pallas import tpu as pltpu

D_MODEL = 1024
D_POOL = 1024
D_MLSTM = 1024
N_HEADS = 4
HEAD_DIM = 256
POOL_WINDOWS = (2, 4, 8, 16)
POOL_BUF = 15
HIST = 16
N_META = 16
PAST_LEN = 16384
EPS = 1e-6
D_MAIN = 2 * D_POOL + 5 * D_MLSTM
GATE_PAD = 128
K_SCALE = HEAD_DIM ** -0.5

OFF_U, OFF_ZA, OFF_Q, OFF_K, OFF_V, OFF_O, OFF_ZB = (i * 1024 for i in range(7))

PROMPT_TILE = 256
STEP_TILES = 2
SAMPLE_GROUP = 8
VMEM_LIMIT = 60000 * 1024

F32 = jnp.float32
BF16 = jnp.bfloat16

LayerW = collections.namedtuple("LayerW", "main gate bias norm1 pool pscale mhln out normf")


def _dot(a, b):
    return jnp.dot(a, b, preferred_element_type=F32)


def _dot_nt(a, b):
    return lax.dot_general(a, b, (((1,), (1,)), ((), ())), preferred_element_type=F32)


def _dot_tn(a, b):
    return lax.dot_general(a, b, (((0,), (0,)), ((), ())), preferred_element_type=F32)


def _rms(x, w):
    return x * lax.rsqrt(jnp.mean(x * x, axis=-1, keepdims=True) + EPS) * w


def _log_sigmoid(x):
    return jnp.minimum(x, 0.0) - jnp.log1p(jnp.exp(-jnp.abs(x)))


def _silu(x):
    return x * jax.nn.sigmoid(x)


def _split3(x):
    hi = x.astype(BF16)
    r = x - hi.astype(F32)
    mid = r.astype(BF16)
    lo = (r - mid.astype(F32)).astype(BF16)
    return hi, mid, lo


def _seq_mask(T, seq_len):
    row = lax.broadcasted_iota(jnp.int32, (T, T), 0)
    col = lax.broadcasted_iota(jnp.int32, (T, T), 1)
    causal = col <= row
    if seq_len < T:
        shift = seq_len.bit_length() - 1
        assert 1 << shift == seq_len
        causal = causal & ((row >> shift) == (col >> shift))
    return causal


def _to_rows(cols):
    T = cols.shape[0]
    pad = -T % 128
    if pad:
        cols = jnp.concatenate([cols, jnp.zeros((pad, cols.shape[1]), cols.dtype)], axis=0)
    return cols.T[:, 0:T]


def _gate_pre(xn, w):
    return _dot(xn, w.gate[...]) + w.bias[...]


def _cumsum_operands(g_col, causal):
    tri = jnp.where(causal, 1.0, 0.0).astype(BF16)
    return tri, _split3(_log_sigmoid(g_col))


def _cumsum(tri, parts):
    return sum(_dot(tri, p) for p in parts)


def _intra_weights(qb, kb, b_col, a_col, r_row, causal):
    dm = jnp.where(causal, b_col + r_row, -jnp.inf)
    m_t = jnp.maximum(a_col, jnp.max(dm, axis=-1, keepdims=True))
    w = jnp.exp(dm - m_t)
    inter = jnp.exp(a_col - m_t)
    return m_t, inter, _dot_nt(qb, kb) * (w * K_SCALE)


def _head_norm(ht, w_row):
    mu = jnp.mean(ht, axis=-1, keepdims=True)
    d = ht - mu
    var = jnp.mean(d * d, axis=-1, keepdims=True)
    return d * lax.rsqrt(var + EPS) * w_row


def _window_sums(ext):
    s2 = ext + pltpu.roll(ext, 1, axis=0)
    s4 = s2[:, 256:] + pltpu.roll(s2[:, 256:], 2, axis=0)
    s8 = s4[:, 256:] + pltpu.roll(s4[:, 256:], 4, axis=0)
    s16 = s8[:, 256:] + pltpu.roll(s8[:, 256:], 8, axis=0)
    return [s2[HIST:, 0:256], s4[HIST:, 0:256], s8[HIST:, 0:256], s16[HIST:, 0:256]]


def _pooled(ext, u, pos_col):
    sums = _window_sums(ext)
    return [sums[g] * (1.0 / jnp.minimum(float(w), pos_col + 1.0)) - u[:, g * 256:(g + 1) * 256]
            for g, w in enumerate(POOL_WINDOWS)]


def _pool_mix(pooled, wpool_ref):
    return jnp.concatenate([_dot(p.astype(BF16), wpool_ref[g]) for g, p in enumerate(pooled)], axis=-1)


def _inproj_steps(x_ref, scr, w):
    proj_scr, x_scr, gcol_scr, _, xn_scr = scr

    def norm():
        x = x_ref[...]
        x_scr[...] = x
        xn_scr[...] = _rms(x, w.norm1[...]).astype(BF16)

    def piece(c0):
        act = {OFF_ZA: _silu, OFF_O: jax.nn.sigmoid, OFF_ZB: _silu}.get(c0 // 1024 * 1024, lambda p: p)

        def run():
            proj_scr[:, c0:c0 + PIECE_COLS] = act(_dot(xn_scr[...], w.main[:, c0:c0 + PIECE_COLS]))
        return run

    def gates():
        gcol_scr[...] = _gate_pre(xn_scr[...], w)

    return [norm] + [piece(c0) for c0 in range(0, D_MAIN, PIECE_COLS)] + [gates]


def _mid_steps(scr, pos0, w, state):
    proj_scr, x_scr, gcol_scr, ycat_scr, _ = scr
    hist_scr, c_scr, n_scr, m_scr = state
    T = x_scr.shape[0]

    def seg(off, h=None):
        if h is None:
            return proj_scr[:, off:off + 1024]
        return proj_scr[:, off + h * 256:off + (h + 1) * 256]

    causal = _seq_mask(T, T)
    g_col = gcol_scr[...]
    tri, lf_parts = _cumsum_operands(g_col, causal)
    yield
    b_col_all = _cumsum(tri, lf_parts)
    r_row_all = _to_rows(g_col - pltpu.roll(b_col_all, GATE_PAD - N_HEADS, axis=1))

    u = seg(OFF_U)
    ext = jnp.concatenate([hist_scr[...], u], axis=0)
    pos_col = (lax.broadcasted_iota(jnp.int32, (T, 1), 0) + pos0).astype(F32)
    pooled = _pooled(ext, u, pos_col)
    hist_scr[...] = ext[T:T + HIST, :]
    yield
    mixed = _pool_mix(pooled, w.pool)
    ycat_scr[:, 0:1024] = (mixed * w.pscale[...] * seg(OFF_ZA)).astype(BF16)

    def head(h):
        q, k, v = seg(OFF_Q, h), seg(OFF_K, h), seg(OFF_V, h)
        qb, kb, vb = q.astype(BF16), k.astype(BF16), v.astype(BF16)
        ig_col = g_col[:, h:h + 1]
        b_col = b_col_all[:, 4 + h:5 + h]
        r_row = r_row_all[h:h + 1, :]
        m_prev = m_scr[h:h + 1, 0:1]
        a_col = b_col + m_prev
        c_old = c_scr[h]
        n_old = n_scr[h:h + 1, :]
        q_c = _dot(qb, c_old.astype(BF16))
        m_t, inter, s = _intra_weights(qb, kb, b_col, a_col, r_row, causal)
        sb = s.astype(BF16)
        yield
        num = inter * q_c + _dot(sb, vb)
        qn = inter * jnp.sum(q * n_old, axis=-1, keepdims=True) + jnp.sum(s, axis=-1, keepdims=True)
        ht = num * (1.0 / jnp.maximum(jnp.abs(qn), jnp.exp(-m_t)))
        m_new = m_t[T - 1:T, :]
        b_last = b_col[T - 1:T, :]
        w_end = jnp.exp(b_last - b_col + ig_col - m_new) * K_SCALE
        decay = jnp.exp(b_last + m_prev - m_new)
        kw = k * w_end
        kwb = kw.astype(BF16)
        yield
        c_scr[h] = decay * c_old + _dot_tn(kwb, vb)
        n_scr[h:h + 1, :] = decay * n_old + jnp.sum(kw, axis=0, keepdims=True)
        m_scr[h:h + 1, :] = jnp.broadcast_to(m_new, (1, 128))
        hn = _head_norm(ht, w.mhln[:, h * 256:(h + 1) * 256])
        ycat_scr[:, 1024 + h * 256:1024 + (h + 1) * 256] = (hn * seg(OFF_O, h) * seg(OFF_ZB, h)).astype(BF16)

    for pair in range(0, N_HEADS, HEADS_IN_FLIGHT):
        running = [head(h) for h in range(pair, pair + HEADS_IN_FLIGHT)]
        while running:
            alive = []
            for g in running:
                if next(g, g) is not g:
                    alive.append(g)
                yield
            running = alive


def _out_steps(scr, w, y_ref):
    _, x_scr, _, ycat_scr, _ = scr

    def residual():
        y_ref[...] = x_scr[...]

    def piece(c0):
        def run():
            cols = slice(c0, c0 + OUT_COLS)
            y_ref[:, cols] = y_ref[:, cols] + _dot(ycat_scr[...], w.out[:, cols])
        return run

    def norm():
        y_ref[...] = _rms(y_ref[...], w.normf[...])

    return [residual] + [piece(c0) for c0 in range(0, D_MODEL, OUT_COLS)] + [norm]


PIECE_COLS = 256
OUT_COLS = 256
HEADS_IN_FLIGHT = 2
MID_YIELDS = 2 + 3 * N_HEADS


def _run_interleaved(mid, lead, at_yield):
    assert len(at_yield) == MID_YIELDS
    for piece in lead:
        piece()
    for pieces in at_yield:
        next(mid)
        for piece in pieces:
            piece()
    for _ in mid:
        raise AssertionError("unexpected extra yield")


def _spread(pieces, yields):
    base, extra = divmod(len(pieces), yields)
    groups, start = [], 0
    for i in range(yields):
        stop = start + base + (i >= yields - extra)
        groups.append(pieces[start:stop])
        start = stop
    return groups


def _m_row(m_scr):
    lane = lax.broadcasted_iota(jnp.int32, (1, 128), 1)
    row = jnp.zeros((1, 128), F32)
    for h in range(N_HEADS):
        row = jnp.where(lane == h, m_scr[h:h + 1, :], row)
    return row


def _prompt_kernel(meta_ref, x0_ref, *refs, steps_per_seq):
    x_next = refs[:STEP_TILES]
    refs = refs[STEP_TILES:]
    w = LayerW(*refs[:9])
    y_ref, pool_out, c_out, n_out, m_out = refs[9:14]
    scr = (refs[14:19], refs[19:24])
    scr0, scr_meta = scr[0], refs[24:29]
    state, state_meta = refs[29:33], refs[33:37]
    hist_scr, c_scr, n_scr, m_scr = state
    s = pl.program_id(0)
    TT = PROMPT_TILE

    @pl.when(s == 0)
    def _():
        for ref in state_meta:
            ref[...] = jnp.zeros_like(ref)
        for p in _inproj_steps(meta_ref, scr_meta, w):
            p()
        for _ in _mid_steps(scr_meta, 0, w, state_meta):
            pass
        for p in _inproj_steps(x0_ref, scr0, w):
            p()

    @pl.when(s % steps_per_seq == 0)
    def _():
        for ref, ref_meta in zip(state, state_meta):
            ref[...] = ref_meta[...]

    out_pieces = []
    for j in range(STEP_TILES):
        norm, *proj = _inproj_steps(x_next[j], scr[1 - j % 2], w)
        if out_pieces:
            residual, o0, o1, o2, o3, out_norm = out_pieces
            lead = [residual, o0, norm, o1]
            at_yield = [[o2], [o3]] + _spread(proj, MID_YIELDS - 2)
            at_yield[3] = at_yield[3] + [out_norm]
        else:
            lead = [norm, proj[0]]
            at_yield = _spread(proj[1:], MID_YIELDS)
        _run_interleaved(_mid_steps(scr[j % 2], N_META, w, state), lead, at_yield)
        out_pieces = _out_steps(scr[j % 2], w, y_ref.at[j * TT:(j + 1) * TT, :])
    for piece in out_pieces:
        piece()

    @pl.when(s % steps_per_seq == steps_per_seq - 1)
    def _():
        b = s // steps_per_seq
        for j in range(POOL_BUF):
            pool_out[0, j, pl.ds(b, 1), :] = hist_scr[1 + j:2 + j, :]
        c_out[0, 0] = c_scr[...]
        n_out[0, 0] = n_scr[0:N_HEADS, :]
        m_out[pl.ds(b, 1), :] = _m_row(m_scr)


def _prep_kernel(wt_ref, wgt_ref, xs_ref, norm1_ref, wout_ref, wmain_ref, wgate_ref, proj_ref, woutb_ref,
                 xn_scr, *, n_seg):
    i = pl.program_id(0)

    @pl.when(i == 0)
    def _():
        g = jnp.concatenate([wgt_ref[...], jnp.zeros((GATE_PAD - 8, D_MODEL), F32)], axis=0)
        wgate_ref[...] = g.T.astype(BF16)
        xn_scr[...] = _rms(xs_ref[...], norm1_ref[...]).astype(BF16)

    woutb_ref[...] = wout_ref[...].astype(BF16)

    @pl.when(i < n_seg)
    def _():
        wb = wt_ref[...].T.astype(BF16)
        wmain_ref[...] = wb
        proj_ref[...] = _dot(xn_scr[...], wb)


def _expand(seq_col, vals):
    out = None
    for i, val in enumerate(vals):
        pick = jnp.where(seq_col == i, val, 0.0)
        out = pick if out is None else out + pick
    return out


def _sample_kernel(x_ref, proj_ref, pool_ref, c_ref, n_ref, m_ref, *refs, seq_len, pos0):
    w = LayerW(None, *refs[:8])
    y_ref, pool_out, c_out, n_out, m_out = refs[8:13]
    ext_scr, ycat_scr = refs[13:15]
    G = SAMPLE_GROUP
    T = G * seq_len
    x = x_ref[...]
    xn = _rms(x, w.norm1[...]).astype(BF16)
    causal = _seq_mask(T, seq_len)
    g_col = _gate_pre(xn, w)
    b_col_all = _cumsum(*_cumsum_operands(g_col, causal))
    r_row_all = _to_rows(g_col - pltpu.roll(b_col_all, GATE_PAD - N_HEADS, axis=1))
    seq_col = lax.broadcasted_iota(jnp.int32, (T, 1), 0) >> (seq_len.bit_length() - 1)
    pos_col = jnp.full((seq_len, 1), float(pos0), F32) + lax.broadcasted_iota(
        jnp.int32, (seq_len, 1), 0).astype(F32)

    pooled_rows = []
    for i in range(G):
        rows = slice(i * seq_len, (i + 1) * seq_len)
        u_i = proj_ref[rows, OFF_U:OFF_U + 1024]
        ext_scr[0:1, :] = jnp.zeros((1, 1024), F32)
        for j in range(POOL_BUF):
            ext_scr[1 + j:2 + j, :] = pool_ref[0, j, i:i + 1, :]
        ext_scr[HIST:HIST + seq_len, :] = u_i
        ext = ext_scr[...]
        pooled_rows.append(_pooled(ext, u_i, pos_col))
        for j in range(POOL_BUF):
            pool_out[0, j, i:i + 1, :] = ext[seq_len + 1 + j:seq_len + 2 + j, :]
    mixed = _pool_mix([jnp.concatenate(p, axis=0) for p in zip(*pooled_rows)], w.pool)
    y_a = mixed * w.pscale[...] * _silu(proj_ref[:, OFF_ZA:OFF_ZA + 1024])
    ycat_scr[:, 0:1024] = y_a.astype(BF16)

    lane4 = lax.broadcasted_iota(jnp.int32, (1, N_HEADS), 1)
    last = [(i + 1) * seq_len - 1 for i in range(G)]
    m_new_heads = [None] * N_HEADS

    def head(h):
        hc = slice(h * 256, (h + 1) * 256)
        q = proj_ref[:, OFF_Q + h * 256:OFF_Q + (h + 1) * 256]
        k = proj_ref[:, OFF_K + h * 256:OFF_K + (h + 1) * 256]
        v = proj_ref[:, OFF_V + h * 256:OFF_V + (h + 1) * 256]
        qb, kb, vb = q.astype(BF16), k.astype(BF16), v.astype(BF16)
        c_old = [c_ref[0, i, h] for i in range(G)]
        c_old_b = [c.astype(BF16) for c in c_old]
        yield
        q_cs = [_dot(qb, cb) for cb in c_old_b]
        ig_col = g_col[:, h:h + 1]
        b_col = b_col_all[:, 4 + h:5 + h]
        r_row = r_row_all[h:h + 1, :]
        m_prev = [m_ref[0, i:i + 1, h:h + 1] for i in range(G)]
        a_col = b_col + _expand(seq_col, m_prev)
        m_t, inter, s = _intra_weights(qb, kb, b_col, a_col, r_row, causal)
        sb = s.astype(BF16)
        yield
        sv = _dot(sb, vb)
        n_old = [n_ref[0, i, h:h + 1, :] for i in range(G)]
        num = inter * _expand(seq_col, q_cs) + sv
        qn = inter * jnp.sum(q * _expand(seq_col, n_old), axis=-1, keepdims=True) + jnp.sum(
            s, axis=-1, keepdims=True)
        ht = num * (1.0 / jnp.maximum(jnp.abs(qn), jnp.exp(-m_t)))
        m_new = [m_t[r:r + 1, :] for r in last]
        b_last = [b_col[r:r + 1, :] for r in last]
        w_end = jnp.exp(_expand(seq_col, b_last) - b_col + ig_col - _expand(seq_col, m_new)) * K_SCALE
        kw = k * w_end
        kwb = kw.astype(BF16)
        v_seq = [jnp.where(seq_col == i, vb, jnp.zeros_like(vb)) for i in range(G)]
        yield
        updates = [_dot_tn(kwb, v_i) for v_i in v_seq]
        for i in range(G):
            decay = jnp.exp(b_last[i] + m_prev[i] - m_new[i])
            c_out[0, i, h] = decay * c_old[i] + updates[i]
            n_out[0, i, h:h + 1, :] = decay * n_old[i] + jnp.sum(
                kw[i * seq_len:(i + 1) * seq_len, :], axis=0, keepdims=True)
        m_new_heads[h] = m_new
        hn = _head_norm(ht, w.mhln[:, hc])
        o = proj_ref[:, OFF_O + h * 256:OFF_O + (h + 1) * 256]
        zb = proj_ref[:, OFF_ZB + h * 256:OFF_ZB + (h + 1) * 256]
        ycat_scr[:, 1024 + h * 256:1024 + (h + 1) * 256] = (hn * jax.nn.sigmoid(o) * _silu(zb)).astype(BF16)

    running = [head(h) for h in range(N_HEADS)]
    while running:
        running = [g for g in running if next(g, g) is not g]
    for i in range(G):
        row = jnp.zeros((1, N_HEADS), F32)
        for h in range(N_HEADS):
            row = jnp.where(lane4 == h, m_new_heads[h][i], row)
        m_out[0, i:i + 1, :] = row

    y = _dot(ycat_scr[...], w.out[...])
    y_ref[...] = _rms(x + y, w.normf[...])


def _const_spec(shape):
    nd = len(shape)
    return pl.BlockSpec(shape, lambda *_: (0,) * nd, pipeline_mode=pl.Buffered(1))


def _params(sem):
    return pltpu.CompilerParams(dimension_semantics=sem, vmem_limit_bytes=VMEM_LIMIT)


def _tile_scratch(T):
    return [pltpu.VMEM((T, D_MAIN), F32), pltpu.VMEM((T, D_MODEL), F32), pltpu.VMEM((T, GATE_PAD), F32),
            pltpu.VMEM((T, 2048), BF16), pltpu.VMEM((T, D_MODEL), BF16)]


def _state_scratch():
    return [pltpu.VMEM((HIST, D_POOL), F32), pltpu.VMEM((N_HEADS, HEAD_DIM, HEAD_DIM), F32),
            pltpu.VMEM((8, HEAD_DIM), F32), pltpu.VMEM((8, 128), F32)]


def kernel(x_prompt, x_sample, state_pool, state_C, state_n, state_m, meta_tokens, norm1_w, w_in,
           b_if, w_pool, pool_scale, mhln_w, w_out, normf_w):
    B, S, _ = x_prompt.shape
    SB, SL, _ = x_sample.shape
    TT = PROMPT_TILE
    assert norm1_w.shape[0] == 1, "single layer"
    assert STEP_TILES % 2 == 0 and S % (STEP_TILES * TT) == 0 and SB % SAMPLE_GROUP == 0

    w_in_t = jnp.swapaxes(w_in[0], 0, 1)
    norm1 = norm1_w[0].reshape(1, D_MODEL)
    n_tok = SB * SL
    xs = x_sample.reshape(n_tok, D_MODEL)
    n_seg = D_MAIN // 1024
    wout_rows = w_out.shape[1] // (n_seg + 1)
    seg = lambda i: jnp.minimum(i, n_seg - 1)
    w_main, w_gate, proj_s, wout = pl.pallas_call(
        functools.partial(_prep_kernel, n_seg=n_seg),
        grid=(n_seg + 1,),
        in_specs=[pl.BlockSpec((1024, D_MODEL), lambda i: (seg(i), 0)),
                  pl.BlockSpec((8, D_MODEL), lambda i: (D_MAIN // 8, 0)),
                  _const_spec(xs.shape), _const_spec(norm1.shape),
                  pl.BlockSpec((wout_rows, D_MODEL), lambda i: (i, 0))],
        out_specs=(pl.BlockSpec((D_MODEL, 1024), lambda i: (0, seg(i))),
                   pl.BlockSpec((D_MODEL, GATE_PAD), lambda i: (0, 0)),
                   pl.BlockSpec((n_tok, 1024), lambda i: (0, seg(i))),
                   pl.BlockSpec((wout_rows, D_MODEL), lambda i: (i, 0))),
        out_shape=(jax.ShapeDtypeStruct((D_MODEL, D_MAIN), BF16),
                   jax.ShapeDtypeStruct((D_MODEL, GATE_PAD), BF16),
                   jax.ShapeDtypeStruct((n_tok, D_MAIN), F32),
                   jax.ShapeDtypeStruct(w_out.shape[1:], BF16)),
        scratch_shapes=[pltpu.VMEM((n_tok, D_MODEL), BF16)],
        compiler_params=_params(("arbitrary",)),
        name="prep",
    )(w_in_t, w_in_t, xs, norm1, w_out[0])
    bias_row = jnp.pad(b_if[0], (0, GATE_PAD - 8)).reshape(1, GATE_PAD)
    wpool = w_pool[0].astype(BF16)
    pscale = pool_scale[0].reshape(1, D_POOL)
    mhln = mhln_w[0].reshape(1, D_MLSTM)
    normf = normf_w.reshape(1, D_MODEL)
    layer_w = LayerW(w_main, w_gate, bias_row, norm1, wpool, pscale, mhln, wout, normf)
    layer_specs = [_const_spec(a.shape) for a in layer_w]

    n_tiles = B * S // TT
    steps = n_tiles // STEP_TILES
    steps_per_seq = S // (STEP_TILES * TT)
    xp = x_prompt.reshape(B * S, D_MODEL)
    y_p, pool_p, c_p, n_p, m_p = pl.pallas_call(
        functools.partial(_prompt_kernel, steps_per_seq=steps_per_seq),
        grid=(steps,),
        in_specs=[_const_spec(meta_tokens.shape),
                  pl.BlockSpec((TT, D_MODEL), lambda s: (0, 0))]
                 + [pl.BlockSpec((TT, D_MODEL),
                                 lambda s, j=j: (jnp.minimum(STEP_TILES * s + 1 + j, n_tiles - 1), 0))
                    for j in range(STEP_TILES)]
                 + layer_specs,
        out_specs=(pl.BlockSpec((STEP_TILES * TT, D_MODEL), lambda s: (s, 0)),
                   pl.BlockSpec((1, POOL_BUF, B, D_POOL), lambda s: (0, 0, 0, 0)),
                   pl.BlockSpec((1, 1, N_HEADS, HEAD_DIM, HEAD_DIM),
                                lambda s: (0, s // steps_per_seq, 0, 0, 0)),
                   pl.BlockSpec((1, 1, N_HEADS, HEAD_DIM), lambda s: (0, s // steps_per_seq, 0, 0)),
                   pl.BlockSpec((B, 128), lambda s: (0, 0))),
        out_shape=(jax.ShapeDtypeStruct((B * S, D_MODEL), F32),
                   jax.ShapeDtypeStruct((1, POOL_BUF, B, D_POOL), F32),
                   jax.ShapeDtypeStruct((1, B, N_HEADS, HEAD_DIM, HEAD_DIM), F32),
                   jax.ShapeDtypeStruct((1, B, N_HEADS, HEAD_DIM), F32),
                   jax.ShapeDtypeStruct((B, 128), F32)),
        scratch_shapes=_tile_scratch(TT) + _tile_scratch(TT) + _tile_scratch(N_META)
        + _state_scratch() + _state_scratch(),
        compiler_params=_params(("arbitrary",)),
        name="prompt",
    )(meta_tokens, xp, *([xp] * STEP_TILES), *layer_w)
    y_prompt = y_p.reshape(B, S, D_MODEL)
    pool_p = jnp.swapaxes(pool_p, 1, 2)
    m_p = m_p[:, :N_HEADS].reshape(1, B, N_HEADS)

    G = SAMPLE_GROUP
    GT = G * SL
    pool_in = jnp.swapaxes(state_pool, 1, 2)
    sample_w = layer_w[1:]
    y_s, pool_s, c_s, n_s, m_s = pl.pallas_call(
        functools.partial(_sample_kernel, seq_len=SL, pos0=PAST_LEN),
        grid=(SB // G,),
        in_specs=[pl.BlockSpec((GT, D_MODEL), lambda i: (i, 0)),
                  pl.BlockSpec((GT, D_MAIN), lambda i: (i, 0)),
                  pl.BlockSpec((1, POOL_BUF, G, D_POOL), lambda i: (0, 0, i, 0)),
                  pl.BlockSpec((1, G, N_HEADS, HEAD_DIM, HEAD_DIM), lambda i: (0, i, 0, 0, 0)),
                  pl.BlockSpec((1, G, N_HEADS, HEAD_DIM), lambda i: (0, i, 0, 0)),
                  pl.BlockSpec((1, G, N_HEADS), lambda i: (0, i, 0))]
                 + [_const_spec(a.shape) for a in sample_w],
        out_specs=(pl.BlockSpec((GT, D_MODEL), lambda i: (i, 0)),
                   pl.BlockSpec((1, POOL_BUF, G, D_POOL), lambda i: (0, 0, i, 0)),
                   pl.BlockSpec((1, G, N_HEADS, HEAD_DIM, HEAD_DIM), lambda i: (0, i, 0, 0, 0)),
                   pl.BlockSpec((1, G, N_HEADS, HEAD_DIM), lambda i: (0, i, 0, 0)),
                   pl.BlockSpec((1, G, N_HEADS), lambda i: (0, i, 0))),
        out_shape=(jax.ShapeDtypeStruct((n_tok, D_MODEL), F32),
                   jax.ShapeDtypeStruct(pool_in.shape, F32),
                   jax.ShapeDtypeStruct(state_C.shape, F32),
                   jax.ShapeDtypeStruct(state_n.shape, F32),
                   jax.ShapeDtypeStruct(state_m.shape, F32)),
        scratch_shapes=[pltpu.VMEM((HIST + SL, D_POOL), F32), pltpu.VMEM((GT, 2048), BF16)],
        compiler_params=_params(("arbitrary",)),
        name="sample",
    )(xs, proj_s, pool_in, state_C, state_n, state_m, *sample_w)
    y_sample = y_s.reshape(SB, SL, D_MODEL)
    pool_s = jnp.swapaxes(pool_s, 1, 2)

    return (y_prompt, y_sample, pool_p, c_p, n_p, m_p, pool_s, c_s, n_s, m_s)
```

```python
import collections
import functools

import jax
import jax.numpy as jnp
from jax import lax
from jax.experimental import pallas as pl
from jax.experimental.pallas import tpu as pltpu

D_MODEL = 1024
D_POOL = 1024
D_MLSTM = 1024
N_HEADS = 4
HEAD_DIM = 256
POOL_WINDOWS = (2, 4, 8, 16)
POOL_BUF = 15
HIST = 16
N_META = 16
PAST_LEN = 16384
EPS = 1e-6
D_MAIN = 2 * D_POOL + 5 * D_MLSTM
GATE_PAD = 128
K_SCALE = HEAD_DIM ** -0.5

OFF_U, OFF_ZA, OFF_Q, OFF_K, OFF_V, OFF_O, OFF_ZB = (i * 1024 for i in range(7))

PROMPT_TILE = 256
STEP_TILES = 2
SAMPLE_GROUP = 8
VMEM_LIMIT = 60000 * 1024

F32 = jnp.float32
BF16 = jnp.bfloat16

LayerW = collections.namedtuple("LayerW", "main gate bias norm1 pool pscale mhln out normf")


def _dot(a, b):
    return jnp.dot(a, b, preferred_element_type=F32)


def _dot_nt(a, b):
    return lax.dot_general(a, b, (((1,), (1,)), ((), ())), preferred_element_type=F32)


def _dot_tn(a, b):
    return lax.dot_general(a, b, (((0,), (0,)), ((), ())), preferred_element_type=F32)


def _rms(x, w):
    return x * lax.rsqrt(jnp.mean(x * x, axis=-1, keepdims=True) + EPS) * w


def _log_sigmoid(x):
    return jnp.minimum(x, 0.0) - jnp.log1p(jnp.exp(-jnp.abs(x)))


def _silu(x):
    return x * jax.nn.sigmoid(x)


def _split3(x):
    hi = x.astype(BF16)
    r = x - hi.astype(F32)
    mid = r.astype(BF16)
    lo = (r - mid.astype(F32)).astype(BF16)
    return hi, mid, lo


def _seq_mask(T, seq_len):
    row = lax.broadcasted_iota(jnp.int32, (T, T), 0)
    col = lax.broadcasted_iota(jnp.int32, (T, T), 1)
    causal = col <= row
    if seq_len < T:
        shift = seq_len.bit_length() - 1
        assert 1 << shift == seq_len
        causal = causal & ((row >> shift) == (col >> shift))
    return causal


def _to_rows(cols):
    T = cols.shape[0]
    pad = -T % 128
    if pad:
        cols = jnp.concatenate([cols, jnp.zeros((pad, cols.shape[1]), cols.dtype)], axis=0)
    return cols.T[:, 0:T]


def _gate_pre(xn, w):
    return _dot(xn, w.gate[...]) + w.bias[...]


def _cumsum_operands(g_col, causal):
    tri = jnp.where(causal, 1.0, 0.0).astype(BF16)
    return tri, _split3(_log_sigmoid(g_col))


def _cumsum(tri, parts):
    return sum(_dot(tri, p) for p in parts)


def _intra_weights(qb, kb, b_col, a_col, r_row, causal):
    dm = jnp.where(causal, b_col + r_row, -jnp.inf)
    m_t = jnp.maximum(a_col, jnp.max(dm, axis=-1, keepdims=True))
    w = jnp.exp(dm - m_t)
    inter = jnp.exp(a_col - m_t)
    return m_t, inter, _dot_nt(qb, kb) * (w * K_SCALE)


def _head_norm(ht, w_row):
    mu = jnp.mean(ht, axis=-1, keepdims=True)
    d = ht - mu
    var = jnp.mean(d * d, axis=-1, keepdims=True)
    return d * lax.rsqrt(var + EPS) * w_row


def _window_sums(ext):
    s2 = ext + pltpu.roll(ext, 1, axis=0)
    s4 = s2[:, 256:] + pltpu.roll(s2[:, 256:], 2, axis=0)
    s8 = s4[:, 256:] + pltpu.roll(s4[:, 256:], 4, axis=0)
    s16 = s8[:, 256:] + pltpu.roll(s8[:, 256:], 8, axis=0)
    return [s2[HIST:, 0:256], s4[HIST:, 0:256], s8[HIST:, 0:256], s16[HIST:, 0:256]]


def _pooled(ext, u, pos_col):
    sums = _window_sums(ext)
    return [sums[g] * (1.0 / jnp.minimum(float(w), pos_col + 1.0)) - u[:, g * 256:(g + 1) * 256]
            for g, w in enumerate(POOL_WINDOWS)]


def _pool_mix(pooled, wpool_ref):
    return jnp.concatenate([_dot(p.astype(BF16), wpool_ref[g]) for g, p in enumerate(pooled)], axis=-1)


def _inproj_steps(x_ref, scr, w):
    proj_scr, x_scr, gcol_scr, _, xn_scr = scr

    def norm():
        x = x_ref[...]
        x_scr[...] = x
        xn_scr[...] = _rms(x, w.norm1[...]).astype(BF16)

    def piece(c0):
        act = {OFF_ZA: _silu, OFF_O: jax.nn.sigmoid, OFF_ZB: _silu}.get(c0 // 1024 * 1024, lambda p: p)

        def run():
            proj_scr[:, c0:c0 + PIECE_COLS] = act(_dot(xn_scr[...], w.main[:, c0:c0 + PIECE_COLS]))
        return run

    def gates():
        gcol_scr[...] = _gate_pre(xn_scr[...], w)

    return [norm] + [piece(c0) for c0 in range(0, D_MAIN, PIECE_COLS)] + [gates]


def _mid_steps(scr, pos0, w, state):
    proj_scr, x_scr, gcol_scr, ycat_scr, _ = scr
    hist_scr, c_scr, n_scr, m_scr = state
    T = x_scr.shape[0]

    def seg(off, h=None):
        if h is None:
            return proj_scr[:, off:off + 1024]
        return proj_scr[:, off + h * 256:off + (h + 1) * 256]

    causal = _seq_mask(T, T)
    g_col = gcol_scr[...]
    tri, lf_parts = _cumsum_operands(g_col, causal)
    yield
    b_col_all = _cumsum(tri, lf_parts)
    r_row_all = _to_rows(g_col - pltpu.roll(b_col_all, GATE_PAD - N_HEADS, axis=1))

    u = seg(OFF_U)
    ext = jnp.concatenate([hist_scr[...], u], axis=0)
    pos_col = (lax.broadcasted_iota(jnp.int32, (T, 1), 0) + pos0).astype(F32)
    pooled = _pooled(ext, u, pos_col)
    hist_scr[...] = ext[T:T + HIST, :]
    yield
    mixed = _pool_mix(pooled, w.pool)
    ycat_scr[:, 0:1024] = (mixed * w.pscale[...] * seg(OFF_ZA)).astype(BF16)

    def head(h):
        q, k, v = seg(OFF_Q, h), seg(OFF_K, h), seg(OFF_V, h)
        qb, kb, vb = q.astype(BF16), k.astype(BF16), v.astype(BF16)
        ig_col = g_col[:, h:h + 1]
        b_col = b_col_all[:, 4 + h:5 + h]
        r_row = r_row_all[h:h + 1, :]
        m_prev = m_scr[h:h + 1, 0:1]
        a_col = b_col + m_prev
        c_old = c_scr[h]
        n_old = n_scr[h:h + 1, :]
        q_c = _dot(qb, c_old.astype(BF16))
        m_t, inter, s = _intra_weights(qb, kb, b_col, a_col, r_row, causal)
        sb = s.astype(BF16)
        yield
        num = inter * q_c + _dot(sb, vb)
        qn = inter * jnp.sum(q * n_old, axis=-1, keepdims=True) + jnp.sum(s, axis=-1, keepdims=True)
        ht = num * (1.0 / jnp.maximum(jnp.abs(qn), jnp.exp(-m_t)))
        m_new = m_t[T - 1:T, :]
        b_last = b_col[T - 1:T, :]
        w_end = jnp.exp(b_last - b_col + ig_col - m_new) * K_SCALE
        decay = jnp.exp(b_last + m_prev - m_new)
        kw = k * w_end
        kwb = kw.astype(BF16)
        yield
        c_scr[h] = decay * c_old + _dot_tn(kwb, vb)
        n_scr[h:h + 1, :] = decay * n_old + jnp.sum(kw, axis=0, keepdims=True)
        m_scr[h:h + 1, :] = jnp.broadcast_to(m_new, (1, 128))
        hn = _head_norm(ht, w.mhln[:, h * 256:(h + 1) * 256])
        ycat_scr[:, 1024 + h * 256:1024 + (h + 1) * 256] = (hn * seg(OFF_O, h) * seg(OFF_ZB, h)).astype(BF16)

    for pair in range(0, N_HEADS, HEADS_IN_FLIGHT):
        running = [head(h) for h in range(pair, pair + HEADS_IN_FLIGHT)]
        while running:
            alive = []
            for g in running:
                if next(g, g) is not g:
                    alive.append(g)
                yield
            running = alive


def _out_steps(scr, w, y_ref):
    _, x_scr, _, ycat_scr, _ = scr

    def residual():
        y_ref[...] = x_scr[...]

    def piece(c0):
        def run():
            cols = slice(c0, c0 + OUT_COLS)
            y_ref[:, cols] = y_ref[:, cols] + _dot(ycat_scr[...], w.out[:, cols])
        return run

    def norm():
        y_ref[...] = _rms(y_ref[...], w.normf[...])

    return [residual] + [piece(c0) for c0 in range(0, D_MODEL, OUT_COLS)] + [norm]


PIECE_COLS = 256
OUT_COLS = 256
HEADS_IN_FLIGHT = 4
MID_YIELDS = 2 + 3 * N_HEADS


def _run_interleaved(mid, lead, at_yield):
    assert len(at_yield) == MID_YIELDS
    for piece in lead:
        piece()
    for pieces in at_yield:
        next(mid)
        for piece in pieces:
            piece()
    for _ in mid:
        raise AssertionError("unexpected extra yield")


def _spread(pieces, yields):
    base, extra = divmod(len(pieces), yields)
    groups, start = [], 0
    for i in range(yields):
        stop = start + base + (i < extra)
        groups.append(pieces[start:stop])
        start = stop
    return groups


def _m_row(m_scr):
    lane = lax.broadcasted_iota(jnp.int32, (1, 128), 1)
    row = jnp.zeros((1, 128), F32)
    for h in range(N_HEADS):
        row = jnp.where(lane == h, m_scr[h:h + 1, :], row)
    return row


def _prompt_kernel(meta_ref, x0_ref, *refs, steps_per_seq):
    x_next = refs[:STEP_TILES]
    refs = refs[STEP_TILES:]
    w = LayerW(*refs[:9])
    y_ref, pool_out, c_out, n_out, m_out = refs[9:14]
    scr = (refs[14:19], refs[19:24])
    scr0, scr_meta = scr[0], refs[24:29]
    state, state_meta = refs[29:33], refs[33:37]
    hist_scr, c_scr, n_scr, m_scr = state
    s = pl.program_id(0)
    TT = PROMPT_TILE

    @pl.when(s == 0)
    def _():
        for ref in state_meta:
            ref[...] = jnp.zeros_like(ref)
        for p in _inproj_steps(meta_ref, scr_meta, w):
            p()
        for _ in _mid_steps(scr_meta, 0, w, state_meta):
            pass
        for p in _inproj_steps(x0_ref, scr0, w):
            p()

    @pl.when(s % steps_per_seq == 0)
    def _():
        for ref, ref_meta in zip(state, state_meta):
            ref[...] = ref_meta[...]

    out_pieces = []
    for j in range(STEP_TILES):
        norm, *proj = _inproj_steps(x_next[j], scr[1 - j % 2], w)
        if out_pieces:
            residual, o0, o1, o2, o3, out_norm = out_pieces
            lead = [residual, o0, norm, o1]
            at_yield = [[o2], [o3]] + _spread(proj, MID_YIELDS - 2)
            at_yield[3] = at_yield[3] + [out_norm]
        else:
            lead = [norm, proj[0]]
            at_yield = _spread(proj[1:], MID_YIELDS)
        _run_interleaved(_mid_steps(scr[j % 2], N_META, w, state), lead, at_yield)
        out_pieces = _out_steps(scr[j % 2], w, y_ref.at[j * TT:(j + 1) * TT, :])
    for piece in out_pieces:
        piece()

    @pl.when(s % steps_per_seq == steps_per_seq - 1)
    def _():
        b = s // steps_per_seq
        for j in range(POOL_BUF):
            pool_out[0, j, pl.ds(b, 1), :] = hist_scr[1 + j:2 + j, :]
        c_out[0, 0] = c_scr[...]
        n_out[0, 0] = n_scr[0:N_HEADS, :]
        m_out[pl.ds(b, 1), :] = _m_row(m_scr)


def _prep_kernel(wt_ref, wgt_ref, xs_ref, norm1_ref, wout_ref, wmain_ref, wgate_ref, proj_ref, woutb_ref,
                 xn_scr, *, n_seg):
    i = pl.program_id(0)

    @pl.when(i == 0)
    def _():
        g = jnp.concatenate([wgt_ref[...], jnp.zeros((GATE_PAD - 8, D_MODEL), F32)], axis=0)
        wgate_ref[...] = g.T.astype(BF16)
        xn_scr[...] = _rms(xs_ref[...], norm1_ref[...]).astype(BF16)

    woutb_ref[...] = wout_ref[...].astype(BF16)

    @pl.when(i < n_seg)
    def _():
        wb = wt_ref[...].T.astype(BF16)
        wmain_ref[...] = wb
        proj_ref[...] = _dot(xn_scr[...], wb)


def _expand(seq_col, vals):
    out = None
    for i, val in enumerate(vals):
        pick = jnp.where(seq_col == i, val, 0.0)
        out = pick if out is None else out + pick
    return out


def _sample_kernel(x_ref, proj_ref, pool_ref, c_ref, n_ref, m_ref, *refs, seq_len, pos0):
    w = LayerW(None, *refs[:8])
    y_ref, pool_out, c_out, n_out, m_out = refs[8:13]
    ext_scr, ycat_scr = refs[13:15]
    G = SAMPLE_GROUP
    T = G * seq_len
    x = x_ref[...]
    xn = _rms(x, w.norm1[...]).astype(BF16)
    causal = _seq_mask(T, seq_len)
    g_col = _gate_pre(xn, w)
    b_col_all = _cumsum(*_cumsum_operands(g_col, causal))
    r_row_all = _to_rows(g_col - pltpu.roll(b_col_all, GATE_PAD - N_HEADS, axis=1))
    seq_col = lax.broadcasted_iota(jnp.int32, (T, 1), 0) >> (seq_len.bit_length() - 1)
    pos_col = jnp.full((seq_len, 1), float(pos0), F32) + lax.broadcasted_iota(
        jnp.int32, (seq_len, 1), 0).astype(F32)

    pooled_rows = []
    for i in range(G):
        rows = slice(i * seq_len, (i + 1) * seq_len)
        u_i = proj_ref[rows, OFF_U:OFF_U + 1024]
        ext_scr[0:1, :] = jnp.zeros((1, 1024), F32)
        for j in range(POOL_BUF):
            ext_scr[1 + j:2 + j, :] = pool_ref[0, j, i:i + 1, :]
        ext_scr[HIST:HIST + seq_len, :] = u_i
        ext = ext_scr[...]
        pooled_rows.append(_pooled(ext, u_i, pos_col))
        for j in range(POOL_BUF):
            pool_out[0, j, i:i + 1, :] = ext[seq_len + 1 + j:seq_len + 2 + j, :]
    mixed = _pool_mix([jnp.concatenate(p, axis=0) for p in zip(*pooled_rows)], w.pool)
    y_a = mixed * w.pscale[...] * _silu(proj_ref[:, OFF_ZA:OFF_ZA + 1024])
    ycat_scr[:, 0:1024] = y_a.astype(BF16)

    lane4 = lax.broadcasted_iota(jnp.int32, (1, N_HEADS), 1)
    last = [(i + 1) * seq_len - 1 for i in range(G)]
    m_new_heads = [None] * N_HEADS

    def head(h):
        hc = slice(h * 256, (h + 1) * 256)
        q = proj_ref[:, OFF_Q + h * 256:OFF_Q + (h + 1) * 256]
        k = proj_ref[:, OFF_K + h * 256:OFF_K + (h + 1) * 256]
        v = proj_ref[:, OFF_V + h * 256:OFF_V + (h + 1) * 256]
        qb, kb, vb = q.astype(BF16), k.astype(BF16), v.astype(BF16)
        c_old = [c_ref[0, i, h] for i in range(G)]
        c_old_b = [c.astype(BF16) for c in c_old]
        yield
        q_cs = [_dot(qb, cb) for cb in c_old_b]
        ig_col = g_col[:, h:h + 1]
        b_col = b_col_all[:, 4 + h:5 + h]
        r_row = r_row_all[h:h + 1, :]
        m_prev = [m_ref[0, i:i + 1, h:h + 1] for i in range(G)]
        a_col = b_col + _expand(seq_col, m_prev)
        m_t, inter, s = _intra_weights(qb, kb, b_col, a_col, r_row, causal)
        sb = s.astype(BF16)
        yield
        sv = _dot(sb, vb)
        n_old = [n_ref[0, i, h:h + 1, :] for i in range(G)]
        num = inter * _expand(seq_col, q_cs) + sv
        qn = inter * jnp.sum(q * _expand(seq_col, n_old), axis=-1, keepdims=True) + jnp.sum(
            s, axis=-1, keepdims=True)
        ht = num * (1.0 / jnp.maximum(jnp.abs(qn), jnp.exp(-m_t)))
        m_new = [m_t[r:r + 1, :] for r in last]
        b_last = [b_col[r:r + 1, :] for r in last]
        w_end = jnp.exp(_expand(seq_col, b_last) - b_col + ig_col - _expand(seq_col, m_new)) * K_SCALE
        kw = k * w_end
        kwb = kw.astype(BF16)
        v_seq = [jnp.where(seq_col == i, vb, jnp.zeros_like(vb)) for i in range(G)]
        yield
        updates = [_dot_tn(kwb, v_i) for v_i in v_seq]
        for i in range(G):
            decay = jnp.exp(b_last[i] + m_prev[i] - m_new[i])
            c_out[0, i, h] = decay * c_old[i] + updates[i]
            n_out[0, i, h:h + 1, :] = decay * n_old[i] + jnp.sum(
                kw[i * seq_len:(i + 1) * seq_len, :], axis=0, keepdims=True)
        m_new_heads[h] = m_new
        hn = _head_norm(ht, w.mhln[:, hc])
        o = proj_ref[:, OFF_O + h * 256:OFF_O + (h + 1) * 256]
        zb = proj_ref[:, OFF_ZB + h * 256:OFF_ZB + (h + 1) * 256]
        ycat_scr[:, 1024 + h * 256:1024 + (h + 1) * 256] = (hn * jax.nn.sigmoid(o) * _silu(zb)).astype(BF16)

    running = [head(h) for h in range(N_HEADS)]
    while running:
        running = [g for g in running if next(g, g) is not g]
    for i in range(G):
        row = jnp.zeros((1, N_HEADS), F32)
        for h in range(N_HEADS):
            row = jnp.where(lane4 == h, m_new_heads[h][i], row)
        m_out[0, i:i + 1, :] = row

    y = _dot(ycat_scr[...], w.out[...])
    y_ref[...] = _rms(x + y, w.normf[...])


def _const_spec(shape):
    nd = len(shape)
    return pl.BlockSpec(shape, lambda *_: (0,) * nd, pipeline_mode=pl.Buffered(1))


def _params(sem):
    return pltpu.CompilerParams(dimension_semantics=sem, vmem_limit_bytes=VMEM_LIMIT)


def _tile_scratch(T):
    return [pltpu.VMEM((T, D_MAIN), F32), pltpu.VMEM((T, D_MODEL), F32), pltpu.VMEM((T, GATE_PAD), F32),
            pltpu.VMEM((T, 2048), BF16), pltpu.VMEM((T, D_MODEL), BF16)]


def _state_scratch():
    return [pltpu.VMEM((HIST, D_POOL), F32), pltpu.VMEM((N_HEADS, HEAD_DIM, HEAD_DIM), F32),
            pltpu.VMEM((8, HEAD_DIM), F32), pltpu.VMEM((8, 128), F32)]


def kernel(x_prompt, x_sample, state_pool, state_C, state_n, state_m, meta_tokens, norm1_w, w_in,
           b_if, w_pool, pool_scale, mhln_w, w_out, normf_w):
    B, S, _ = x_prompt.shape
    SB, SL, _ = x_sample.shape
    TT = PROMPT_TILE
    assert norm1_w.shape[0] == 1, "single layer"
    assert STEP_TILES % 2 == 0 and S % (STEP_TILES * TT) == 0 and SB % SAMPLE_GROUP == 0

    w_in_t = jnp.swapaxes(w_in[0], 0, 1)
    norm1 = norm1_w[0].reshape(1, D_MODEL)
    n_tok = SB * SL
    xs = x_sample.reshape(n_tok, D_MODEL)
    n_seg = D_MAIN // 1024
    wout_rows = w_out.shape[1] // (n_seg + 1)
    seg = lambda i: jnp.minimum(i, n_seg - 1)
    w_main, w_gate, proj_s, wout = pl.pallas_call(
        functools.partial(_prep_kernel, n_seg=n_seg),
        grid=(n_seg + 1,),
        in_specs=[pl.BlockSpec((1024, D_MODEL), lambda i: (seg(i), 0)),
                  pl.BlockSpec((8, D_MODEL), lambda i: (D_MAIN // 8, 0)),
                  _const_spec(xs.shape), _const_spec(norm1.shape),
                  pl.BlockSpec((wout_rows, D_MODEL), lambda i: (i, 0))],
        out_specs=(pl.BlockSpec((D_MODEL, 1024), lambda i: (0, seg(i))),
                   pl.BlockSpec((D_MODEL, GATE_PAD), lambda i: (0, 0)),
                   pl.BlockSpec((n_tok, 1024), lambda i: (0, seg(i))),
                   pl.BlockSpec((wout_rows, D_MODEL), lambda i: (i, 0))),
        out_shape=(jax.ShapeDtypeStruct((D_MODEL, D_MAIN), BF16),
                   jax.ShapeDtypeStruct((D_MODEL, GATE_PAD), BF16),
                   jax.ShapeDtypeStruct((n_tok, D_MAIN), F32),
                   jax.ShapeDtypeStruct(w_out.shape[1:], BF16)),
        scratch_shapes=[pltpu.VMEM((n_tok, D_MODEL), BF16)],
        compiler_params=_params(("arbitrary",)),
        name="prep",
    )(w_in_t, w_in_t, xs, norm1, w_out[0])
    bias_row = jnp.pad(b_if[0], (0, GATE_PAD - 8)).reshape(1, GATE_PAD)
    wpool = w_pool[0].astype(BF16)
    pscale = pool_scale[0].reshape(1, D_POOL)
    mhln = mhln_w[0].reshape(1, D_MLSTM)
    normf = normf_w.reshape(1, D_MODEL)
    layer_w = LayerW(w_main, w_gate, bias_row, norm1, wpool, pscale, mhln, wout, normf)
    layer_specs = [_const_spec(a.shape) for a in layer_w]

    n_tiles = B * S // TT
    steps = n_tiles // STEP_TILES
    steps_per_seq = S // (STEP_TILES * TT)
    xp = x_prompt.reshape(B * S, D_MODEL)
    y_p, pool_p, c_p, n_p, m_p = pl.pallas_call(
        functools.partial(_prompt_kernel, steps_per_seq=steps_per_seq),
        grid=(steps,),
        in_specs=[_const_spec(meta_tokens.shape),
                  pl.BlockSpec((TT, D_MODEL), lambda s: (0, 0))]
                 + [pl.BlockSpec((TT, D_MODEL),
                                 lambda s, j=j: (jnp.minimum(STEP_TILES * s + 1 + j, n_tiles - 1), 0))
                    for j in range(STEP_TILES)]
                 + layer_specs,
        out_specs=(pl.BlockSpec((STEP_TILES * TT, D_MODEL), lambda s: (s, 0)),
                   pl.BlockSpec((1, POOL_BUF, B, D_POOL), lambda s: (0, 0, 0, 0)),
                   pl.BlockSpec((1, 1, N_HEADS, HEAD_DIM, HEAD_DIM),
                                lambda s: (0, s // steps_per_seq, 0, 0, 0)),
                   pl.BlockSpec((1, 1, N_HEADS, HEAD_DIM), lambda s: (0, s // steps_per_seq, 0, 0)),
                   pl.BlockSpec((B, 128), lambda s: (0, 0))),
        out_shape=(jax.ShapeDtypeStruct((B * S, D_MODEL), F32),
                   jax.ShapeDtypeStruct((1, POOL_BUF, B, D_POOL), F32),
                   jax.ShapeDtypeStruct((1, B, N_HEADS, HEAD_DIM, HEAD_DIM), F32),
                   jax.ShapeDtypeStruct((1, B, N_HEADS, HEAD_DIM), F32),
                   jax.ShapeDtypeStruct((B, 128), F32)),
        scratch_shapes=_tile_scratch(TT) + _tile_scratch(TT) + _tile_scratch(N_META)
        + _state_scratch() + _state_scratch(),
        compiler_params=_params(("arbitrary",)),
        name="prompt",
    )(meta_tokens, xp, *([xp] * STEP_TILES), *layer_w)
    y_prompt = y_p.reshape(B, S, D_MODEL)
    pool_p = jnp.swapaxes(pool_p, 1, 2)
    m_p = m_p[:, :N_HEADS].reshape(1, B, N_HEADS)

    G = SAMPLE_GROUP
    GT = G * SL
    pool_in = jnp.swapaxes(state_pool, 1, 2)
    sample_w = layer_w[1:]
    y_s, pool_s, c_s, n_s, m_s = pl.pallas_call(
        functools.partial(_sample_kernel, seq_len=SL, pos0=PAST_LEN),
        grid=(SB // G,),
        in_specs=[pl.BlockSpec((GT, D_MODEL), lambda i: (i, 0)),
                  pl.BlockSpec((GT, D_MAIN), lambda i: (i, 0)),
                  pl.BlockSpec((1, POOL_BUF, G, D_POOL), lambda i: (0, 0, i, 0)),
                  pl.BlockSpec((1, G, N_HEADS, HEAD_DIM, HEAD_DIM), lambda i: (0, i, 0, 0, 0)),
                  pl.BlockSpec((1, G, N_HEADS, HEAD_DIM), lambda i: (0, i, 0, 0)),
                  pl.BlockSpec((1, G, N_HEADS), lambda i: (0, i, 0))]
                 + [_const_spec(a.shape) for a in sample_w],
        out_specs=(pl.BlockSpec((GT, D_MODEL), lambda i: (i, 0)),
                   pl.BlockSpec((1, POOL_BUF, G, D_POOL), lambda i: (0, 0, i, 0)),
                   pl.BlockSpec((1, G, N_HEADS, HEAD_DIM, HEAD_DIM), lambda i: (0, i, 0, 0, 0)),
                   pl.BlockSpec((1, G, N_HEADS, HEAD_DIM), lambda i: (0, i, 0, 0)),
                   pl.BlockSpec((1, G, N_HEADS), lambda i: (0, i, 0))),
        out_shape=(jax.ShapeDtypeStruct((n_tok, D_MODEL), F32),
                   jax.ShapeDtypeStruct(pool_in.shape, F32),
                   jax.ShapeDtypeStruct(state_C.shape, F32),
                   jax.ShapeDtypeStruct(state_n.shape, F32),
                   jax.ShapeDtypeStruct(state_m.shape, F32)),
        scratch_shapes=[pltpu.VMEM((HIST + SL, D_POOL), F32), pltpu.VMEM((GT, 2048), BF16)],
        compiler_params=_params(("arbitrary",)),
        name="sample",
    )(xs, proj_s, pool_in, state_C, state_n, state_m, *sample_w)
    y_sample = y_s.reshape(SB, SL, D_MODEL)
    pool_s = jnp.swapaxes(pool_s, 1, 2)

    return (y_prompt, y_sample, pool_p, c_p, n_p, m_p, pool_s, c_s, n_s, m_s)
```

```python
import collections
import functools

import jax
import jax.numpy as jnp
from jax import lax
from jax.experimental import pallas as pl
from jax.experimental.pallas import tpu as pltpu

D_MODEL = 1024
D_POOL = 1024
D_MLSTM = 1024
N_HEADS = 4
HEAD_DIM = 256
POOL_WINDOWS = (2, 4, 8, 16)
POOL_BUF = 15
HIST = 16
N_META = 16
PAST_LEN = 16384
EPS = 1e-6
D_MAIN = 2 * D_POOL + 5 * D_MLSTM
GATE_PAD = 128
K_SCALE = HEAD_DIM ** -0.5

OFF_U, OFF_ZA, OFF_Q, OFF_K, OFF_V, OFF_O, OFF_ZB = (i * 1024 for i in range(7))

PROMPT_TILE = 256
STEP_TILES = 2
SAMPLE_GROUP = 8
VMEM_LIMIT = 60000 * 1024

F32 = jnp.float32
BF16 = jnp.bfloat16

LayerW = collections.namedtuple("LayerW", "main gate bias norm1 pool pscale mhln out normf")


def _dot(a, b):
    return jnp.dot(a, b, preferred_element_type=F32)


def _dot_nt(a, b):
    return lax.dot_general(a, b, (((1,), (1,)), ((), ())), preferred_element_type=F32)


def _dot_tn(a, b):
    return lax.dot_general(a, b, (((0,), (0,)), ((), ())), preferred_element_type=F32)


def _rms(x, w):
    return x * lax.rsqrt(jnp.mean(x * x, axis=-1, keepdims=True) + EPS) * w


def _log_sigmoid(x):
    return jnp.minimum(x, 0.0) - jnp.log1p(jnp.exp(-jnp.abs(x)))


def _silu(x):
    return x * jax.nn.sigmoid(x)


def _split3(x):
    hi = x.astype(BF16)
    r = x - hi.astype(F32)
    mid = r.astype(BF16)
    lo = (r - mid.astype(F32)).astype(BF16)
    return hi, mid, lo


def _seq_mask(T, seq_len):
    row = lax.broadcasted_iota(jnp.int32, (T, T), 0)
    col = lax.broadcasted_iota(jnp.int32, (T, T), 1)
    causal = col <= row
    if seq_len < T:
        shift = seq_len.bit_length() - 1
        assert 1 << shift == seq_len
        causal = causal & ((row >> shift) == (col >> shift))
    return causal


def _to_rows(cols):
    T = cols.shape[0]
    pad = -T % 128
    if pad:
        cols = jnp.concatenate([cols, jnp.zeros((pad, cols.shape[1]), cols.dtype)], axis=0)
    return cols.T[:, 0:T]


def _gate_pre(xn, w):
    return _dot(xn, w.gate[...]) + w.bias[...]


def _cumsum_operands(g_col, causal):
    tri = jnp.where(causal, 1.0, 0.0).astype(BF16)
    return tri, _split3(_log_sigmoid(g_col))


def _cumsum(tri, parts):
    return sum(_dot(tri, p) for p in parts)


def _intra_weights(qb, kb, b_col, a_col, r_row, causal):
    dm = jnp.where(causal, b_col + r_row, -jnp.inf)
    m_t = jnp.maximum(a_col, jnp.max(dm, axis=-1, keepdims=True))
    w = jnp.exp(dm - m_t)
    inter = jnp.exp(a_col - m_t)
    return m_t, inter, _dot_nt(qb, kb) * (w * K_SCALE)


def _head_norm(ht, w_row):
    mu = jnp.mean(ht, axis=-1, keepdims=True)
    d = ht - mu
    var = jnp.mean(d * d, axis=-1, keepdims=True)
    return d * lax.rsqrt(var + EPS) * w_row


def _window_sums(ext):
    s2 = ext + pltpu.roll(ext, 1, axis=0)
    s4 = s2[:, 256:] + pltpu.roll(s2[:, 256:], 2, axis=0)
    s8 = s4[:, 256:] + pltpu.roll(s4[:, 256:], 4, axis=0)
    s16 = s8[:, 256:] + pltpu.roll(s8[:, 256:], 8, axis=0)
    return [s2[HIST:, 0:256], s4[HIST:, 0:256], s8[HIST:, 0:256], s16[HIST:, 0:256]]


def _pooled(ext, u, pos_col):
    sums = _window_sums(ext)
    return [sums[g] * (1.0 / jnp.minimum(float(w), pos_col + 1.0)) - u[:, g * 256:(g + 1) * 256]
            for g, w in enumerate(POOL_WINDOWS)]


def _pooled_group(ext_g, u_g, pos_col, g):
    s = ext_g
    for k in range(g + 1):
        s = s + pltpu.roll(s, 2 ** k, axis=0)
    return s[HIST:] * (1.0 / jnp.minimum(float(POOL_WINDOWS[g]), pos_col + 1.0)) - u_g


def _pool_mix(pooled, wpool_ref):
    return jnp.concatenate([_dot(p.astype(BF16), wpool_ref[g]) for g, p in enumerate(pooled)], axis=-1)


def _inproj_steps(srcs, w):
    PROJ, X, GCOL, XN = 0, 1, 2, 4

    def norm():
        for x_ref, scr in srcs:
            x = x_ref[...]
            scr[X][...] = x
            scr[XN][...] = _rms(x, w.norm1[...]).astype(BF16)

    def xn_rows():
        parts = [scr[XN][...] for _, scr in srcs]
        return parts[0] if len(parts) == 1 else jnp.concatenate(parts, axis=0)

    def store(which, cols, val):
        r0 = 0
        for _, scr in srcs:
            rows = scr[X].shape[0]
            scr[which][:, cols] = val[r0:r0 + rows]
            r0 += rows

    def piece(c0):
        act = {OFF_ZA: _silu, OFF_O: jax.nn.sigmoid, OFF_ZB: _silu}.get(c0 // 1024 * 1024, lambda p: p)

        def run():
            store(PROJ, slice(c0, c0 + PIECE_COLS), act(_dot(xn_rows(), w.main[:, c0:c0 + PIECE_COLS])))
        return run

    def gates():
        store(GCOL, slice(None), _gate_pre(xn_rows(), w))

    return [norm] + [piece(c0) for c0 in range(0, D_MAIN, PIECE_COLS)] + [gates]


def _mid_steps(scr, pos0, w, state):
    proj_scr, x_scr, gcol_scr, ycat_scr, _ = scr
    hist_scr, c_scr, n_scr, m_scr = state
    T = x_scr.shape[0]

    def seg(off, h=None):
        if h is None:
            return proj_scr[:, off:off + 1024]
        return proj_scr[:, off + h * 256:off + (h + 1) * 256]

    causal = _seq_mask(T, T)
    g_col = gcol_scr[...]
    tri, lf_parts = _cumsum_operands(g_col, causal)
    yield
    b_col_all = _cumsum(tri, lf_parts)
    r_row_all = _to_rows(g_col - pltpu.roll(b_col_all, GATE_PAD - N_HEADS, axis=1))

    pos_col = (lax.broadcasted_iota(jnp.int32, (T, 1), 0) + pos0).astype(F32)
    pooled = []
    for g in range(len(POOL_WINDOWS)):
        cols = slice(g * 256, (g + 1) * 256)
        u_g = seg(OFF_U, g)
        ext_g = jnp.concatenate([hist_scr[:, cols], u_g], axis=0)
        pooled.append(_pooled_group(ext_g, u_g, pos_col, g).astype(BF16))
        hist_scr[:, cols] = ext_g[T:T + HIST, :]
    yield
    for g, p in enumerate(pooled):
        cols = slice(g * 256, (g + 1) * 256)
        ycat_scr[:, cols] = (_dot(p, w.pool[g]) * w.pscale[:, cols] * seg(OFF_ZA, g)).astype(BF16)

    def head(h):
        q, k, v = seg(OFF_Q, h), seg(OFF_K, h), seg(OFF_V, h)
        qb, kb, vb = q.astype(BF16), k.astype(BF16), v.astype(BF16)
        ig_col = g_col[:, h:h + 1]
        b_col = b_col_all[:, 4 + h:5 + h]
        r_row = r_row_all[h:h + 1, :]
        m_prev = m_scr[h:h + 1, 0:1]
        a_col = b_col + m_prev
        c_old = c_scr[h]
        n_old = n_scr[h:h + 1, :]
        q_c = _dot(qb, c_old.astype(BF16))
        m_t, inter, s = _intra_weights(qb, kb, b_col, a_col, r_row, causal)
        sb = s.astype(BF16)
        yield
        num = inter * q_c + _dot(sb, vb)
        qn = inter * jnp.sum(q * n_old, axis=-1, keepdims=True) + jnp.sum(s, axis=-1, keepdims=True)
        ht = num * (1.0 / jnp.maximum(jnp.abs(qn), jnp.exp(-m_t)))
        m_new = m_t[T - 1:T, :]
        b_last = b_col[T - 1:T, :]
        w_end = jnp.exp(b_last - b_col + ig_col - m_new) * K_SCALE
        decay = jnp.exp(b_last + m_prev - m_new)
        kw = k * w_end
        kwb = kw.astype(BF16)
        yield
        c_scr[h] = decay * c_old + _dot_tn(kwb, vb)
        n_scr[h:h + 1, :] = decay * n_old + jnp.sum(kw, axis=0, keepdims=True)
        m_scr[h:h + 1, :] = jnp.broadcast_to(m_new, (1, 128))
        hn = _head_norm(ht, w.mhln[:, h * 256:(h + 1) * 256])
        ycat_scr[:, 1024 + h * 256:1024 + (h + 1) * 256] = (hn * seg(OFF_O, h) * seg(OFF_ZB, h)).astype(BF16)

    for pair in range(0, N_HEADS, HEADS_IN_FLIGHT):
        running = [head(h) for h in range(pair, pair + HEADS_IN_FLIGHT)]
        while running:
            alive = []
            for g in running:
                if next(g, g) is not g:
                    alive.append(g)
                yield
            running = alive


def _out_steps(scr, w, y_ref):
    _, x_scr, _, ycat_scr, _ = scr

    def residual():
        y_ref[...] = x_scr[...]

    def piece(c0):
        def run():
            cols = slice(c0, c0 + OUT_COLS)
            y_ref[:, cols] = y_ref[:, cols] + _dot(ycat_scr[...], w.out[:, cols])
        return run

    def norm():
        y_ref[...] = _rms(y_ref[...], w.normf[...])

    return [residual] + [piece(c0) for c0 in range(0, D_MODEL, OUT_COLS)] + [norm]


PIECE_COLS = 256
OUT_COLS = 256
HEADS_IN_FLIGHT = 2
MID_YIELDS = 2 + 3 * N_HEADS


def _run_interleaved(mid, lead, at_yield):
    assert len(at_yield) == MID_YIELDS
    for piece in lead:
        piece()
    for pieces in at_yield:
        next(mid)
        for piece in pieces:
            piece()
    for _ in mid:
        raise AssertionError("unexpected extra yield")


def _spread(pieces, yields):
    base, extra = divmod(len(pieces), yields)
    groups, start = [], 0
    for i in range(yields):
        stop = start + base + (i < extra)
        groups.append(pieces[start:stop])
        start = stop
    return groups


def _m_row(m_scr):
    lane = lax.broadcasted_iota(jnp.int32, (1, 128), 1)
    row = jnp.zeros((1, 128), F32)
    for h in range(N_HEADS):
        row = jnp.where(lane == h, m_scr[h:h + 1, :], row)
    return row


def _prompt_kernel(meta_ref, x0_ref, *refs, steps_per_seq):
    x_next = refs[:STEP_TILES]
    refs = refs[STEP_TILES:]
    w = LayerW(*refs[:9])
    y_ref, pool_out, c_out, n_out, m_out = refs[9:14]
    scr = (refs[14:19], refs[19:24])
    scr0, scr_meta = scr[0], refs[24:29]
    state, state_meta = refs[29:33], refs[33:37]
    hist_scr, c_scr, n_scr, m_scr = state
    s = pl.program_id(0)
    TT = PROMPT_TILE

    @pl.when(s == 0)
    def _():
        for ref in state_meta:
            ref[...] = jnp.zeros_like(ref)
        for p in _inproj_steps([(meta_ref, scr_meta), (x0_ref, scr0)], w):
            p()
        for _ in _mid_steps(scr_meta, 0, w, state_meta):
            pass

    @pl.when(s % steps_per_seq == 0)
    def _():
        for ref, ref_meta in zip(state, state_meta):
            ref[...] = ref_meta[...]

    out_pieces = []
    for j in range(STEP_TILES):
        norm, *proj = _inproj_steps([(x_next[j], scr[1 - j % 2])], w)
        if out_pieces:
            residual, o0, o1, o2, o3, out_norm = out_pieces
            lead = [residual, o0, norm, o1]
            at_yield = [[o2], [o3]] + _spread(proj, MID_YIELDS - 2)
            at_yield[3] = at_yield[3] + [out_norm]
        else:
            lead = [norm, proj[0]]
            at_yield = _spread(proj[1:], MID_YIELDS)
        _run_interleaved(_mid_steps(scr[j % 2], N_META, w, state), lead, at_yield)
        out_pieces = _out_steps(scr[j % 2], w, y_ref.at[j * TT:(j + 1) * TT, :])
    for piece in out_pieces:
        piece()

    @pl.when(s % steps_per_seq == steps_per_seq - 1)
    def _():
        b = s // steps_per_seq
        for j in range(POOL_BUF):
            pool_out[0, j, pl.ds(b, 1), :] = hist_scr[1 + j:2 + j, :]
        c_out[0, 0] = c_scr[...]
        n_out[0, 0] = n_scr[0:N_HEADS, :]
        m_out[pl.ds(b, 1), :] = _m_row(m_scr)


def _prep_kernel(wt_ref, wgt_ref, xs_ref, norm1_ref, wout_ref, wmain_ref, wgate_ref, proj_ref, woutb_ref,
                 xn_scr, *, n_seg):
    i = pl.program_id(0)

    @pl.when(i == 0)
    def _():
        g = jnp.concatenate([wgt_ref[...], jnp.zeros((GATE_PAD - 8, D_MODEL), F32)], axis=0)
        wgate_ref[...] = g.T.astype(BF16)
        xn_scr[...] = _rms(xs_ref[...], norm1_ref[...]).astype(BF16)

    woutb_ref[...] = wout_ref[...].astype(BF16)

    @pl.when(i < n_seg)
    def _():
        wb = wt_ref[...].T.astype(BF16)
        wmain_ref[...] = wb
        proj_ref[...] = _dot(xn_scr[...], wb)


def _expand(seq_col, vals):
    out = None
    for i, val in enumerate(vals):
        pick = jnp.where(seq_col == i, val, 0.0)
        out = pick if out is None else out + pick
    return out


def _sample_kernel(x_ref, proj_ref, pool_ref, c_ref, n_ref, m_ref, *refs, seq_len, pos0):
    w = LayerW(None, *refs[:8])
    y_ref, pool_out, c_out, n_out, m_out = refs[8:13]
    ext_scr, ycat_scr = refs[13:15]
    G = SAMPLE_GROUP
    T = G * seq_len
    x = x_ref[...]
    xn = _rms(x, w.norm1[...]).astype(BF16)
    causal = _seq_mask(T, seq_len)
    g_col = _gate_pre(xn, w)
    b_col_all = _cumsum(*_cumsum_operands(g_col, causal))
    r_row_all = _to_rows(g_col - pltpu.roll(b_col_all, GATE_PAD - N_HEADS, axis=1))
    seq_col = lax.broadcasted_iota(jnp.int32, (T, 1), 0) >> (seq_len.bit_length() - 1)
    pos_col = jnp.full((seq_len, 1), float(pos0), F32) + lax.broadcasted_iota(
        jnp.int32, (seq_len, 1), 0).astype(F32)

    pooled_rows = []
    for i in range(G):
        rows = slice(i * seq_len, (i + 1) * seq_len)
        u_i = proj_ref[rows, OFF_U:OFF_U + 1024]
        ext_scr[0:1, :] = jnp.zeros((1, 1024), F32)
        for j in range(POOL_BUF):
            ext_scr[1 + j:2 + j, :] = pool_ref[0, j, i:i + 1, :]
        ext_scr[HIST:HIST + seq_len, :] = u_i
        ext = ext_scr[...]
        pooled_rows.append(_pooled(ext, u_i, pos_col))
        for j in range(POOL_BUF):
            pool_out[0, j, i:i + 1, :] = ext[seq_len + 1 + j:seq_len + 2 + j, :]
    mixed = _pool_mix([jnp.concatenate(p, axis=0) for p in zip(*pooled_rows)], w.pool)
    y_a = mixed * w.pscale[...] * _silu(proj_ref[:, OFF_ZA:OFF_ZA + 1024])
    ycat_scr[:, 0:1024] = y_a.astype(BF16)

    lane4 = lax.broadcasted_iota(jnp.int32, (1, N_HEADS), 1)
    last = [(i + 1) * seq_len - 1 for i in range(G)]
    m_new_heads = [None] * N_HEADS

    def head(h):
        hc = slice(h * 256, (h + 1) * 256)
        q = proj_ref[:, OFF_Q + h * 256:OFF_Q + (h + 1) * 256]
        k = proj_ref[:, OFF_K + h * 256:OFF_K + (h + 1) * 256]
        v = proj_ref[:, OFF_V + h * 256:OFF_V + (h + 1) * 256]
        qb, kb, vb = q.astype(BF16), k.astype(BF16), v.astype(BF16)
        c_old = [c_ref[0, i, h] for i in range(G)]
        c_old_b = [c.astype(BF16) for c in c_old]
        yield
        q_cs = [_dot(qb, cb) for cb in c_old_b]
        ig_col = g_col[:, h:h + 1]
        b_col = b_col_all[:, 4 + h:5 + h]
        r_row = r_row_all[h:h + 1, :]
        m_prev = [m_ref[0, i:i + 1, h:h + 1] for i in range(G)]
        a_col = b_col + _expand(seq_col, m_prev)
        m_t, inter, s = _intra_weights(qb, kb, b_col, a_col, r_row, causal)
        sb = s.astype(BF16)
        yield
        sv = _dot(sb, vb)
        n_old = [n_ref[0, i, h:h + 1, :] for i in range(G)]
        num = inter * _expand(seq_col, q_cs) + sv
        qn = inter * jnp.sum(q * _expand(seq_col, n_old), axis=-1, keepdims=True) + jnp.sum(
            s, axis=-1, keepdims=True)
        ht = num * (1.0 / jnp.maximum(jnp.abs(qn), jnp.exp(-m_t)))
        m_new = [m_t[r:r + 1, :] for r in last]
        b_last = [b_col[r:r + 1, :] for r in last]
        w_end = jnp.exp(_expand(seq_col, b_last) - b_col + ig_col - _expand(seq_col, m_new)) * K_SCALE
        kw = k * w_end
        kwb = kw.astype(BF16)
        v_seq = [jnp.where(seq_col == i, vb, jnp.zeros_like(vb)) for i in range(G)]
        yield
        updates = [_dot_tn(kwb, v_i) for v_i in v_seq]
        for i in range(G):
            decay = jnp.exp(b_last[i] + m_prev[i] - m_new[i])
            c_out[0, i, h] = decay * c_old[i] + updates[i]
            n_out[0, i, h:h + 1, :] = decay * n_old[i] + jnp.sum(
                kw[i * seq_len:(i + 1) * seq_len, :], axis=0, keepdims=True)
        m_new_heads[h] = m_new
        hn = _head_norm(ht, w.mhln[:, hc])
        o = proj_ref[:, OFF_O + h * 256:OFF_O + (h + 1) * 256]
        zb = proj_ref[:, OFF_ZB + h * 256:OFF_ZB + (h + 1) * 256]
        ycat_scr[:, 1024 + h * 256:1024 + (h + 1) * 256] = (hn * jax.nn.sigmoid(o) * _silu(zb)).astype(BF16)

    running = [head(h) for h in range(N_HEADS)]
    while running:
        running = [g for g in running if next(g, g) is not g]
    for i in range(G):
        row = jnp.zeros((1, N_HEADS), F32)
        for h in range(N_HEADS):
            row = jnp.where(lane4 == h, m_new_heads[h][i], row)
        m_out[0, i:i + 1, :] = row

    y = _dot(ycat_scr[...], w.out[...])
    y_ref[...] = _rms(x + y, w.normf[...])


def _const_spec(shape):
    nd = len(shape)
    return pl.BlockSpec(shape, lambda *_: (0,) * nd, pipeline_mode=pl.Buffered(1))


def _params(sem):
    return pltpu.CompilerParams(dimension_semantics=sem, vmem_limit_bytes=VMEM_LIMIT)


def _tile_scratch(T):
    return [pltpu.VMEM((T, D_MAIN), F32), pltpu.VMEM((T, D_MODEL), F32), pltpu.VMEM((T, GATE_PAD), F32),
            pltpu.VMEM((T, 2048), BF16), pltpu.VMEM((T, D_MODEL), BF16)]


def _state_scratch():
    return [pltpu.VMEM((HIST, D_POOL), F32), pltpu.VMEM((N_HEADS, HEAD_DIM, HEAD_DIM), F32),
            pltpu.VMEM((8, HEAD_DIM), F32), pltpu.VMEM((8, 128), F32)]


def kernel(x_prompt, x_sample, state_pool, state_C, state_n, state_m, meta_tokens, norm1_w, w_in,
           b_if, w_pool, pool_scale, mhln_w, w_out, normf_w):
    B, S, _ = x_prompt.shape
    SB, SL, _ = x_sample.shape
    TT = PROMPT_TILE
    assert norm1_w.shape[0] == 1, "single layer"
    assert STEP_TILES % 2 == 0 and S % (STEP_TILES * TT) == 0 and SB % SAMPLE_GROUP == 0

    w_in_t = jnp.swapaxes(w_in[0], 0, 1)
    norm1 = norm1_w[0].reshape(1, D_MODEL)
    n_tok = SB * SL
    xs = x_sample.reshape(n_tok, D_MODEL)
    n_seg = D_MAIN // 1024
    wout_rows = w_out.shape[1] // (n_seg + 1)
    seg = lambda i: jnp.minimum(i, n_seg - 1)
    w_main, w_gate, proj_s, wout = pl.pallas_call(
        functools.partial(_prep_kernel, n_seg=n_seg),
        grid=(n_seg + 1,),
        in_specs=[pl.BlockSpec((1024, D_MODEL), lambda i: (seg(i), 0)),
                  pl.BlockSpec((8, D_MODEL), lambda i: (D_MAIN // 8, 0)),
                  _const_spec(xs.shape), _const_spec(norm1.shape),
                  pl.BlockSpec((wout_rows, D_MODEL), lambda i: (i, 0))],
        out_specs=(pl.BlockSpec((D_MODEL, 1024), lambda i: (0, seg(i))),
                   pl.BlockSpec((D_MODEL, GATE_PAD), lambda i: (0, 0)),
                   pl.BlockSpec((n_tok, 1024), lambda i: (0, seg(i))),
                   pl.BlockSpec((wout_rows, D_MODEL), lambda i: (i, 0))),
        out_shape=(jax.ShapeDtypeStruct((D_MODEL, D_MAIN), BF16),
                   jax.ShapeDtypeStruct((D_MODEL, GATE_PAD), BF16),
                   jax.ShapeDtypeStruct((n_tok, D_MAIN), F32),
                   jax.ShapeDtypeStruct(w_out.shape[1:], BF16)),
        scratch_shapes=[pltpu.VMEM((n_tok, D_MODEL), BF16)],
        compiler_params=_params(("arbitrary",)),
        name="prep",
    )(w_in_t, w_in_t, xs, norm1, w_out[0])
    bias_row = jnp.pad(b_if[0], (0, GATE_PAD - 8)).reshape(1, GATE_PAD)
    wpool = w_pool[0].astype(BF16)
    pscale = pool_scale[0].reshape(1, D_POOL)
    mhln = mhln_w[0].reshape(1, D_MLSTM)
    normf = normf_w.reshape(1, D_MODEL)
    layer_w = LayerW(w_main, w_gate, bias_row, norm1, wpool, pscale, mhln, wout, normf)
    layer_specs = [_const_spec(a.shape) for a in layer_w]

    n_tiles = B * S // TT
    steps = n_tiles // STEP_TILES
    steps_per_seq = S // (STEP_TILES * TT)
    xp = x_prompt.reshape(B * S, D_MODEL)
    y_p, pool_p, c_p, n_p, m_p = pl.pallas_call(
        functools.partial(_prompt_kernel, steps_per_seq=steps_per_seq),
        grid=(steps,),
        in_specs=[_const_spec(meta_tokens.shape),
                  pl.BlockSpec((TT, D_MODEL), lambda s: (0, 0))]
                 + [pl.BlockSpec((TT, D_MODEL),
                                 lambda s, j=j: (jnp.minimum(STEP_TILES * s + 1 + j, n_tiles - 1), 0))
                    for j in range(STEP_TILES)]
                 + layer_specs,
        out_specs=(pl.BlockSpec((STEP_TILES * TT, D_MODEL), lambda s: (s, 0)),
                   pl.BlockSpec((1, POOL_BUF, B, D_POOL), lambda s: (0, 0, 0, 0)),
                   pl.BlockSpec((1, 1, N_HEADS, HEAD_DIM, HEAD_DIM),
                                lambda s: (0, s // steps_per_seq, 0, 0, 0)),
                   pl.BlockSpec((1, 1, N_HEADS, HEAD_DIM), lambda s: (0, s // steps_per_seq, 0, 0)),
                   pl.BlockSpec((B, 128), lambda s: (0, 0))),
        out_shape=(jax.ShapeDtypeStruct((B * S, D_MODEL), F32),
                   jax.ShapeDtypeStruct((1, POOL_BUF, B, D_POOL), F32),
                   jax.ShapeDtypeStruct((1, B, N_HEADS, HEAD_DIM, HEAD_DIM), F32),
                   jax.ShapeDtypeStruct((1, B, N_HEADS, HEAD_DIM), F32),
                   jax.ShapeDtypeStruct((B, 128), F32)),
        scratch_shapes=_tile_scratch(TT) + _tile_scratch(TT) + _tile_scratch(N_META)
        + _state_scratch() + _state_scratch(),
        compiler_params=_params(("arbitrary",)),
        name="prompt",
    )(meta_tokens, xp, *([xp] * STEP_TILES), *layer_w)
    y_prompt = y_p.reshape(B, S, D_MODEL)
    pool_p = jnp.swapaxes(pool_p, 1, 2)
    m_p = m_p[:, :N_HEADS].reshape(1, B, N_HEADS)

    G = SAMPLE_GROUP
    GT = G * SL
    pool_in = jnp.swapaxes(state_pool, 1, 2)
    sample_w = layer_w[1:]
    y_s, pool_s, c_s, n_s, m_s = pl.pallas_call(
        functools.partial(_sample_kernel, seq_len=SL, pos0=PAST_LEN),
        grid=(SB // G,),
        in_specs=[pl.BlockSpec((GT, D_MODEL), lambda i: (i, 0)),
                  pl.BlockSpec((GT, D_MAIN), lambda i: (i, 0)),
                  pl.BlockSpec((1, POOL_BUF, G, D_POOL), lambda i: (0, 0, i, 0)),
                  pl.BlockSpec((1, G, N_HEADS, HEAD_DIM, HEAD_DIM), lambda i: (0, i, 0, 0, 0)),
                  pl.BlockSpec((1, G, N_HEADS, HEAD_DIM), lambda i: (0, i, 0, 0)),
                  pl.BlockSpec((1, G, N_HEADS), lambda i: (0, i, 0))]
                 + [_const_spec(a.shape) for a in sample_w],
        out_specs=(pl.BlockSpec((GT, D_MODEL), lambda i: (i, 0)),
                   pl.BlockSpec((1, POOL_BUF, G, D_POOL), lambda i: (0, 0, i, 0)),
                   pl.BlockSpec((1, G, N_HEADS, HEAD_DIM, HEAD_DIM), lambda i: (0, i, 0, 0, 0)),
                   pl.BlockSpec((1, G, N_HEADS, HEAD_DIM), lambda i: (0, i, 0, 0)),
                   pl.BlockSpec((1, G, N_HEADS), lambda i: (0, i, 0))),
        out_shape=(jax.ShapeDtypeStruct((n_tok, D_MODEL), F32),
                   jax.ShapeDtypeStruct(pool_in.shape, F32),
                   jax.ShapeDtypeStruct(state_C.shape, F32),
                   jax.ShapeDtypeStruct(state_n.shape, F32),
                   jax.ShapeDtypeStruct(state_m.shape, F32)),
        scratch_shapes=[pltpu.VMEM((HIST + SL, D_POOL), F32), pltpu.VMEM((GT, 2048), BF16)],
        compiler_params=_params(("arbitrary",)),
        name="sample",
    )(xs, proj_s, pool_in, state_C, state_n, state_m, *sample_w)
    y_sample = y_s.reshape(SB, SL, D_MODEL)
    pool_s = jnp.swapaxes(pool_s, 1, 2)

    return (y_prompt, y_sample, pool_p, c_p, n_p, m_p, pool_s, c_s, n_s, m_s)
```

```python
import collections
import functools

import jax
import jax.numpy as jnp
from jax import lax
from jax.experimental import pallas as pl
from jax.experimental.pallas import tpu as pltpu

D_MODEL = 1024
D_POOL = 1024
D_MLSTM = 1024
N_HEADS = 4
HEAD_DIM = 256
POOL_WINDOWS = (2, 4, 8, 16)
POOL_BUF = 15
HIST = 16
N_META = 16
PAST_LEN = 16384
EPS = 1e-6
D_MAIN = 2 * D_POOL + 5 * D_MLSTM
GATE_PAD = 128
K_SCALE = HEAD_DIM ** -0.5

OFF_U, OFF_ZA, OFF_Q, OFF_K, OFF_V, OFF_O, OFF_ZB = (i * 1024 for i in range(7))

PROMPT_TILE = 256
STEP_TILES = 2
SAMPLE_GROUP = 8
VMEM_LIMIT = 60000 * 1024

F32 = jnp.float32
BF16 = jnp.bfloat16

LayerW = collections.namedtuple("LayerW", "main gate bias norm1 pool pscale mhln out normf")


def _dot(a, b):
    return jnp.dot(a, b, preferred_element_type=F32)


def _dot_nt(a, b):
    return lax.dot_general(a, b, (((1,), (1,)), ((), ())), preferred_element_type=F32)


def _dot_tn(a, b):
    return lax.dot_general(a, b, (((0,), (0,)), ((), ())), preferred_element_type=F32)


def _rms(x, w):
    return x * lax.rsqrt(jnp.mean(x * x, axis=-1, keepdims=True) + EPS) * w


def _log_sigmoid(x):
    return jnp.minimum(x, 0.0) - jnp.log1p(jnp.exp(-jnp.abs(x)))


def _silu(x):
    return x * jax.nn.sigmoid(x)


def _split3(x):
    hi = x.astype(BF16)
    r = x - hi.astype(F32)
    mid = r.astype(BF16)
    lo = (r - mid.astype(F32)).astype(BF16)
    return hi, mid, lo


def _seq_mask(T, seq_len):
    row = lax.broadcasted_iota(jnp.int32, (T, T), 0)
    col = lax.broadcasted_iota(jnp.int32, (T, T), 1)
    causal = col <= row
    if seq_len < T:
        shift = seq_len.bit_length() - 1
        assert 1 << shift == seq_len
        causal = causal & ((row >> shift) == (col >> shift))
    return causal


def _to_rows(cols):
    T = cols.shape[0]
    pad = -T % 128
    if pad:
        cols = jnp.concatenate([cols, jnp.zeros((pad, cols.shape[1]), cols.dtype)], axis=0)
    return cols.T[:, 0:T]


def _gate_pre(xn, w):
    return _dot(xn, w.gate[...]) + w.bias[...]


def _cumsum_operands(g_col, causal):
    tri = jnp.where(causal, 1.0, 0.0).astype(BF16)
    return tri, _split3(_log_sigmoid(g_col))


def _cumsum(tri, parts):
    return sum(_dot(tri, p) for p in parts)


def _intra_weights(qb, kb, b_col, a_col, r_row, causal):
    dm = jnp.where(causal, b_col + r_row, -jnp.inf)
    m_t = jnp.maximum(a_col, jnp.max(dm, axis=-1, keepdims=True))
    w = jnp.exp(dm - m_t)
    inter = jnp.exp(a_col - m_t)
    return m_t, inter, _dot_nt(qb, kb) * (w * K_SCALE)


def _head_norm(ht, w_row):
    mu = jnp.mean(ht, axis=-1, keepdims=True)
    d = ht - mu
    var = jnp.mean(d * d, axis=-1, keepdims=True)
    return d * lax.rsqrt(var + EPS) * w_row


def _window_sums(ext):
    s2 = ext + pltpu.roll(ext, 1, axis=0)
    s4 = s2[:, 256:] + pltpu.roll(s2[:, 256:], 2, axis=0)
    s8 = s4[:, 256:] + pltpu.roll(s4[:, 256:], 4, axis=0)
    s16 = s8[:, 256:] + pltpu.roll(s8[:, 256:], 8, axis=0)
    return [s2[HIST:, 0:256], s4[HIST:, 0:256], s8[HIST:, 0:256], s16[HIST:, 0:256]]


def _pooled(ext, u, pos_col):
    sums = _window_sums(ext)
    return [sums[g] * (1.0 / jnp.minimum(float(w), pos_col + 1.0)) - u[:, g * 256:(g + 1) * 256]
            for g, w in enumerate(POOL_WINDOWS)]


def _pooled_group(ext_g, u_g, pos_col, g):
    s = ext_g
    for k in range(g + 1):
        s = s + pltpu.roll(s, 2 ** k, axis=0)
    return s[HIST:] * (1.0 / jnp.minimum(float(POOL_WINDOWS[g]), pos_col + 1.0)) - u_g


def _pool_mix(pooled, wpool_ref):
    return jnp.concatenate([_dot(p.astype(BF16), wpool_ref[g]) for g, p in enumerate(pooled)], axis=-1)


def _inproj_steps(srcs, w):
    PROJ, X, GCOL, XN = 0, 1, 2, 4

    def norm():
        for x_ref, scr in srcs:
            x = x_ref[...]
            scr[X][...] = x
            scr[XN][...] = _rms(x, w.norm1[...]).astype(BF16)

    def xn_rows():
        parts = [scr[XN][...] for _, scr in srcs]
        return parts[0] if len(parts) == 1 else jnp.concatenate(parts, axis=0)

    def store(which, cols, val):
        r0 = 0
        for _, scr in srcs:
            rows = scr[X].shape[0]
            scr[which][:, cols] = val[r0:r0 + rows]
            r0 += rows

    def piece(c0):
        act = {OFF_ZA: _silu, OFF_O: jax.nn.sigmoid, OFF_ZB: _silu}.get(c0 // 1024 * 1024, lambda p: p)

        def run():
            store(PROJ, slice(c0, c0 + PIECE_COLS), act(_dot(xn_rows(), w.main[:, c0:c0 + PIECE_COLS])))
        return run

    def gates():
        store(GCOL, slice(None), _gate_pre(xn_rows(), w))

    return [norm] + [piece(c0) for c0 in range(0, D_MAIN, PIECE_COLS)] + [gates]


def _mid_steps(scr, pos0, w, state):
    proj_scr, x_scr, gcol_scr, ycat_scr, _ = scr
    hist_scr, c_scr, n_scr, m_scr = state
    T = x_scr.shape[0]

    def seg(off, h=None):
        if h is None:
            return proj_scr[:, off:off + 1024]
        return proj_scr[:, off + h * 256:off + (h + 1) * 256]

    causal = _seq_mask(T, T)
    g_col = gcol_scr[...]
    tri, lf_parts = _cumsum_operands(g_col, causal)
    yield
    b_col_all = _cumsum(tri, lf_parts)
    r_row_all = _to_rows(g_col - pltpu.roll(b_col_all, GATE_PAD - N_HEADS, axis=1))

    pos_col = (lax.broadcasted_iota(jnp.int32, (T, 1), 0) + pos0).astype(F32)
    for g in range(len(POOL_WINDOWS)):
        cols = slice(g * 256, (g + 1) * 256)
        u_g = seg(OFF_U, g)
        ext_g = jnp.concatenate([hist_scr[:, cols], u_g], axis=0)
        p = _pooled_group(ext_g, u_g, pos_col, g).astype(BF16)
        hist_scr[:, cols] = ext_g[T:T + HIST, :]
        if g == 0:
            yield
        ycat_scr[:, cols] = (_dot(p, w.pool[g]) * w.pscale[:, cols] * seg(OFF_ZA, g)).astype(BF16)

    def head(h):
        q, k, v = seg(OFF_Q, h), seg(OFF_K, h), seg(OFF_V, h)
        qb, kb, vb = q.astype(BF16), k.astype(BF16), v.astype(BF16)
        ig_col = g_col[:, h:h + 1]
        b_col = b_col_all[:, 4 + h:5 + h]
        r_row = r_row_all[h:h + 1, :]
        m_prev = m_scr[h:h + 1, 0:1]
        a_col = b_col + m_prev
        c_old = c_scr[h]
        n_old = n_scr[h:h + 1, :]
        q_c = _dot(qb, c_old.astype(BF16))
        m_t, inter, s = _intra_weights(qb, kb, b_col, a_col, r_row, causal)
        sb = s.astype(BF16)
        yield
        num = inter * q_c + _dot(sb, vb)
        qn = inter * jnp.sum(q * n_old, axis=-1, keepdims=True) + jnp.sum(s, axis=-1, keepdims=True)
        ht = num * (1.0 / jnp.maximum(jnp.abs(qn), jnp.exp(-m_t)))
        m_new = m_t[T - 1:T, :]
        b_last = b_col[T - 1:T, :]
        w_end = jnp.exp(b_last - b_col + ig_col - m_new) * K_SCALE
        decay = jnp.exp(b_last + m_prev - m_new)
        kw = k * w_end
        kwb = kw.astype(BF16)
        yield
        c_scr[h] = decay * c_old + _dot_tn(kwb, vb)
        n_scr[h:h + 1, :] = decay * n_old + jnp.sum(kw, axis=0, keepdims=True)
        m_scr[h:h + 1, :] = jnp.broadcast_to(m_new, (1, 128))
        hn = _head_norm(ht, w.mhln[:, h * 256:(h + 1) * 256])
        ycat_scr[:, 1024 + h * 256:1024 + (h + 1) * 256] = (hn * seg(OFF_O, h) * seg(OFF_ZB, h)).astype(BF16)

    for pair in range(0, N_HEADS, HEADS_IN_FLIGHT):
        running = [head(h) for h in range(pair, pair + HEADS_IN_FLIGHT)]
        while running:
            alive = []
            for g in running:
                if next(g, g) is not g:
                    alive.append(g)
                yield
            running = alive


def _out_steps(scr, w, y_ref):
    _, x_scr, _, ycat_scr, _ = scr

    def residual():
        y_ref[...] = x_scr[...]

    def piece(c0):
        def run():
            cols = slice(c0, c0 + OUT_COLS)
            y_ref[:, cols] = y_ref[:, cols] + _dot(ycat_scr[...], w.out[:, cols])
        return run

    def norm():
        y_ref[...] = _rms(y_ref[...], w.normf[...])

    return [residual] + [piece(c0) for c0 in range(0, D_MODEL, OUT_COLS)] + [norm]


PIECE_COLS = 256
OUT_COLS = 256
HEADS_IN_FLIGHT = 2
MID_YIELDS = 2 + 3 * N_HEADS


def _run_interleaved(mid, lead, at_yield):
    assert len(at_yield) == MID_YIELDS
    for piece in lead:
        piece()
    for pieces in at_yield:
        next(mid)
        for piece in pieces:
            piece()
    for _ in mid:
        raise AssertionError("unexpected extra yield")


def _spread(pieces, yields):
    base, extra = divmod(len(pieces), yields)
    groups, start = [], 0
    for i in range(yields):
        stop = start + base + (i < extra)
        groups.append(pieces[start:stop])
        start = stop
    return groups


def _m_row(m_scr):
    lane = lax.broadcasted_iota(jnp.int32, (1, 128), 1)
    row = jnp.zeros((1, 128), F32)
    for h in range(N_HEADS):
        row = jnp.where(lane == h, m_scr[h:h + 1, :], row)
    return row


def _prompt_kernel(meta_ref, x0_ref, *refs, steps_per_seq):
    x_next = refs[:STEP_TILES]
    refs = refs[STEP_TILES:]
    w = LayerW(*refs[:9])
    y_ref, pool_out, c_out, n_out, m_out = refs[9:14]
    scr = (refs[14:19], refs[19:24])
    scr0, scr_meta = scr[0], refs[24:29]
    state, state_meta = refs[29:33], refs[33:37]
    hist_scr, c_scr, n_scr, m_scr = state
    s = pl.program_id(0)
    TT = PROMPT_TILE

    @pl.when(s == 0)
    def _():
        for ref in state_meta:
            ref[...] = jnp.zeros_like(ref)
        for p in _inproj_steps([(meta_ref, scr_meta), (x0_ref, scr0)], w):
            p()
        for _ in _mid_steps(scr_meta, 0, w, state_meta):
            pass

    @pl.when(s % steps_per_seq == 0)
    def _():
        for ref, ref_meta in zip(state, state_meta):
            ref[...] = ref_meta[...]

    out_pieces = []
    for j in range(STEP_TILES):
        norm, *proj = _inproj_steps([(x_next[j], scr[1 - j % 2])], w)
        if out_pieces:
            residual, o0, o1, o2, o3, out_norm = out_pieces
            lead = [residual, o0, norm, o1]
            at_yield = [[o2], [o3]] + _spread(proj, MID_YIELDS - 2)
            at_yield[3] = at_yield[3] + [out_norm]
        else:
            lead = [norm, proj[0]]
            at_yield = _spread(proj[1:], MID_YIELDS)
        _run_interleaved(_mid_steps(scr[j % 2], N_META, w, state), lead, at_yield)
        out_pieces = _out_steps(scr[j % 2], w, y_ref.at[j * TT:(j + 1) * TT, :])
    for piece in out_pieces:
        piece()

    @pl.when(s % steps_per_seq == steps_per_seq - 1)
    def _():
        b = s // steps_per_seq
        for j in range(POOL_BUF):
            pool_out[0, j, pl.ds(b, 1), :] = hist_scr[1 + j:2 + j, :]
        c_out[0, 0] = c_scr[...]
        n_out[0, 0] = n_scr[0:N_HEADS, :]
        m_out[pl.ds(b, 1), :] = _m_row(m_scr)


def _prep_kernel(wt_ref, wgt_ref, xs_ref, norm1_ref, wout_ref, wmain_ref, wgate_ref, proj_ref, woutb_ref,
                 xn_scr, *, n_seg):
    i = pl.program_id(0)

    @pl.when(i == 0)
    def _():
        g = jnp.concatenate([wgt_ref[...], jnp.zeros((GATE_PAD - 8, D_MODEL), F32)], axis=0)
        wgate_ref[...] = g.T.astype(BF16)
        xn_scr[...] = _rms(xs_ref[...], norm1_ref[...]).astype(BF16)

    woutb_ref[...] = wout_ref[...].astype(BF16)

    @pl.when(i < n_seg)
    def _():
        wb = wt_ref[...].T.astype(BF16)
        wmain_ref[...] = wb
        proj_ref[...] = _dot(xn_scr[...], wb)


def _expand(seq_col, vals):
    out = None
    for i, val in enumerate(vals):
        pick = jnp.where(seq_col == i, val, 0.0)
        out = pick if out is None else out + pick
    return out


def _sample_kernel(x_ref, proj_ref, pool_ref, c_ref, n_ref, m_ref, *refs, seq_len, pos0):
    w = LayerW(None, *refs[:8])
    y_ref, pool_out, c_out, n_out, m_out = refs[8:13]
    ext_scr, ycat_scr = refs[13:15]
    G = SAMPLE_GROUP
    T = G * seq_len
    x = x_ref[...]
    xn = _rms(x, w.norm1[...]).astype(BF16)
    causal = _seq_mask(T, seq_len)
    g_col = _gate_pre(xn, w)
    b_col_all = _cumsum(*_cumsum_operands(g_col, causal))
    r_row_all = _to_rows(g_col - pltpu.roll(b_col_all, GATE_PAD - N_HEADS, axis=1))
    seq_col = lax.broadcasted_iota(jnp.int32, (T, 1), 0) >> (seq_len.bit_length() - 1)
    pos_col = jnp.full((seq_len, 1), float(pos0), F32) + lax.broadcasted_iota(
        jnp.int32, (seq_len, 1), 0).astype(F32)

    pooled_rows = []
    for i in range(G):
        rows = slice(i * seq_len, (i + 1) * seq_len)
        u_i = proj_ref[rows, OFF_U:OFF_U + 1024]
        ext_scr[0:1, :] = jnp.zeros((1, 1024), F32)
        for j in range(POOL_BUF):
            ext_scr[1 + j:2 + j, :] = pool_ref[0, j, i:i + 1, :]
        ext_scr[HIST:HIST + seq_len, :] = u_i
        ext = ext_scr[...]
        pooled_rows.append(_pooled(ext, u_i, pos_col))
        for j in range(POOL_BUF):
            pool_out[0, j, i:i + 1, :] = ext[seq_len + 1 + j:seq_len + 2 + j, :]
    mixed = _pool_mix([jnp.concatenate(p, axis=0) for p in zip(*pooled_rows)], w.pool)
    y_a = mixed * w.pscale[...] * _silu(proj_ref[:, OFF_ZA:OFF_ZA + 1024])
    ycat_scr[:, 0:1024] = y_a.astype(BF16)

    lane4 = lax.broadcasted_iota(jnp.int32, (1, N_HEADS), 1)
    last = [(i + 1) * seq_len - 1 for i in range(G)]
    m_new_heads = [None] * N_HEADS

    def head(h):
        hc = slice(h * 256, (h + 1) * 256)
        q = proj_ref[:, OFF_Q + h * 256:OFF_Q + (h + 1) * 256]
        k = proj_ref[:, OFF_K + h * 256:OFF_K + (h + 1) * 256]
        v = proj_ref[:, OFF_V + h * 256:OFF_V + (h + 1) * 256]
        qb, kb, vb = q.astype(BF16), k.astype(BF16), v.astype(BF16)
        c_old = [c_ref[0, i, h] for i in range(G)]
        c_old_b = [c.astype(BF16) for c in c_old]
        yield
        q_cs = [_dot(qb, cb) for cb in c_old_b]
        ig_col = g_col[:, h:h + 1]
        b_col = b_col_all[:, 4 + h:5 + h]
        r_row = r_row_all[h:h + 1, :]
        m_prev = [m_ref[0, i:i + 1, h:h + 1] for i in range(G)]
        a_col = b_col + _expand(seq_col, m_prev)
        m_t, inter, s = _intra_weights(qb, kb, b_col, a_col, r_row, causal)
        sb = s.astype(BF16)
        yield
        sv = _dot(sb, vb)
        n_old = [n_ref[0, i, h:h + 1, :] for i in range(G)]
        num = inter * _expand(seq_col, q_cs) + sv
        qn = inter * jnp.sum(q * _expand(seq_col, n_old), axis=-1, keepdims=True) + jnp.sum(
            s, axis=-1, keepdims=True)
        ht = num * (1.0 / jnp.maximum(jnp.abs(qn), jnp.exp(-m_t)))
        m_new = [m_t[r:r + 1, :] for r in last]
        b_last = [b_col[r:r + 1, :] for r in last]
        w_end = jnp.exp(_expand(seq_col, b_last) - b_col + ig_col - _expand(seq_col, m_new)) * K_SCALE
        kw = k * w_end
        kwb = kw.astype(BF16)
        v_seq = [jnp.where(seq_col == i, vb, jnp.zeros_like(vb)) for i in range(G)]
        yield
        updates = [_dot_tn(kwb, v_i) for v_i in v_seq]
        for i in range(G):
            decay = jnp.exp(b_last[i] + m_prev[i] - m_new[i])
            c_out[0, i, h] = decay * c_old[i] + updates[i]
            n_out[0, i, h:h + 1, :] = decay * n_old[i] + jnp.sum(
                kw[i * seq_len:(i + 1) * seq_len, :], axis=0, keepdims=True)
        m_new_heads[h] = m_new
        hn = _head_norm(ht, w.mhln[:, hc])
        o = proj_ref[:, OFF_O + h * 256:OFF_O + (h + 1) * 256]
        zb = proj_ref[:, OFF_ZB + h * 256:OFF_ZB + (h + 1) * 256]
        ycat_scr[:, 1024 + h * 256:1024 + (h + 1) * 256] = (hn * jax.nn.sigmoid(o) * _silu(zb)).astype(BF16)

    running = [head(h) for h in range(N_HEADS)]
    while running:
        running = [g for g in running if next(g, g) is not g]
    for i in range(G):
        row = jnp.zeros((1, N_HEADS), F32)
        for h in range(N_HEADS):
            row = jnp.where(lane4 == h, m_new_heads[h][i], row)
        m_out[0, i:i + 1, :] = row

    y = _dot(ycat_scr[...], w.out[...])
    y_ref[...] = _rms(x + y, w.normf[...])


def _const_spec(shape):
    nd = len(shape)
    return pl.BlockSpec(shape, lambda *_: (0,) * nd, pipeline_mode=pl.Buffered(1))


def _params(sem):
    return pltpu.CompilerParams(dimension_semantics=sem, vmem_limit_bytes=VMEM_LIMIT)


def _tile_scratch(T):
    return [pltpu.VMEM((T, D_MAIN), F32), pltpu.VMEM((T, D_MODEL), F32), pltpu.VMEM((T, GATE_PAD), F32),
            pltpu.VMEM((T, 2048), BF16), pltpu.VMEM((T, D_MODEL), BF16)]


def _state_scratch():
    return [pltpu.VMEM((HIST, D_POOL), F32), pltpu.VMEM((N_HEADS, HEAD_DIM, HEAD_DIM), F32),
            pltpu.VMEM((8, HEAD_DIM), F32), pltpu.VMEM((8, 128), F32)]


def kernel(x_prompt, x_sample, state_pool, state_C, state_n, state_m, meta_tokens, norm1_w, w_in,
           b_if, w_pool, pool_scale, mhln_w, w_out, normf_w):
    B, S, _ = x_prompt.shape
    SB, SL, _ = x_sample.shape
    TT = PROMPT_TILE
    assert norm1_w.shape[0] == 1, "single layer"
    assert STEP_TILES % 2 == 0 and S % (STEP_TILES * TT) == 0 and SB % SAMPLE_GROUP == 0

    w_in_t = jnp.swapaxes(w_in[0], 0, 1)
    norm1 = norm1_w[0].reshape(1, D_MODEL)
    n_tok = SB * SL
    xs = x_sample.reshape(n_tok, D_MODEL)
    n_seg = D_MAIN // 1024
    wout_rows = w_out.shape[1] // (n_seg + 1)
    seg = lambda i: jnp.minimum(i, n_seg - 1)
    w_main, w_gate, proj_s, wout = pl.pallas_call(
        functools.partial(_prep_kernel, n_seg=n_seg),
        grid=(n_seg + 1,),
        in_specs=[pl.BlockSpec((1024, D_MODEL), lambda i: (seg(i), 0)),
                  pl.BlockSpec((8, D_MODEL), lambda i: (D_MAIN // 8, 0)),
                  _const_spec(xs.shape), _const_spec(norm1.shape),
                  pl.BlockSpec((wout_rows, D_MODEL), lambda i: (i, 0))],
        out_specs=(pl.BlockSpec((D_MODEL, 1024), lambda i: (0, seg(i))),
                   pl.BlockSpec((D_MODEL, GATE_PAD), lambda i: (0, 0)),
                   pl.BlockSpec((n_tok, 1024), lambda i: (0, seg(i))),
                   pl.BlockSpec((wout_rows, D_MODEL), lambda i: (i, 0))),
        out_shape=(jax.ShapeDtypeStruct((D_MODEL, D_MAIN), BF16),
                   jax.ShapeDtypeStruct((D_MODEL, GATE_PAD), BF16),
                   jax.ShapeDtypeStruct((n_tok, D_MAIN), F32),
                   jax.ShapeDtypeStruct(w_out.shape[1:], BF16)),
        scratch_shapes=[pltpu.VMEM((n_tok, D_MODEL), BF16)],
        compiler_params=_params(("arbitrary",)),
        name="prep",
    )(w_in_t, w_in_t, xs, norm1, w_out[0])
    bias_row = jnp.pad(b_if[0], (0, GATE_PAD - 8)).reshape(1, GATE_PAD)
    wpool = w_pool[0].astype(BF16)
    pscale = pool_scale[0].reshape(1, D_POOL)
    mhln = mhln_w[0].reshape(1, D_MLSTM)
    normf = normf_w.reshape(1, D_MODEL)
    layer_w = LayerW(w_main, w_gate, bias_row, norm1, wpool, pscale, mhln, wout, normf)
    layer_specs = [_const_spec(a.shape) for a in layer_w]

    n_tiles = B * S // TT
    steps = n_tiles // STEP_TILES
    steps_per_seq = S // (STEP_TILES * TT)
    xp = x_prompt.reshape(B * S, D_MODEL)
    y_p, pool_p, c_p, n_p, m_p = pl.pallas_call(
        functools.partial(_prompt_kernel, steps_per_seq=steps_per_seq),
        grid=(steps,),
        in_specs=[_const_spec(meta_tokens.shape),
                  pl.BlockSpec((TT, D_MODEL), lambda s: (0, 0))]
                 + [pl.BlockSpec((TT, D_MODEL),
                                 lambda s, j=j: (jnp.minimum(STEP_TILES * s + 1 + j, n_tiles - 1), 0))
                    for j in range(STEP_TILES)]
                 + layer_specs,
        out_specs=(pl.BlockSpec((STEP_TILES * TT, D_MODEL), lambda s: (s, 0)),
                   pl.BlockSpec((1, POOL_BUF, B, D_POOL), lambda s: (0, 0, 0, 0)),
                   pl.BlockSpec((1, 1, N_HEADS, HEAD_DIM, HEAD_DIM),
                                lambda s: (0, s // steps_per_seq, 0, 0, 0)),
                   pl.BlockSpec((1, 1, N_HEADS, HEAD_DIM), lambda s: (0, s // steps_per_seq, 0, 0)),
                   pl.BlockSpec((B, 128), lambda s: (0, 0))),
        out_shape=(jax.ShapeDtypeStruct((B * S, D_MODEL), F32),
                   jax.ShapeDtypeStruct((1, POOL_BUF, B, D_POOL), F32),
                   jax.ShapeDtypeStruct((1, B, N_HEADS, HEAD_DIM, HEAD_DIM), F32),
                   jax.ShapeDtypeStruct((1, B, N_HEADS, HEAD_DIM), F32),
                   jax.ShapeDtypeStruct((B, 128), F32)),
        scratch_shapes=_tile_scratch(TT) + _tile_scratch(TT) + _tile_scratch(N_META)
        + _state_scratch() + _state_scratch(),
        compiler_params=_params(("arbitrary",)),
        name="prompt",
    )(meta_tokens, xp, *([xp] * STEP_TILES), *layer_w)
    y_prompt = y_p.reshape(B, S, D_MODEL)
    pool_p = jnp.swapaxes(pool_p, 1, 2)
    m_p = m_p[:, :N_HEADS].reshape(1, B, N_HEADS)

    G = SAMPLE_GROUP
    GT = G * SL
    pool_in = jnp.swapaxes(state_pool, 1, 2)
    sample_w = layer_w[1:]
    y_s, pool_s, c_s, n_s, m_s = pl.pallas_call(
        functools.partial(_sample_kernel, seq_len=SL, pos0=PAST_LEN),
        grid=(SB // G,),
        in_specs=[pl.BlockSpec((GT, D_MODEL), lambda i: (i, 0)),
                  pl.BlockSpec((GT, D_MAIN), lambda i: (i, 0)),
                  pl.BlockSpec((1, POOL_BUF, G, D_POOL), lambda i: (0, 0, i, 0)),
                  pl.BlockSpec((1, G, N_HEADS, HEAD_DIM, HEAD_DIM), lambda i: (0, i, 0, 0, 0)),
                  pl.BlockSpec((1, G, N_HEADS, HEAD_DIM), lambda i: (0, i, 0, 0)),
                  pl.BlockSpec((1, G, N_HEADS), lambda i: (0, i, 0))]
                 + [_const_spec(a.shape) for a in sample_w],
        out_specs=(pl.BlockSpec((GT, D_MODEL), lambda i: (i, 0)),
                   pl.BlockSpec((1, POOL_BUF, G, D_POOL), lambda i: (0, 0, i, 0)),
                   pl.BlockSpec((1, G, N_HEADS, HEAD_DIM, HEAD_DIM), lambda i: (0, i, 0, 0, 0)),
                   pl.BlockSpec((1, G, N_HEADS, HEAD_DIM), lambda i: (0, i, 0, 0)),
                   pl.BlockSpec((1, G, N_HEADS), lambda i: (0, i, 0))),
        out_shape=(jax.ShapeDtypeStruct((n_tok, D_MODEL), F32),
                   jax.ShapeDtypeStruct(pool_in.shape, F32),
                   jax.ShapeDtypeStruct(state_C.shape, F32),
                   jax.ShapeDtypeStruct(state_n.shape, F32),
                   jax.ShapeDtypeStruct(state_m.shape, F32)),
        scratch_shapes=[pltpu.VMEM((HIST + SL, D_POOL), F32), pltpu.VMEM((GT, 2048), BF16)],
        compiler_params=_params(("arbitrary",)),
        name="sample",
    )(xs, proj_s, pool_in, state_C, state_n, state_m, *sample_w)
    y_sample = y_s.reshape(SB, SL, D_MODEL)
    pool_s = jnp.swapaxes(pool_s, 1, 2)

    return (y_prompt, y_sample, pool_p, c_p, n_p, m_p, pool_s, c_s, n_s, m_s)
```
